```python
import jax, jax.numpy as jnp
from jax import lax
import numpy as np

D_MODEL = 2048
BATCH = 1
SEQ = 8192
DEPTH = 2
DEC_BATCH = 32
DEC_SEQ = 8
PAST_LEN = 8192
PAGE_SIZE = 128

N_BRANCH = 4
BRANCH_W = D_MODEL // N_BRANCH
A_HD = 64
A_HEADS = BRANCH_W // A_HD
A_W = BRANCH_W
A_LORA_W = 96
A_LORA_A = 96
A_LORA_G = 256
RWKV_GN_EPS = 64e-5
B_HEADS = 4
B_HD = BRANCH_W // B_HEADS
B_W = BRANCH_W
FOX_QBLOCK = 128
C_HEADS = 4
C_HD = BRANCH_W // C_HEADS
C_W = BRANCH_W
MOBA_BLOCK = 256
MOBA_TOPK = 3
MOBA_QBLOCK = 64
D_HEADS = 4
D_HD = BRANCH_W // D_HEADS
D_W = BRANCH_W
RET_CHUNK = 128
RET_GN_EPS = 1e-6
ROPE_BASE = 10000.0
FFN_HIDDEN = -(-8 * D_MODEL // (3 * 256)) * 256
ALPHA = (2 * DEPTH) ** 0.25
BETA = (8 * DEPTH) ** -0.25

A_COLS = 3 * A_W + A_LORA_W + A_LORA_A + A_LORA_G
B_COLS = 3 * B_W + B_HEADS
C_COLS = 3 * C_W
D_COLS = 4 * D_W
GATE_COLS = N_BRANCH * D_MODEL
IN_COLS = A_COLS + B_COLS + C_COLS + D_COLS + GATE_COLS
IN_SPLITS = (A_COLS, A_COLS + B_COLS, A_COLS + B_COLS + C_COLS, A_COLS + B_COLS + C_COLS + D_COLS)
A_SPLITS = (A_W, 2 * A_W, 3 * A_W, 3 * A_W + A_LORA_W, 3 * A_W + A_LORA_W + A_LORA_A)
B_SPLITS = (B_W, 2 * B_W, 3 * B_W)
C_SPLITS = (C_W, 2 * C_W)
D_SPLITS = (D_W, 2 * D_W, 3 * D_W)
F32 = jnp.float32

kernel_name = 'hybrid_rwkv7_fox_moba_retnet_decode_step'


def _layernorm(x, eps=1e-5):
    xf = x.astype(F32)
    mu = jnp.mean(xf, axis=-1, keepdims=True)
    var = jnp.mean(jnp.square(xf - mu), axis=-1, keepdims=True)
    return ((xf - mu) * lax.rsqrt(var + eps)).astype(x.dtype)


def _headnorm(y, eps):
    yn = _layernorm(y, eps)
    return yn.reshape(y.shape[0], y.shape[1], -1)


def _rotary(x, pos):
    half = x.shape[-1] // 2
    inv = 1.0 / (ROPE_BASE ** (jnp.arange(half, dtype=F32) / half))
    ang = pos.astype(F32)[:, None] * inv[None, :]
    cos = jnp.cos(ang)[None, :, None, :]
    sin = jnp.sin(ang)[None, :, None, :]
    xf = x.astype(F32)
    x1, x2 = xf[..., :half], xf[..., half:]
    return jnp.concatenate([x1 * cos - x2 * sin, x1 * sin + x2 * cos], axis=-1).astype(x.dtype)


def _gather_past(cache_l, page_table):
    g = cache_l[page_table]
    return g.reshape((g.shape[0], g.shape[1] * g.shape[2]) + g.shape[3:])


def _rwkv7(za, shift_prev, S0, lp):
    B, T, _ = za.shape
    prev = jnp.concatenate([shift_prev[:, None].astype(za.dtype), za[:, :-1]], axis=1)
    zm = za + lp['rwkv_mu'] * (prev - za)
    r, k, v, wl, al, gl = jnp.split(zm, A_SPLITS, axis=-1)
    w = -jax.nn.softplus(-(lp['rwkv_w0'] + jnp.tanh(wl) @ lp['rwkv_w2'])) - 0.5
    decay = jnp.exp(-jnp.exp(w.astype(F32)))
    a = jax.nn.sigmoid(lp['rwkv_a0'] + al @ lp['rwkv_a2'])
    g = jax.nn.sigmoid(gl) @ lp['rwkv_g2']
    heads = lambda t: t.reshape(B, T, A_HEADS, A_HD)
    kk = heads(k * lp['rwkv_k_k']).astype(F32)
    kk = kk / jnp.maximum(jnp.linalg.norm(kk, axis=-1, keepdims=True), 1e-12)
    k = k * (1 + (a - 1) * lp['rwkv_k_a'])
    r_h, k_h, v_h, a_h, w_h = heads(r), heads(k), heads(v), heads(a), heads(decay)

    def step(S, inp):
        r_t, w_t, k_t, v_t, kk_t, a_t = inp
        S = (S * w_t[:, :, None, :]
             + jnp.einsum('bhvk,bhk->bhv', S, -kk_t)[..., None] * (kk_t * a_t)[:, :, None, :]
             + v_t[..., None] * k_t[:, :, None, :])
        return S, jnp.einsum('bhvk,bhk->bhv', S, r_t)

    xs = tuple(jnp.moveaxis(t.astype(F32), 1, 0) for t in (r_h, w_h, k_h, v_h, kk, a_h))
    S_fin, y = lax.scan(step, S0.astype(F32), xs)
    y = jnp.moveaxis(y, 0, 1).astype(za.dtype)
    y = _headnorm(y, RWKV_GN_EPS) * lp['rwkv_ln_g'] + lp['rwkv_ln_b']
    bonus = jnp.sum(r_h * k_h * lp['rwkv_r_k'], axis=-1, keepdims=True) * v_h
    y = (y + bonus.reshape(B, T, A_W)) * g
    return y, S_fin.astype(za.dtype), za[:, -1]


def _fox_attend(q, k, v, cum):
    B, Tq, H, hd = q.shape
    Tk = k.shape[1]
    p0 = Tk - Tq
    qb = FOX_QBLOCK if Tq % FOX_QBLOCK == 0 else Tq
    nblk = Tq // qb
    cum_k = jnp.transpose(cum, (0, 2, 1))
    cum_q = cum_k[:, :, p0:]
    kpos = jnp.arange(Tk)
    scale = hd ** -0.5

    def block(i):
        qs = lax.dynamic_slice_in_dim(q, i * qb, qb, axis=1)
        cq = lax.dynamic_slice_in_dim(cum_q, i * qb, qb, axis=2)
        qpos = p0 + i * qb + jnp.arange(qb)
        s = jnp.einsum('bqhd,bkhd->bhqk', qs, k).astype(F32) * scale
        s = s + cq[..., None] - cum_k[:, :, None, :]
        s = jnp.where(kpos[None, :] <= qpos[:, None], s, -jnp.inf)
        p = jax.nn.softmax(s, axis=-1).astype(v.dtype)
        return jnp.einsum('bhqk,bkhd->bqhd', p, v)

    o = lax.map(block, jnp.arange(nblk))
    return jnp.moveaxis(o, 0, 1).reshape(B, Tq, H, hd)


def _fox(zb, k_past, v_past, lf_past, lp):
    B, T, _ = zb.shape
    q, k, v, fl = jnp.split(zb, B_SPLITS, axis=-1)
    heads = lambda t: t.reshape(B, T, B_HEADS, B_HD)
    q, k, v = heads(q), heads(k), heads(v)
    logf = jax.nn.log_sigmoid((fl + lp['fox_bf']).astype(F32))
    k_all = jnp.concatenate([k_past, k], axis=1)
    v_all = jnp.concatenate([v_past, v], axis=1)
    cum = jnp.cumsum(jnp.concatenate([lf_past.astype(F32), logf], axis=1), axis=1)
    o = _fox_attend(q, k_all, v_all, cum)
    return o.reshape(B, T, B_W), k, v, logf.astype(zb.dtype)


def _moba_attend(q, kp, vp, Tk):
    B, Tq, H, hd = q.shape
    p0 = Tk - Tq
    nb = kp.shape[1] // MOBA_BLOCK
    kb = kp.reshape(B, nb, MOBA_BLOCK, H, hd)
    vb = vp.reshape(B, nb, MOBA_BLOCK, H, hd)
    kmean = jnp.mean(kb.astype(F32), axis=2)
    ktop = min(MOBA_TOPK, nb)
    qb = MOBA_QBLOCK if Tq % MOBA_QBLOCK == 0 else Tq
    nblk = Tq // qb
    bidx = jnp.arange(B)[:, None, None, None]
    hidx = jnp.arange(H)[None, None, :, None]
    offs = jnp.arange(MOBA_BLOCK)
    scale = hd ** -0.5

    def block(i):
        qs = lax.dynamic_slice_in_dim(q, i * qb, qb, axis=1)
        qpos = p0 + i * qb + jnp.arange(qb)
        own = qpos // MOBA_BLOCK
        sc = jnp.einsum('bqhd,bnhd->bqhn', qs.astype(F32), kmean)
        past_ok = jnp.arange(nb)[None, :] < own[:, None]
        sc = jnp.where(past_ok[None, :, None, :], sc, -jnp.inf)
        _, top_i = lax.top_k(sc, ktop)
        sel_ok = top_i < own[None, :, None, None]
        idx = jnp.concatenate([top_i, jnp.broadcast_to(own[None, :, None, None], (B, qb, H, 1))], axis=-1)
        ok = jnp.concatenate([sel_ok, jnp.ones((B, qb, H, 1), dtype=bool)], axis=-1)
        kg = kb[bidx, idx, :, hidx, :]
        vg = vb[bidx, idx, :, hidx, :]
        s = jnp.einsum('bqhd,bqhnkd->bqhnk', qs, kg).astype(F32) * scale
        kpos = idx[..., None] * MOBA_BLOCK + offs
        mask = ok[..., None] & (kpos <= qpos[None, :, None, None, None])
        s = jnp.where(mask, s, -jnp.inf)
        p = jax.nn.softmax(s.reshape(B, qb, H, -1), axis=-1).reshape(s.shape).astype(vg.dtype)
        return jnp.einsum('bqhnk,bqhnkd->bqhd', p, vg)

    o = lax.map(block, jnp.arange(nblk))
    return jnp.moveaxis(o, 0, 1).reshape(B, Tq, H, hd)


def _moba(zc, k_past, v_past):
    B, T, _ = zc.shape
    q, k, v = jnp.split(zc, C_SPLITS, axis=-1)
    heads = lambda t: t.reshape(B, T, C_HEADS, C_HD)
    q, k, v = heads(q), heads(k), heads(v)
    Tk = k_past.shape[1] + T
    pad = (-Tk) % MOBA_BLOCK
    zpad = jnp.zeros((B, pad, C_HEADS, C_HD), k.dtype)
    k_all = jnp.concatenate([k_past, k, zpad], axis=1)
    v_all = jnp.concatenate([v_past, v, zpad], axis=1)
    o = _moba_attend(q, k_all, v_all, Tk)
    return o.reshape(B, T, C_W), k, v


def _retention(zd, S0, pos0, lp):
    B, T, _ = zd.shape
    q, k, v, gd = jnp.split(zd, D_SPLITS, axis=-1)
    heads = lambda t: t.reshape(B, T, D_HEADS, D_HD)
    pos = pos0 + jnp.arange(T)
    q = _rotary(heads(q), pos)
    k = _rotary(heads(k), pos) * (D_HD ** -0.5)
    v = heads(v)
    C = RET_CHUNK if T % RET_CHUNK == 0 else T
    nc = T // C
    lg = jnp.log(1.0 - 2.0 ** (-5.0 - jnp.arange(D_HEADS, dtype=F32)))
    i = jnp.arange(C, dtype=F32)
    diff = i[:, None] - i[None, :]
    decay_in = jnp.where(diff[None] >= 0, jnp.exp(jnp.maximum(diff, 0.0)[None] * lg[:, None, None]), 0.0)
    q_dec = jnp.exp((i[None, :] + 1.0) * lg[:, None]).T
    k_dec = jnp.exp((C - 1.0 - i)[None, :] * lg[:, None])
    c_dec = jnp.exp(C * lg)
    chunks = lambda t: jnp.moveaxis(t.astype(F32).reshape(B, nc, C, D_HEADS, D_HD), 1, 0)

    def step(S, inp):
        qi, ki, vi = inp
        att = jnp.einsum('bihd,bjhd->bhij', qi, ki) * decay_in
        o = (jnp.einsum('bhij,bjhe->bihe', att, vi)
             + jnp.einsum('bihd,bhde->bihe', qi, S) * q_dec[None, :, :, None])
        S = S * c_dec[None, :, None, None] + jnp.einsum('bjhd,bjhe,hj->bhde', ki, vi, k_dec)
        return S, o

    S_fin, o = lax.scan(step, S0.astype(F32), (chunks(q), chunks(k), chunks(v)))
    o = jnp.moveaxis(o, 0, 1).reshape(B, T, D_HEADS, D_HD).astype(zd.dtype)
    y = _headnorm(o, RET_GN_EPS) * lp['ret_ln_g'] * jax.nn.silu(gd)
    return y, S_fin.astype(zd.dtype)


def _layer(x, c, lp, st):
    rwkv_S, rwkv_shift, ret_S, fox_k, fox_v, fox_lf, moba_k, moba_v = st
    B, T, _ = x.shape
    pos0 = fox_k.shape[1]
    ada = jax.nn.silu(c) @ lp['w_ada'] + lp['b_ada']
    sh1, sc1, g1, sh2, sc2, g2 = jnp.split(ada, 6, axis=-1)
    h = _layernorm(x) * (1 + sc1[:, None]) + sh1[:, None]
    z = h @ lp['w_in']
    za, zb, zc, zd, zg = jnp.split(z, IN_SPLITS, axis=-1)
    ya, rwkv_S_new, shift_new = _rwkv7(za, rwkv_shift, rwkv_S, lp)
    yb, fk, fv, flf = _fox(zb, fox_k, fox_v, fox_lf, lp)
    yc, mk, mv = _moba(zc, moba_k, moba_v)
    yd, ret_S_new = _retention(zd, ret_S, pos0, lp)
    br = jnp.stack([ya, yb, yc, yd], axis=2)
    up = jnp.einsum('btnc,ncd->btnd', br, lp['w_branch'])
    gates = jax.nn.sigmoid(zg.reshape(B, T, N_BRANCH, D_MODEL))
    mix = jnp.sum(gates * up, axis=2) @ lp['w_o']
    x = _layernorm(ALPHA * x + (1 + g1[:, None]) * mix) * lp['ln1_g'] + lp['ln1_b']
    h2 = _layernorm(x) * (1 + sc2[:, None]) + sh2[:, None]
    f = (jax.nn.silu(h2 @ lp['w_ffn_gate']) * (h2 @ lp['w_ffn_in'])) @ lp['w_ffn_out']
    x = _layernorm(ALPHA * x + (1 + g2[:, None]) * f) * lp['ln2_g'] + lp['ln2_b']
    return x, (fk, fv, flf, mk, mv, rwkv_S_new, shift_new, ret_S_new)


def setup_inputs(seed: int = 0) -> dict:
    key = jax.random.key(seed)
    keys = iter(jax.random.split(key, 64))

    def nrm(shape, scale=1.0):
        return jax.random.normal(next(keys), shape, jnp.float32) * scale

    def uni(shape, lo, hi):
        return jax.random.uniform(next(keys), shape, jnp.float32, lo, hi)

    n_pages = PAST_LEN // PAGE_SIZE
    n_used = DEC_BATCH * n_pages
    n_pool = n_used + (n_used + 3) // 4
    page_table = jax.random.permutation(next(keys), n_pool)[:n_used].reshape(DEC_BATCH, n_pages).astype(jnp.int32)
    ds = D_MODEL ** -0.5
    return {
        'x_prompt': nrm((BATCH, SEQ, D_MODEL)),
        'x_sample': nrm((DEC_BATCH, DEC_SEQ, D_MODEL)),
        'c_prompt': nrm((BATCH, D_MODEL)),
        'c_sample': nrm((DEC_BATCH, D_MODEL)),
        'page_table': page_table,
        'cache_fox_k': nrm((DEPTH, n_pool, PAGE_SIZE, B_HEADS, B_HD)),
        'cache_fox_v': nrm((DEPTH, n_pool, PAGE_SIZE, B_HEADS, B_HD)),
        'cache_fox_logf': jax.nn.log_sigmoid(2.5 + nrm((DEPTH, n_pool, PAGE_SIZE, B_HEADS))),
        'cache_moba_k': nrm((DEPTH, n_pool, PAGE_SIZE, C_HEADS, C_HD)),
        'cache_moba_v': nrm((DEPTH, n_pool, PAGE_SIZE, C_HEADS, C_HD)),
        'state_rwkv': nrm((DEPTH, DEC_BATCH, A_HEADS, A_HD, A_HD), 0.1),
        'state_rwkv_shift': nrm((DEPTH, DEC_BATCH, A_COLS)),
        'state_ret': nrm((DEPTH, DEC_BATCH, D_HEADS, D_HD, D_HD), 0.1),
        'w_ada': nrm((DEPTH, D_MODEL, 6 * D_MODEL), 0.5 * ds),
        'b_ada': nrm((DEPTH, 6 * D_MODEL), 0.01),
        'w_in': nrm((DEPTH, D_MODEL, IN_COLS), ds),
        'rwkv_mu': uni((DEPTH, A_COLS), 0.0, 1.0),
        'rwkv_w0': uni((DEPTH, A_W), -5.0, 1.0),
        'rwkv_w2': nrm((DEPTH, A_LORA_W, A_W), 0.1 * A_LORA_W ** -0.5),
        'rwkv_a0': nrm((DEPTH, A_W), 0.1),
        'rwkv_a2': nrm((DEPTH, A_LORA_A, A_W), 0.1 * A_LORA_A ** -0.5),
        'rwkv_g2': nrm((DEPTH, A_LORA_G, A_W), A_LORA_G ** -0.5),
        'rwkv_k_k': 0.85 + nrm((DEPTH, A_W), 0.02),
        'rwkv_k_a': 1.0 + nrm((DEPTH, A_W), 0.02),
        'rwkv_r_k': nrm((DEPTH, A_HEADS, A_HD), 0.1),
        'rwkv_ln_g': 1.0 + nrm((DEPTH, A_W), 0.02),
        'rwkv_ln_b': nrm((DEPTH, A_W), 0.01),
        'fox_bf': uni((DEPTH, B_HEADS), 1.0, 4.0),
        'ret_ln_g': 1.0 + nrm((DEPTH, D_W), 0.02),
        'w_branch': nrm((DEPTH, N_BRANCH, BRANCH_W, D_MODEL), BETA * BRANCH_W ** -0.5),
        'w_o': nrm((DEPTH, D_MODEL, D_MODEL), BETA * ds),
        'ln1_g': 1.0 + nrm((DEPTH, D_MODEL), 0.02),
        'ln1_b': nrm((DEPTH, D_MODEL), 0.01),
        'ln2_g': 1.0 + nrm((DEPTH, D_MODEL), 0.02),
        'ln2_b': nrm((DEPTH, D_MODEL), 0.01),
        'w_ffn_gate': nrm((DEPTH, D_MODEL, FFN_HIDDEN), ds),
        'w_ffn_in': nrm((DEPTH, D_MODEL, FFN_HIDDEN), ds),
        'w_ffn_out': nrm((DEPTH, FFN_HIDDEN, D_MODEL), BETA * FFN_HIDDEN ** -0.5),
    }


def reference(x_prompt, x_sample, c_prompt, c_sample, page_table,
              cache_fox_k, cache_fox_v, cache_fox_logf, cache_moba_k, cache_moba_v,
              state_rwkv, state_rwkv_shift, state_ret,
              w_ada, b_ada, w_in, rwkv_mu, rwkv_w0, rwkv_w2, rwkv_a0, rwkv_a2, rwkv_g2,
              rwkv_k_k, rwkv_k_a, rwkv_r_k, rwkv_ln_g, rwkv_ln_b, fox_bf, ret_ln_g,
              w_branch, w_o, ln1_g, ln1_b, ln2_g, ln2_b, w_ffn_gate, w_ffn_in, w_ffn_out):
    dt = x_prompt.dtype
    Bp = x_prompt.shape[0]
    xp, xs = x_prompt, x_sample
    new_p, new_s = [], []
    for l in range(DEPTH):
        lp = {'w_ada': w_ada[l], 'b_ada': b_ada[l], 'w_in': w_in[l], 'rwkv_mu': rwkv_mu[l],
              'rwkv_w0': rwkv_w0[l], 'rwkv_w2': rwkv_w2[l], 'rwkv_a0': rwkv_a0[l], 'rwkv_a2': rwkv_a2[l],
              'rwkv_g2': rwkv_g2[l], 'rwkv_k_k': rwkv_k_k[l], 'rwkv_k_a': rwkv_k_a[l], 'rwkv_r_k': rwkv_r_k[l],
              'rwkv_ln_g': rwkv_ln_g[l], 'rwkv_ln_b': rwkv_ln_b[l], 'fox_bf': fox_bf[l], 'ret_ln_g': ret_ln_g[l],
              'w_branch': w_branch[l], 'w_o': w_o[l], 'ln1_g': ln1_g[l], 'ln1_b': ln1_b[l],
              'ln2_g': ln2_g[l], 'ln2_b': ln2_b[l], 'w_ffn_gate': w_ffn_gate[l], 'w_ffn_in': w_ffn_in[l],
              'w_ffn_out': w_ffn_out[l]}
        st_p = (jnp.zeros((Bp, A_HEADS, A_HD, A_HD), dt), jnp.zeros((Bp, A_COLS), dt),
                jnp.zeros((Bp, D_HEADS, D_HD, D_HD), dt),
                jnp.zeros((Bp, 0, B_HEADS, B_HD), dt), jnp.zeros((Bp, 0, B_HEADS, B_HD), dt),
                jnp.zeros((Bp, 0, B_HEADS), dt),
                jnp.zeros((Bp, 0, C_HEADS, C_HD), dt), jnp.zeros((Bp, 0, C_HEADS, C_HD), dt))
        xp, out_p = _layer(xp, c_prompt, lp, st_p)
        st_s = (state_rwkv[l], state_rwkv_shift[l], state_ret[l],
                _gather_past(cache_fox_k[l], page_table), _gather_past(cache_fox_v[l], page_table),
                _gather_past(cache_fox_logf[l], page_table),
                _gather_past(cache_moba_k[l], page_table), _gather_past(cache_moba_v[l], page_table))
        xs, out_s = _layer(xs, c_sample, lp, st_s)
        new_p.append(out_p)
        new_s.append(out_s)
    fox_k_p, fox_v_p, fox_logf_p, moba_k_p, moba_v_p, rwkv_S_p, rwkv_shift_p, ret_S_p = [jnp.stack(t) for t in zip(*new_p)]
    fox_k_s, fox_v_s, fox_logf_s, moba_k_s, moba_v_s, rwkv_S_s, rwkv_shift_s, ret_S_s = [jnp.stack(t) for t in zip(*new_s)]
    return (xp, xs,
            fox_k_p, fox_v_p, fox_logf_p, moba_k_p, moba_v_p, rwkv_S_p, rwkv_shift_p, ret_S_p,
            fox_k_s, fox_v_s, fox_logf_s, moba_k_s, moba_v_s, rwkv_S_s, rwkv_shift_s, ret_S_s)
```

```python
import functools

import numpy as np
import jax
import jax.numpy as jnp
from jax import lax
from jax.experimental import pallas as pl
from jax.experimental.pallas import tpu as pltpu

F32 = jnp.float32
BF16 = jnp.bfloat16

D_MODEL = 2048
DEPTH = 2
PAGE_SIZE = 128
BRANCH_W = 512
A_HD = 64
A_HEADS = 8
A_LORA_W = 96
A_LORA_A = 96
A_LORA_G = 256
A_COLS = 3 * BRANCH_W + A_LORA_W + A_LORA_A + A_LORA_G
B_COLS = 3 * BRANCH_W + 4
C_COLS = 3 * BRANCH_W
D_COLS = 4 * BRANCH_W
HD = 128
NH = 4
RWKV_GN_EPS = 64e-5
RET_GN_EPS = 1e-6
LN_EPS = 1e-5
MOBA_BLOCK = 256
MOBA_TOPK = 3
RET_CHUNK = 128
RWKV_CHUNK = 64
ROPE_BASE = 10000.0
FFN_HIDDEN = 5632
ALPHA = (2 * DEPTH) ** 0.25

Z_G = 0
Z_A = 4 * D_MODEL
Z_D = Z_A + 2048
Z_C = Z_D + 2048
Z_B = Z_C + 1536
Z_COLS = Z_B + 1664

TM = 256
LANES = 128
NEG = -1e30
VMEM_LIMIT = 56 << 20

NN = (((1,), (0,)), ((), ()))
NT = (((1,), (1,)), ((), ()))
TN = (((0,), (0,)), ((), ()))


def _cp(sem):
    return pltpu.CompilerParams(dimension_semantics=sem, vmem_limit_bytes=VMEM_LIMIT)


def _dot(a, b, dims=NN):
    return lax.dot_general(a, b, dims, preferred_element_type=F32)


def _split2(x):
    hi = x.astype(BF16)
    return hi, (x - hi.astype(F32)).astype(BF16)


def _split3(x):
    hi = x.astype(BF16)
    r1 = x - hi.astype(F32)
    mid = r1.astype(BF16)
    return hi, mid, (r1 - mid.astype(F32)).astype(BF16)


def _dot3(a, b, dims=NN):
    ah, al = _split2(a)
    bh, bl = _split2(b)
    return _dot(ah, bh, dims) + (_dot(ah, bl, dims) + _dot(al, bh, dims))


def _dot_exact_l(m_bf16, x, n=3):
    parts = _split3(x) if n == 3 else _split2(x)
    acc = _dot(m_bf16, parts[0])
    for p in parts[1:]:
        acc = acc + _dot(m_bf16, p)
    return acc


def _dot_exact_r(x, m_bf16, n=3):
    parts = _split3(x) if n == 3 else _split2(x)
    acc = _dot(parts[0], m_bf16)
    for p in parts[1:]:
        acc = acc + _dot(p, m_bf16)
    return acc


def _ln(x, eps):
    mu = jnp.mean(x, axis=-1, keepdims=True)
    xc = x - mu
    var = jnp.mean(xc * xc, axis=-1, keepdims=True)
    return xc * lax.rsqrt(var + eps)


def _sigmoid(x):
    return 1.0 / (1.0 + jnp.exp(-x))


def _log_sigmoid(x):
    return jnp.minimum(x, 0.0) - jnp.log(1.0 + jnp.exp(-jnp.abs(x)))


def _iota(shape, dim):
    return lax.broadcasted_iota(jnp.int32, shape, dim)


def _ada_kernel(c_ref, w_ref, b_ref, o_ref):
    c = c_ref[...]
    s = (c * _sigmoid(c)).astype(BF16)
    o_ref[...] = _dot(s, w_ref[...].astype(BF16)) + b_ref[...]


def _ada_call(c_all, w_ada, b_ada, l):
    R, D = c_all.shape
    N = w_ada.shape[2]
    tn = 2048
    return pl.pallas_call(
        _ada_kernel, grid=(N // tn,),
        in_specs=[pl.BlockSpec((R, D), lambda j: (0, 0)),
                  pl.BlockSpec((None, D, tn), lambda j: (l, 0, j)),
                  pl.BlockSpec((None, 1, tn), lambda j: (l, 0, j))],
        out_specs=pl.BlockSpec((R, tn), lambda j: (0, j)),
        out_shape=jax.ShapeDtypeStruct((R, N), F32),
        compiler_params=_cp(("arbitrary",)), name="ada")(c_all, w_ada, b_ada)


def _lnmod_kernel(x_ref, scp_ref, shp_ref, scs_ref, shs_ref, o_ref, *, n_p):
    is_s = pl.program_id(0) >= n_p
    sc = jnp.where(is_s, scs_ref[...], scp_ref[0:1, :])
    sh = jnp.where(is_s, shs_ref[...], shp_ref[0:1, :])
    o_ref[...] = (_ln(x_ref[...], LN_EPS) * (1.0 + sc) + sh).astype(BF16)


def _mod_specs(n_p, cols):
    specs = []
    for cb in cols:
        specs.append(pl.BlockSpec((8, D_MODEL), lambda i, cb=cb: (0, cb)))
    for cb in cols:
        specs.append(pl.BlockSpec((TM, D_MODEL), lambda i, cb=cb: (jnp.maximum(i - n_p, 0), cb)))
    return specs


def _lnmod_call(x, ada_p, mod_s, n_p, sc_col, sh_col):
    M, D = x.shape
    return pl.pallas_call(
        functools.partial(_lnmod_kernel, n_p=n_p), grid=(M // TM,),
        in_specs=[pl.BlockSpec((TM, D), lambda i: (i, 0))] + _mod_specs(n_p, (sc_col, sh_col)),
        out_specs=pl.BlockSpec((TM, D), lambda i: (i, 0)),
        out_shape=jax.ShapeDtypeStruct((M, D), BF16),
        compiler_params=_cp(("arbitrary",)), name="lnmod")(x, ada_p, ada_p, mod_s, mod_s)


def _mm_kernel(a_ref, w_ref, o_ref, wb_ref):
    @pl.when(pl.program_id(1) == 0)
    def _():
        wb_ref[...] = w_ref[...].astype(BF16)

    o_ref[...] = _dot(a_ref[...].astype(BF16), wb_ref[...]).astype(o_ref.dtype)


def _mm_call(a, w, w_index, tm, tn, out_dtype, name):
    M, K = a.shape
    N = w.shape[-1]
    if w.ndim == 3:
        w_spec = pl.BlockSpec((None, K, tn), lambda j, i: (w_index, 0, j))
    else:
        w_spec = pl.BlockSpec((K, tn), lambda j, i: (0, j))
    return pl.pallas_call(
        _mm_kernel, grid=(N // tn, M // tm),
        in_specs=[pl.BlockSpec((tm, K), lambda j, i: (i, 0)), w_spec],
        out_specs=pl.BlockSpec((tm, tn), lambda j, i: (i, j)),
        out_shape=jax.ShapeDtypeStruct((M, N), out_dtype),
        scratch_shapes=[pltpu.VMEM((K, tn), BF16)],
        compiler_params=_cp(("arbitrary", "arbitrary")), name=name)(a, w)


def _resid_kernel(*refs, n_p, with_h):
    if with_h:
        (x_ref, mix_ref, gp_ref, scp_ref, shp_ref, gs_ref, scs_ref, shs_ref, lg_ref, lb_ref,
         x1_ref, h_ref) = refs
    else:
        x_ref, mix_ref, gp_ref, gs_ref, lg_ref, lb_ref, x1_ref = refs
    is_s = pl.program_id(0) >= n_p
    g = jnp.where(is_s, gs_ref[...], gp_ref[0:1, :])
    y = ALPHA * x_ref[...] + (1.0 + g) * mix_ref[...]
    x1 = _ln(y, LN_EPS) * lg_ref[...] + lb_ref[...]
    x1_ref[...] = x1
    if with_h:
        sc = jnp.where(is_s, scs_ref[...], scp_ref[0:1, :])
        sh = jnp.where(is_s, shs_ref[...], shp_ref[0:1, :])
        h_ref[...] = (_ln(x1, LN_EPS) * (1.0 + sc) + sh).astype(BF16)


def _resid_call(x, mix, ada_p, mod_s, ln_g, ln_b, l, n_p, g_col, sc_col=None, sh_col=None):
    M, D = x.shape
    with_h = sc_col is not None
    cols = (g_col, sc_col, sh_col) if with_h else (g_col,)
    row = pl.BlockSpec((TM, D), lambda i: (i, 0))
    vec = pl.BlockSpec((None, 1, D), lambda i: (l, 0, 0))
    in_specs = [row, row] + _mod_specs(n_p, cols) + [vec, vec]
    args = [x, mix] + [ada_p] * len(cols) + [mod_s] * len(cols) + [ln_g, ln_b]
    out_shape = [jax.ShapeDtypeStruct((M, D), F32)]
    out_specs = [row]
    if with_h:
        out_shape.append(jax.ShapeDtypeStruct((M, D), BF16))
        out_specs.append(row)
    res = pl.pallas_call(
        functools.partial(_resid_kernel, n_p=n_p, with_h=with_h), grid=(M // TM,),
        in_specs=in_specs, out_specs=out_specs, out_shape=out_shape,
        compiler_params=_cp(("arbitrary",)), name="resid_h" if with_h else "resid")(*args)
    return res if with_h else res[0]


def _ffn_kernel(h_ref, wg_ref, wi_ref, o_ref, wgb_ref, wib_ref):
    @pl.when(pl.program_id(1) == 0)
    def _():
        wgb_ref[...] = wg_ref[...].astype(BF16)
        wib_ref[...] = wi_ref[...].astype(BF16)

    h = h_ref[...]
    a = _dot(h, wgb_ref[...])
    b = _dot(h, wib_ref[...])
    o_ref[...] = (a * _sigmoid(a) * b).astype(BF16)


def _ffn_call(h, wg, wi, l, tm, tn):
    M, K = h.shape
    N = wg.shape[-1]
    w_spec = pl.BlockSpec((None, K, tn), lambda j, i: (l, 0, j))
    return pl.pallas_call(
        _ffn_kernel, grid=(N // tn, M // tm),
        in_specs=[pl.BlockSpec((tm, K), lambda j, i: (i, 0)), w_spec, w_spec],
        out_specs=pl.BlockSpec((tm, tn), lambda j, i: (i, j)),
        out_shape=jax.ShapeDtypeStruct((M, N), BF16),
        scratch_shapes=[pltpu.VMEM((K, tn), BF16), pltpu.VMEM((K, tn), BF16)],
        compiler_params=_cp(("arbitrary", "arbitrary")), name="ffn_act")(h, wg, wi)


def _merge_kernel(ya_ref, yb_ref, yc_ref, yd_ref, wb_ref, g0_ref, g1_ref, g2_ref, g3_ref, o_ref, wbb_ref):
    @pl.when(pl.program_id(1) == 0)
    def _():
        wbb_ref[...] = wb_ref[...].astype(BF16)

    acc = None
    for n, (y_ref, g_ref) in enumerate(((ya_ref, g0_ref), (yb_ref, g1_ref), (yc_ref, g2_ref), (yd_ref, g3_ref))):
        up = _dot(y_ref[...].astype(BF16), wbb_ref[n])
        t = _sigmoid(g_ref[...]) * up
        acc = t if acc is None else acc + t
    o_ref[...] = acc.astype(BF16)


def _merge_call(ys, w_branch, z, l, tm, tn):
    M = z.shape[0]
    W = BRANCH_W
    D = D_MODEL
    nj = D // tn
    y_spec = pl.BlockSpec((tm, W), lambda j, i: (i, 0))
    g_specs = [pl.BlockSpec((tm, tn), lambda j, i, n=n: (i, (Z_G + n * D) // tn + j)) for n in range(4)]
    return pl.pallas_call(
        _merge_kernel, grid=(nj, M // tm),
        in_specs=[y_spec] * 4 + [pl.BlockSpec((None, 4, W, tn), lambda j, i: (l, 0, 0, j))] + g_specs,
        out_specs=pl.BlockSpec((tm, tn), lambda j, i: (i, j)),
        out_shape=jax.ShapeDtypeStruct((M, D), BF16),
        scratch_shapes=[pltpu.VMEM((4, W, tn), BF16)],
        compiler_params=_cp(("arbitrary", "arbitrary")), name="merge")(*ys, w_branch, z, z, z, z)


def _rwkv_pre_kernel(za_ref, prev_ref, fill_ref, mu_ref, w0_ref, a0_ref, kk_ref, ka_ref, w2_ref, a2_ref,
                     g2_ref, e_ref, r_o, lw_o, k_o, v_o, kn_o, b_o, g_o, *, n_p, seq_s):
    i = pl.program_id(0)
    is_s = i >= n_p
    za = za_ref[...]
    rows = _iota(za.shape, 0)
    prev = jnp.where(rows == 0, prev_ref[7:8, :], pltpu.roll(za, 1, axis=0))
    base = jnp.where(is_s, n_p * TM, 0)
    pmask = jnp.where(is_s, seq_s - 1, 0x3FFFFFFF)
    start = ((rows + (i * TM - base)) & pmask) == 0
    fill = jnp.where(is_s, fill_ref[...], 0.0)
    prev = jnp.where(start, fill, prev)
    zm = za + mu_ref[...] * (prev - za)
    W = BRANCH_W
    r = zm[:, 0:W]
    k = zm[:, W:2 * W]
    v = zm[:, 2 * W:3 * W]
    x = zm[:, 3 * W:4 * W]
    wl = _dot(jnp.tanh(x).astype(BF16), w2_ref[...].astype(BF16))
    y = -(w0_ref[...] + wl)
    softplus = jnp.maximum(y, 0.0) + jnp.log(1.0 + jnp.exp(-jnp.abs(y)))
    w = -softplus - 0.5
    logw = -jnp.exp(w)
    a = _sigmoid(a0_ref[...] + _dot(x.astype(BF16), a2_ref[...].astype(BF16)))
    g = _dot(_sigmoid(x).astype(BF16), g2_ref[...].astype(BF16))
    kk0 = k * kk_ref[...]
    ss = _dot_exact_r(kk0 * kk0, e_ref[...], n=3)
    kn = kk0 / jnp.maximum(jnp.sqrt(ss), 1e-12)
    k2 = k * (1.0 + (a - 1.0) * ka_ref[...])
    b = kn * a
    for h in range(A_HEADS):
        sl = slice(h * A_HD, (h + 1) * A_HD)
        r_o[h] = r[:, sl]
        lw_o[h] = logw[:, sl]
        k_o[h] = k2[:, sl]
        v_o[h] = v[:, sl]
        kn_o[h] = kn[:, sl]
        b_o[h] = b[:, sl]
        g_o[h] = g[:, sl]


def _rwkv_pre_call(z, fill_s, mu, w0, a0, k_k, k_a, w2p, a2p, g2p, e64, n_p, seq_s):
    M = z.shape[0]
    W = BRANCH_W
    cb = Z_A // 2048
    vec = lambda n: pl.BlockSpec((1, n), lambda i: (0, 0))
    mat = pl.BlockSpec((W, W), lambda i: (0, 0))
    out = jax.ShapeDtypeStruct((A_HEADS, M, A_HD), F32)
    ospec = pl.BlockSpec((A_HEADS, TM, A_HD), lambda i: (0, i, 0))
    return pl.pallas_call(
        functools.partial(_rwkv_pre_kernel, n_p=n_p, seq_s=seq_s), grid=(M // TM,),
        in_specs=[pl.BlockSpec((TM, 2048), lambda i: (i, cb)),
                  pl.BlockSpec((8, 2048), lambda i: (jnp.maximum(i * (TM // 8) - 1, 0), cb)),
                  pl.BlockSpec((TM, 2048), lambda i: (jnp.maximum(i - n_p, 0), 0)),
                  vec(2048), vec(W), vec(W), vec(W), vec(W), mat, mat, mat, mat],
        out_specs=[ospec] * 7, out_shape=[out] * 7,
        compiler_params=_cp(("arbitrary",)), name="rwkv_pre")(
            z, z, fill_s, mu, w0, a0, k_k, k_a, w2p, a2p, g2p, e64)


def _rwkv_chunk_kernel(r_ref, lw_ref, k_ref, v_ref, kn_ref, b_ref, g_ref, s0_ref, rk_ref, lng_ref, lnb_ref,
                       y_ref, sfin_ref, S_ref, *, C, nc):
    c = pl.program_id(1)

    @pl.when(c == 0)
    def _():
        S_ref[...] = s0_ref[...]

    row = _iota((C, C), 0)
    col = _iota((C, C), 1)
    strict = row > col
    incl = row >= col
    tri = incl.astype(BF16)
    eye = (row == col).astype(F32)
    eye_k = _iota((A_HD, A_HD), 0) == _iota((A_HD, A_HD), 1)
    for h in range(A_HEADS):
        r = r_ref[h]
        lw = lw_ref[h]
        k = k_ref[h]
        v = v_ref[h]
        kn = kn_ref[h]
        b = b_ref[h]
        L = _dot_exact_l(tri, lw)
        Lx = L - lw
        Lend = L[C - 1:C, :]
        eL = jnp.exp(L)
        eN = jnp.exp(-L)
        eE = jnp.exp(Lend - L)
        KKg = kn * jnp.exp(Lx)
        Rg = r * eL
        Binv = b * eN
        Kinv = k * eN
        Bd = b * eE
        Kd = k * eE
        Abk = jnp.where(strict, _dot3(KKg, Binv, NT), 0.0)
        Akk = jnp.where(strict, _dot3(KKg, Kinv, NT), 0.0)
        Abr = jnp.where(incl, _dot3(Rg, Binv, NT), 0.0)
        Akr = jnp.where(incl, _dot3(Rg, Kinv, NT), 0.0)
        X = eye - Abk
        P = _dot3(Abk, Abk)
        n = 2
        while n < C:
            X = X + _dot3(X, P)
            n *= 2
            if n < C:
                P = _dot3(P, P)
        S0 = S_ref[h]
        U = _dot3(X, _dot3(KKg, S0) + _dot3(Akk, v))
        Y = _dot3(Rg, S0) - _dot3(Abr, U) + _dot3(Akr, v)
        gcol = jnp.sum(jnp.where(eye_k, jnp.broadcast_to(eL[C - 1:C, :], (A_HD, A_HD)), 0.0), axis=1, keepdims=True)
        S_ref[h] = S0 * gcol - _dot3(Bd, U, TN) + _dot3(Kd, v, TN)
        yn = _ln(Y, RWKV_GN_EPS) * lng_ref[h] + lnb_ref[h]
        bonus = jnp.sum(r * k * rk_ref[h], axis=1, keepdims=True) * v
        y_ref[:, h * A_HD:(h + 1) * A_HD] = (yn + bonus) * g_ref[h]

    @pl.when(c == nc - 1)
    def _():
        sfin_ref[...] = S_ref[...]


def _rwkv_chunk_call(pre, s0t, r_k, ln_g, ln_b, row0, B, T, C):
    nc = T // C
    blk0 = row0 // C
    in_spec = pl.BlockSpec((A_HEADS, C, A_HD), lambda b, c: (0, blk0 + b * nc + c, 0))
    hvec = pl.BlockSpec((A_HEADS, 1, A_HD), lambda b, c: (0, 0, 0))
    st = pl.BlockSpec((None, A_HEADS, A_HD, A_HD), lambda b, c: (b, 0, 0, 0))
    return pl.pallas_call(
        functools.partial(_rwkv_chunk_kernel, C=C, nc=nc), grid=(B, nc),
        in_specs=[in_spec] * 7 + [st, hvec, hvec, hvec],
        out_specs=[pl.BlockSpec((C, BRANCH_W), lambda b, c: (b * nc + c, 0)), st],
        out_shape=[jax.ShapeDtypeStruct((B * T, BRANCH_W), F32),
                   jax.ShapeDtypeStruct((B, A_HEADS, A_HD, A_HD), F32)],
        scratch_shapes=[pltpu.VMEM((A_HEADS, A_HD, A_HD), F32)],
        compiler_params=_cp(("arbitrary", "arbitrary")), name="rwkv_chunk_c%d" % C)(
            *pre, s0t, r_k, ln_g, ln_b)


def _foxcum_kernel(fl_ref, bf_ref, lf_ref, cum_ref, carry_ref):
    @pl.when(pl.program_id(0) == 0)
    def _():
        carry_ref[...] = jnp.zeros_like(carry_ref)

    lane = _iota((TM, LANES), 1)
    lf = jnp.where(lane < NH, _log_sigmoid(fl_ref[...] + bf_ref[...]), 0.0)
    tri = (_iota((TM, TM), 0) >= _iota((TM, TM), 1)).astype(BF16)
    cum = _dot_exact_l(tri, lf) + carry_ref[...]
    lf_ref[...] = lf
    cum_ref[...] = cum
    carry_ref[...] = cum[TM - 1:TM, :]


def _foxcum_call(z, bf):
    M = z.shape[0]
    cb = (Z_B + 3 * BRANCH_W) // LANES
    spec = pl.BlockSpec((TM, LANES), lambda i: (i, 0))
    return pl.pallas_call(
        _foxcum_kernel, grid=(M // TM,),
        in_specs=[pl.BlockSpec((TM, LANES), lambda i: (i, cb)), pl.BlockSpec((1, LANES), lambda i: (0, 0))],
        out_specs=[spec, spec], out_shape=[jax.ShapeDtypeStruct((M, LANES), F32)] * 2,
        scratch_shapes=[pltpu.VMEM((1, LANES), F32)],
        compiler_params=_cp(("arbitrary",)), name="fox_cum")(z, bf)


def _softmax_step(s, h, sl, v_bf, m_ref, l_ref, acc_ref):
    m_prev = m_ref[h]
    m_new = jnp.maximum(m_prev, jnp.max(s, axis=1, keepdims=True))
    alpha = jnp.exp(m_prev - m_new)
    p = jnp.exp(s - m_new)
    l_ref[h] = alpha * l_ref[h] + jnp.sum(p, axis=1, keepdims=True)
    acc_ref[:, sl] = alpha * acc_ref[:, sl] + _dot(p.astype(BF16), v_bf)
    m_ref[h] = m_new


def _attn_init(m_ref, l_ref, acc_ref):
    m_ref[...] = jnp.full(m_ref.shape, NEG, F32)
    l_ref[...] = jnp.zeros(l_ref.shape, F32)
    acc_ref[...] = jnp.zeros(acc_ref.shape, F32)


def _fox_kernel(q_ref, k_ref, v_ref, cq_ref, ck_ref, o_ref, m_ref, l_ref, acc_ref, *, scale):
    i = pl.program_id(0)
    j = pl.program_id(1)
    T = q_ref.shape[0]

    @pl.when(j == 0)
    def _():
        _attn_init(m_ref, l_ref, acc_ref)

    @pl.when(j <= i)
    def _():
        q = q_ref[...]
        k = k_ref[...]
        v = v_ref[...]
        cq = cq_ref[...]
        ck = ck_ref[...]
        causal = (_iota((T, T), 1) - _iota((T, T), 0)) <= jnp.where(j < i, 1 << 30, 0)
        for h in range(NH):
            sl = slice(h * HD, (h + 1) * HD)
            s = _dot(q[:, sl].astype(BF16), k[:, sl].astype(BF16), NT) * scale
            s = s + (cq[:, h:h + 1] - ck[h:h + 1, :])
            s = jnp.where(causal, s, NEG)
            _softmax_step(s, h, sl, v[:, sl].astype(BF16), m_ref, l_ref, acc_ref)

    @pl.when(j == i)
    def _():
        for h in range(NH):
            sl = slice(h * HD, (h + 1) * HD)
            o_ref[:, sl] = acc_ref[:, sl] / l_ref[h]


def _attn_scratch(T):
    return [pltpu.VMEM((NH, T, 1), F32), pltpu.VMEM((NH, T, 1), F32), pltpu.VMEM((T, NH * HD), F32)]


def _fox_call(z, cum, cum_t, Tp):
    T = MOBA_BLOCK
    W = BRANCH_W
    nq = Tp // T
    cb = Z_B // W
    kv = lambda off: pl.BlockSpec((T, W), lambda i, j: (jnp.minimum(i, j), cb + off))
    return pl.pallas_call(
        functools.partial(_fox_kernel, scale=HD ** -0.5), grid=(nq, nq),
        in_specs=[pl.BlockSpec((T, W), lambda i, j: (i, cb)), kv(1), kv(2),
                  pl.BlockSpec((T, LANES), lambda i, j: (i, 0)),
                  pl.BlockSpec((8, T), lambda i, j: (0, jnp.minimum(i, j)))],
        out_specs=pl.BlockSpec((T, W), lambda i, j: (i, 0)),
        out_shape=jax.ShapeDtypeStruct((Tp, W), F32),
        scratch_shapes=_attn_scratch(T),
        compiler_params=_cp(("arbitrary", "arbitrary")), name="fox_prompt")(z, z, z, cum, cum_t)


def _top3_select(sc, valid, lane_f):
    sc = jnp.where(valid, sc, -jnp.inf)
    sel = jnp.zeros(sc.shape, F32)
    for _ in range(MOBA_TOPK):
        mx = jnp.max(sc, axis=1, keepdims=True)
        idx = jnp.min(jnp.where(sc == mx, lane_f, 1e9), axis=1, keepdims=True)
        hit = lane_f == idx
        sel = jnp.where(hit & valid, 1.0, sel)
        sc = jnp.where(hit, -jnp.inf, sc)
    return sel


def _blockmean_kernel(k_ref, o_ref):
    o_ref[...] = jnp.sum(k_ref[...], axis=0, keepdims=True) * (1.0 / MOBA_BLOCK)


def _blockmean_call(z, Tp):
    nb = Tp // MOBA_BLOCK
    W = BRANCH_W
    cb = Z_C // W + 1
    return pl.pallas_call(
        _blockmean_kernel, grid=(nb,),
        in_specs=[pl.BlockSpec((MOBA_BLOCK, W), lambda i: (i, cb))],
        out_specs=pl.BlockSpec((None, 1, W), lambda i: (i, 0, 0)),
        out_shape=jax.ShapeDtypeStruct((nb, 1, W), F32),
        compiler_params=_cp(("arbitrary",)), name="moba_kmean")(z)


def _moba_kernel(q_ref, k_ref, v_ref, km_ref, o_ref, m_ref, l_ref, acc_ref, sel_ref, *, scale):
    i = pl.program_id(0)
    j = pl.program_id(1)
    T = q_ref.shape[0]
    lane = _iota((T, LANES), 1)

    @pl.when(j == 0)
    def _():
        _attn_init(m_ref, l_ref, acc_ref)
        q = q_ref[...]
        km = km_ref[...]
        lane_f = lane.astype(F32)
        for h in range(NH):
            sl = slice(h * HD, (h + 1) * HD)
            sc = _dot3(q[:, sl], km[:, sl], NT)
            sel_ref[h] = _top3_select(sc, lane < i, lane_f)

    @pl.when(j <= i)
    def _():
        q = q_ref[...]
        k = k_ref[...]
        v = v_ref[...]
        causal = (_iota((T, T), 1) - _iota((T, T), 0)) <= jnp.where(j < i, 1 << 30, 0)
        for h in range(NH):
            sl = slice(h * HD, (h + 1) * HD)
            picked = jnp.sum(jnp.where(lane == j, sel_ref[h], 0.0), axis=1, keepdims=True)
            picked = jnp.where(j < i, picked, 1.0)
            s = _dot(q[:, sl].astype(BF16), k[:, sl].astype(BF16), NT) * scale
            s = jnp.where(causal & (picked > 0.5), s, NEG)
            _softmax_step(s, h, sl, v[:, sl].astype(BF16), m_ref, l_ref, acc_ref)

    @pl.when(j == i)
    def _():
        for h in range(NH):
            sl = slice(h * HD, (h + 1) * HD)
            o_ref[:, sl] = acc_ref[:, sl] / l_ref[h]


def _moba_call(z, kmean, Tp):
    T = MOBA_BLOCK
    W = BRANCH_W
    nq = Tp // T
    cb = Z_C // W
    kv = lambda off: pl.BlockSpec((T, W), lambda i, j: (jnp.minimum(i, j), cb + off))
    return pl.pallas_call(
        functools.partial(_moba_kernel, scale=HD ** -0.5), grid=(nq, nq),
        in_specs=[pl.BlockSpec((T, W), lambda i, j: (i, cb)), kv(1), kv(2),
                  pl.BlockSpec((LANES, W), lambda i, j: (0, 0))],
        out_specs=pl.BlockSpec((T, W), lambda i, j: (i, 0)),
        out_shape=jax.ShapeDtypeStruct((Tp, W), F32),
        scratch_shapes=_attn_scratch(T) + [pltpu.VMEM((NH, T, LANES), F32)],
        compiler_params=_cp(("arbitrary", "arbitrary")), name="moba_prompt")(z, z, z, kmean)


def _ret_kernel(q_ref, k_ref, v_ref, gd_ref, cos_ref, sin_ref, din_ref, qd_ref, kd_ref, cd_ref, s0_ref, lng_ref,
                y_ref, sfin_ref, S_ref, *, nc):
    c = pl.program_id(1)

    @pl.when(c == 0)
    def _():
        S_ref[...] = s0_ref[...]

    q = q_ref[...]
    k = k_ref[...]
    v = v_ref[...]
    gd = gd_ref[...]
    cos = cos_ref[...]
    sin = sin_ref[...]
    lng = lng_ref[...]
    for h in range(NH):
        sl = slice(h * HD, (h + 1) * HD)
        qh = q[:, sl]
        kh = k[:, sl]
        qr = qh * cos + pltpu.roll(qh, HD // 2, axis=1) * sin
        kr = (kh * cos + pltpu.roll(kh, HD // 2, axis=1) * sin) * (HD ** -0.5)
        vb = v[:, sl].astype(BF16)
        qb = qr.astype(BF16)
        att = _dot(qb, kr.astype(BF16), NT) * din_ref[h]
        S = S_ref[h]
        o = _dot(att.astype(BF16), vb) + _dot(qb, S.astype(BF16)) * qd_ref[h]
        S_ref[h] = S * cd_ref[h] + _dot((kr * kd_ref[h]).astype(BF16), vb, TN)
        y_ref[:, sl] = _ln(o, RET_GN_EPS) * lng[:, sl] * (gd[:, sl] * _sigmoid(gd[:, sl]))

    @pl.when(c == nc - 1)
    def _():
        sfin_ref[...] = S_ref[...]


def _ret_tables(C):
    lg = np.log(1.0 - 2.0 ** (-5.0 - np.arange(NH, dtype=np.float32))).astype(np.float32)
    i = np.arange(C, dtype=np.float32)
    diff = i[:, None] - i[None, :]
    din = np.where(diff[None] >= 0, np.exp(np.maximum(diff, 0.0)[None] * lg[:, None, None]), 0.0)
    qd = np.exp((i[None, :] + 1.0) * lg[:, None])
    kd = np.exp((C - 1.0 - i)[None, :] * lg[:, None])
    cd = np.exp(C * lg)
    bc = lambda t: np.ascontiguousarray(np.broadcast_to(t[:, :, None], (NH, C, HD))).astype(np.float32)
    cdb = np.ascontiguousarray(np.broadcast_to(cd[:, None, None], (NH, 1, HD))).astype(np.float32)
    return din.astype(np.float32), bc(qd), bc(kd), cdb


def _rope_tables(pos0, T):
    half = HD // 2
    inv = 1.0 / (ROPE_BASE ** (jnp.arange(half, dtype=F32) / half))
    ang = (pos0 + jnp.arange(T)).astype(F32)[:, None] * inv[None, :]
    cos = jnp.cos(ang)
    sin = jnp.sin(ang)
    return jnp.concatenate([cos, cos], axis=1), jnp.concatenate([-sin, sin], axis=1)


def _ret_call(z, s0, ln_g, l, row0, B, T, C, pos0):
    nc = T // C
    W = BRANCH_W
    cb = Z_D // W
    blk0 = row0 // C
    cos, sin = _rope_tables(pos0, T)
    din, qd, kd, cd = _ret_tables(C)
    zs = lambda off: pl.BlockSpec((C, W), lambda b, c: (blk0 + b * nc + c, cb + off))
    tab = pl.BlockSpec((C, HD), lambda b, c: (c, 0))
    full = lambda shape: pl.BlockSpec(shape, lambda b, c: (0,) * len(shape))
    st = pl.BlockSpec((None, NH, HD, HD), lambda b, c: (b, 0, 0, 0))
    return pl.pallas_call(
        functools.partial(_ret_kernel, nc=nc), grid=(B, nc),
        in_specs=[zs(0), zs(1), zs(2), zs(3), tab, tab, full((NH, C, C)), full((NH, C, HD)), full((NH, C, HD)),
                  full((NH, 1, HD)), st, pl.BlockSpec((None, 1, W), lambda b, c: (l, 0, 0))],
        out_specs=[pl.BlockSpec((C, W), lambda b, c: (b * nc + c, 0)), st],
        out_shape=[jax.ShapeDtypeStruct((B * T, W), F32), jax.ShapeDtypeStruct((B, NH, HD, HD), F32)],
        scratch_shapes=[pltpu.VMEM((NH, HD, HD), F32)],
        compiler_params=_cp(("arbitrary", "arbitrary")), name="ret_c%d" % C)(
            z, z, z, z, cos, sin, din, qd, kd, cd, s0, ln_g)


def _lfsuf_kernel(pt_ref, lf_hbm, ts_ref, tot_ref, o_ref, buf, sem, *, npg, base):
    b = pl.program_id(0)

    def page_copy(p):
        return pltpu.make_async_copy(lf_hbm.at[pl.ds(base + pt_ref[b * npg + p], 1), :],
                                     buf.at[pl.ds(p, 1), :], sem.at[p])

    for p in range(npg):
        page_copy(p).start()
    for p in range(npg):
        page_copy(p).wait()
    x = buf[...]
    loc = _dot_exact_r(x, ts_ref[...])
    tot = _dot_exact_r(x, tot_ref[...])
    later = (_iota((npg, npg), 1) > _iota((npg, npg), 0)).astype(BF16)
    o_ref[...] = loc + _dot_exact_l(later, tot)


def _lfsuf_call(pt_flat, lf_flat, ts, tot, B, npg, base):
    Wp = lf_flat.shape[1]
    return pl.pallas_call(
        functools.partial(_lfsuf_kernel, npg=npg, base=base),
        grid_spec=pltpu.PrefetchScalarGridSpec(
            num_scalar_prefetch=1, grid=(B,),
            in_specs=[pl.BlockSpec(memory_space=pl.ANY),
                      pl.BlockSpec((Wp, Wp), lambda b, pt: (0, 0)),
                      pl.BlockSpec((Wp, Wp), lambda b, pt: (0, 0))],
            out_specs=pl.BlockSpec((None, npg, Wp), lambda b, pt: (b, 0, 0)),
            scratch_shapes=[pltpu.VMEM((npg, Wp), F32), pltpu.SemaphoreType.DMA((npg,))]),
        out_shape=jax.ShapeDtypeStruct((B, npg, Wp), F32),
        compiler_params=_cp(("arbitrary",)), name="fox_logf_suffix")(pt_flat, lf_flat, ts, tot)


def _suffix_matrices():
    r = np.arange(PAGE_SIZE)
    h = np.arange(NH)
    src_r = np.repeat(r, NH)
    src_h = np.tile(h, PAGE_SIZE)
    dst_h = np.repeat(h, PAGE_SIZE)
    dst_r = np.tile(r, NH)
    same = src_h[:, None] == dst_h[None, :]
    ts = same & (src_r[:, None] > dst_r[None, :])
    return jnp.asarray(ts, BF16), jnp.asarray(same, BF16)


def _sattn_kernel(pt_ref, fq_ref, fkn_ref, fvn_ref, lfn_ref, mq_ref, mkn_ref, mvn_ref, R_ref,
                  fk0, fk1, fv0, fv1, mk0, mk1, mv0, mv1,
                  yb_ref, yc_ref,
                  fqb, mqb, mqf, cnb, fm, fl, facc, sc_all, m_all, l_all, o_all, *, nb, S, scale):
    n = pl.program_id(1)
    R4 = NH * S
    W = NH * HD
    own_head = (_iota((R4, W), 1) // HD) == (_iota((R4, W), 0) // S)
    lane = _iota((R4, LANES), 1)

    def stack_heads(x):
        return jnp.where(own_head, jnp.concatenate([x] * NH, axis=0), 0.0)

    def rows_by_head(x, width):
        return jnp.concatenate(
            [jnp.broadcast_to(x[:, h * width:(h + 1) * width], (S, width)) for h in range(NH)], axis=0)

    @pl.when(n == 0)
    def _():
        fqb[...] = stack_heads(fq_ref[...]).astype(BF16)
        mq = stack_heads(mq_ref[...])
        mqb[...] = mq.astype(BF16)
        mqf[...] = mq
        tri = (_iota((S, S), 0) >= _iota((S, S), 1)).astype(BF16)
        cn = _dot_exact_l(tri, lfn_ref[...])
        cnb[...] = jnp.concatenate([jnp.broadcast_to(cn[:, h:h + 1], (S, LANES)) for h in range(NH)], axis=0)
        fm[...] = jnp.full(fm.shape, NEG, F32)
        fl[...] = jnp.zeros(fl.shape, F32)
        facc[...] = jnp.zeros(facc.shape, F32)
        sc_all[...] = jnp.full(sc_all.shape, -jnp.inf, F32)
        m_all[...] = jnp.full(m_all.shape, NEG, F32)
        l_all[...] = jnp.zeros(l_all.shape, F32)

    def fox_update(s, v_bf):
        m_prev = fm[...]
        m_new = jnp.maximum(m_prev, jnp.max(s, axis=1, keepdims=True))
        alpha = jnp.exp(m_prev - m_new)
        p = jnp.exp(s - m_new)
        fl[...] = alpha * fl[...] + jnp.sum(p, axis=1, keepdims=True)
        facc[...] = alpha * facc[...] + _dot(p.astype(BF16), v_bf)
        fm[...] = m_new

    cn_b = cnb[...]
    for half, (k_ref, v_ref) in enumerate(((fk0, fv0), (fk1, fv1))):
        s = _dot(fqb[...], k_ref[...].astype(BF16), NT) * scale
        suf = R_ref[pl.ds(2 * n + half, 1), :]
        s = s + cn_b + rows_by_head(suf, PAGE_SIZE)
        fox_update(s, v_ref[...].astype(BF16))

    kblk = jnp.concatenate([mk0[...], mk1[...]], axis=0)
    vblk = jnp.concatenate([mv0[...], mv1[...]], axis=0)
    kmean = jnp.sum(kblk, axis=0, keepdims=True) * (1.0 / MOBA_BLOCK)
    sc_col = jnp.sum(mqf[...] * kmean, axis=1, keepdims=True)
    s = _dot(mqb[...], kblk.astype(BF16), NT) * scale
    m_b = jnp.max(s, axis=1, keepdims=True)
    p = jnp.exp(s - m_b)
    here = lane == n
    sc_all[...] = jnp.where(here, sc_col, sc_all[...])
    m_all[...] = jnp.where(here, m_b, m_all[...])
    l_all[...] = jnp.where(here, jnp.sum(p, axis=1, keepdims=True), l_all[...])
    o_all[n] = _dot(p.astype(BF16), vblk.astype(BF16))

    @pl.when(n == nb - 1)
    def _():
        qi = _iota((R4, S), 0) % S
        ki = _iota((R4, S), 1)
        causal = ki <= qi
        eye = _iota((S, S), 0) == _iota((S, S), 1)
        cn_rows = []
        for h in range(NH):
            col = cn_b[h * S:(h + 1) * S, 0:1]
            cn_rows.append(jnp.broadcast_to(
                jnp.sum(jnp.where(eye, jnp.broadcast_to(col, (S, S)), 0.0), axis=0, keepdims=True), (S, S)))
        s = _dot(fqb[...], fkn_ref[...].astype(BF16), NT) * scale
        s = s + (cn_b[:, 0:S] - jnp.concatenate(cn_rows, axis=0))
        fox_update(jnp.where(causal, s, NEG), fvn_ref[...].astype(BF16))
        out = facc[...] / fl[...]
        for h in range(NH):
            yb_ref[:, h * HD:(h + 1) * HD] = out[h * S:(h + 1) * S, h * HD:(h + 1) * HD]

        s = _dot(mqb[...], mkn_ref[...].astype(BF16), NT) * scale
        s = jnp.where(causal, s, NEG)
        m_o = jnp.max(s, axis=1, keepdims=True)
        p = jnp.exp(s - m_o)
        l_o = jnp.sum(p, axis=1, keepdims=True)
        o_o = _dot(p.astype(BF16), mvn_ref[...].astype(BF16))
        sel = _top3_select(sc_all[...], lane < nb, lane.astype(F32)) > 0.5
        m_past = m_all[...]
        m_tot = jnp.maximum(jnp.max(jnp.where(sel, m_past, NEG), axis=1, keepdims=True), m_o)
        w = jnp.where(sel, jnp.exp(m_past - m_tot), 0.0)
        w_o = jnp.exp(m_o - m_tot)
        den = jnp.sum(w * l_all[...], axis=1, keepdims=True) + w_o * l_o
        num = w_o * o_o
        for blk in range(nb):
            num = num + w[:, blk:blk + 1] * o_all[blk]
        out = num / den
        for h in range(NH):
            yc_ref[:, h * HD:(h + 1) * HD] = out[h * S:(h + 1) * S, h * HD:(h + 1) * HD]


def _sattn_call(pt_flat, z, lf, R, caches, l, Mp, B, S, npg, n_pool):
    fk, fv, mk, mv = caches
    W = BRANCH_W
    nb = npg // 2
    R4 = NH * S
    rb = Mp // S
    base = l * n_pool
    zrow = lambda cb: pl.BlockSpec((S, W), lambda b, n, pt: (rb + b, cb))
    page = lambda half: pl.BlockSpec((None, PAGE_SIZE, W),
                                     lambda b, n, pt: (base + pt[b * npg + 2 * n + half], 0, 0))
    cbB = Z_B // W
    cbC = Z_C // W
    in_specs = [zrow(cbB), zrow(cbB + 1), zrow(cbB + 2),
                pl.BlockSpec((S, LANES), lambda b, n, pt: (rb + b, 0)),
                zrow(cbC), zrow(cbC + 1), zrow(cbC + 2),
                pl.BlockSpec((None, npg, W), lambda b, n, pt: (b, 0, 0)),
                page(0), page(1), page(0), page(1), page(0), page(1), page(0), page(1)]
    out_spec = pl.BlockSpec((S, W), lambda b, n, pt: (b, 0))
    scratch = [pltpu.VMEM((R4, W), BF16), pltpu.VMEM((R4, W), BF16), pltpu.VMEM((R4, W), F32),
               pltpu.VMEM((R4, LANES), F32),
               pltpu.VMEM((R4, 1), F32), pltpu.VMEM((R4, 1), F32), pltpu.VMEM((R4, W), F32),
               pltpu.VMEM((R4, LANES), F32), pltpu.VMEM((R4, LANES), F32), pltpu.VMEM((R4, LANES), F32),
               pltpu.VMEM((nb, R4, W), F32)]
    return pl.pallas_call(
        functools.partial(_sattn_kernel, nb=nb, S=S, scale=HD ** -0.5),
        grid_spec=pltpu.PrefetchScalarGridSpec(
            num_scalar_prefetch=1, grid=(B, nb), in_specs=in_specs,
            out_specs=[out_spec, out_spec], scratch_shapes=scratch),
        out_shape=[jax.ShapeDtypeStruct((B * S, W), F32)] * 2,
        compiler_params=_cp(("arbitrary", "arbitrary")), name="sample_attn")(
            pt_flat, z, z, z, lf, z, z, z, R, fk, fk, fv, fv, mk, mk, mv, mv)


def _pick_tile(m, cands):
    for t in cands:
        if m % t == 0:
            return t
    raise ValueError("no row tile divides %d" % m)


def _pad_cols(x, n):
    return jnp.pad(x, ((0, 0), (0, n - x.shape[1])))


def kernel(x_prompt, x_sample, c_prompt, c_sample, page_table, cache_fox_k, cache_fox_v, cache_fox_logf, cache_moba_k, cache_moba_v, state_rwkv, state_rwkv_shift, state_ret, w_ada, b_ada, w_in, rwkv_mu, rwkv_w0, rwkv_w2, rwkv_a0, rwkv_a2, rwkv_g2, rwkv_k_k, rwkv_k_a, rwkv_r_k, rwkv_ln_g, rwkv_ln_b, fox_bf, ret_ln_g, w_branch, w_o, ln1_g, ln1_b, ln2_g, ln2_b, w_ffn_gate, w_ffn_in, w_ffn_out):
    Bp, Tp, D = x_prompt.shape
    Bs, S, _ = x_sample.shape
    depth = w_in.shape[0]
    n_pool = cache_fox_k.shape[1]
    npg = page_table.shape[1]
    past = npg * PAGE_SIZE
    Mp, Ms = Bp * Tp, Bs * S
    M = Mp + Ms
    assert Bp == 1 and D == D_MODEL and depth == DEPTH
    assert Mp % TM == 0 and Ms % TM == 0 and Tp % MOBA_BLOCK == 0 and Tp // MOBA_BLOCK <= LANES
    assert Tp % RWKV_CHUNK == 0 and Tp % RET_CHUNK == 0 and S == 8 and past % MOBA_BLOCK == 0
    n_p = Mp // TM
    W = BRANCH_W
    tm_big = _pick_tile(M, (768, 512, 256))
    tm_small = _pick_tile(M, (384, 256))

    x = jnp.concatenate([x_prompt.reshape(Mp, D), x_sample.reshape(Ms, D)], axis=0)
    c_rows = 8 * (-(-(Bp + Bs) // 8))
    c_all = jnp.pad(jnp.concatenate([c_prompt, c_sample], axis=0), ((0, c_rows - Bp - Bs), (0, 0)))
    pt_flat = page_table.reshape(-1).astype(jnp.int32)
    cache3 = lambda t: t.reshape(depth * n_pool, PAGE_SIZE, NH * HD)
    caches = (cache3(cache_fox_k), cache3(cache_fox_v), cache3(cache_moba_k), cache3(cache_moba_v))
    lf_flat = cache_fox_logf.reshape(depth * n_pool, PAGE_SIZE * NH)
    ts_m, tot_m = _suffix_matrices()
    e64 = jnp.asarray(np.kron(np.eye(A_HEADS), np.ones((A_HD, A_HD))), BF16)
    b_ada3 = b_ada.reshape(depth, 1, -1)
    vec3 = lambda t: t.reshape(depth, 1, -1)
    ln1_g3, ln1_b3, ln2_g3, ln2_b3, ret_g3 = map(vec3, (ln1_g, ln1_b, ln2_g, ln2_b, ret_ln_g))

    per_layer = []
    for l in range(depth):
        ada = _ada_call(c_all, w_ada, b_ada3, l)
        ada_p = ada[0:8]
        mod_s = jnp.repeat(ada[Bp:Bp + Bs], S, axis=0)
        h = _lnmod_call(x, ada_p, mod_s, n_p, 1, 0)
        wl = w_in[l]
        o_b, o_c, o_d = A_COLS, A_COLS + B_COLS, A_COLS + B_COLS + C_COLS
        o_g = o_d + D_COLS
        w_in_p = jnp.concatenate(
            [wl[:, o_g:], _pad_cols(wl[:, :o_b], 2048), wl[:, o_d:o_g], wl[:, o_c:o_d],
             _pad_cols(wl[:, o_b:o_c], 1664)], axis=1)
        z = _mm_call(h, w_in_p, 0, tm_big, 1408, F32, "w_in")

        fill_s = jnp.repeat(_pad_cols(state_rwkv_shift[l], 2048), S, axis=0)
        pad_rows = lambda w2, r0: jnp.zeros((W, W), F32).at[r0:r0 + w2.shape[0]].set(w2)
        pre = _rwkv_pre_call(
            z, fill_s, _pad_cols(rwkv_mu[l][None], 2048), rwkv_w0[l][None], rwkv_a0[l][None],
            rwkv_k_k[l][None], rwkv_k_a[l][None], pad_rows(rwkv_w2[l], 0), pad_rows(rwkv_a2[l], A_LORA_W),
            pad_rows(rwkv_g2[l], A_LORA_W + A_LORA_A), e64, n_p, S)
        hv = lambda t: t.reshape(A_HEADS, 1, A_HD)
        rk, lg, lb = hv(rwkv_r_k[l]), hv(rwkv_ln_g[l]), hv(rwkv_ln_b[l])
        ya_p, sT_p = _rwkv_chunk_call(pre, jnp.zeros((Bp, A_HEADS, A_HD, A_HD), F32), rk, lg, lb,
                                      0, Bp, Tp, RWKV_CHUNK)
        ya_s, sT_s = _rwkv_chunk_call(pre, jnp.swapaxes(state_rwkv[l], -1, -2), rk, lg, lb, Mp, Bs, S, S)
        ya = jnp.concatenate([ya_p, ya_s], axis=0)

        lf, cum = _foxcum_call(z, _pad_cols(fox_bf[l][None], LANES))
        yb_p = _fox_call(z, cum, jnp.transpose(cum[:Mp, :8]), Tp)
        kmean = _blockmean_call(z, Tp).reshape(Tp // MOBA_BLOCK, W)
        kmean = jnp.pad(kmean, ((0, LANES - kmean.shape[0]), (0, 0)))
        yc_p = _moba_call(z, kmean, Tp)
        R = _lfsuf_call(pt_flat, lf_flat, ts_m, tot_m, Bs, npg, l * n_pool)
        yb_s, yc_s = _sattn_call(pt_flat, z, lf, R, caches, l, Mp, Bs, S, npg, n_pool)
        yb = jnp.concatenate([yb_p, yb_s], axis=0)
        yc = jnp.concatenate([yc_p, yc_s], axis=0)

        yd_p, rS_p = _ret_call(z, jnp.zeros((Bp, NH, HD, HD), F32), ret_g3, l, 0, Bp, Tp, RET_CHUNK, 0)
        yd_s, rS_s = _ret_call(z, state_ret[l], ret_g3, l, Mp, Bs, S, S, past)
        yd = jnp.concatenate([yd_p, yd_s], axis=0)

        mixin = _merge_call((ya, yb, yc, yd), w_branch, z, l, tm_big, 512)
        mix = _mm_call(mixin, w_o, l, tm_big, 512, F32, "w_o")
        x1, h2 = _resid_call(x, mix, ada_p, mod_s, ln1_g3, ln1_b3, l, n_p, 2, 4, 3)
        act = _ffn_call(h2, w_ffn_gate, w_ffn_in, l, tm_big, 512)
        f = _mm_call(act, w_ffn_out, l, tm_small, 512, F32, "ffn_out")
        x = _resid_call(x1, f, ada_p, mod_s, ln2_g3, ln2_b3, l, n_p, 5)

        zb = Z_B
        zc = Z_C
        hp = lambda t: t.reshape(Bp, Tp, NH, HD)
        hs = lambda t: t.reshape(Bs, S, NH, HD)
        per_layer.append(dict(
            fox_k_p=hp(z[:Mp, zb + W:zb + 2 * W]), fox_v_p=hp(z[:Mp, zb + 2 * W:zb + 3 * W]),
            fox_lf_p=lf[:Mp, :NH].reshape(Bp, Tp, NH),
            moba_k_p=hp(z[:Mp, zc + W:zc + 2 * W]), moba_v_p=hp(z[:Mp, zc + 2 * W:zc + 3 * W]),
            rwkv_S_p=jnp.swapaxes(sT_p, -1, -2), shift_p=z[Mp - 1:Mp, Z_A:Z_A + A_COLS], ret_S_p=rS_p,
            fox_k_s=hs(z[Mp:, zb + W:zb + 2 * W]), fox_v_s=hs(z[Mp:, zb + 2 * W:zb + 3 * W]),
            fox_lf_s=lf[Mp:, :NH].reshape(Bs, S, NH),
            moba_k_s=hs(z[Mp:, zc + W:zc + 2 * W]), moba_v_s=hs(z[Mp:, zc + 2 * W:zc + 3 * W]),
            rwkv_S_s=jnp.swapaxes(sT_s, -1, -2),
            shift_s=z[Mp:, Z_A:Z_A + A_COLS].reshape(Bs, S, A_COLS)[:, S - 1], ret_S_s=rS_s))

    st = lambda name: jnp.stack([p[name] for p in per_layer])
    return (x[:Mp].reshape(Bp, Tp, D), x[Mp:].reshape(Bs, S, D),
            st("fox_k_p"), st("fox_v_p"), st("fox_lf_p"), st("moba_k_p"), st("moba_v_p"),
            st("rwkv_S_p"), st("shift_p"), st("ret_S_p"),
            st("fox_k_s"), st("fox_v_s"), st("fox_lf_s"), st("moba_k_s"), st("moba_v_s"),
            st("rwkv_S_s"), st("shift_s"), st("ret_S_s"))
```

```python
import functools

import numpy as np
import jax
import jax.numpy as jnp
from jax import lax
from jax.experimental import pallas as pl
from jax.experimental.pallas import tpu as pltpu

F32 = jnp.float32
BF16 = jnp.bfloat16

D_MODEL = 2048
DEPTH = 2
PAGE_SIZE = 128
BRANCH_W = 512
A_HD = 64
A_HEADS = 8
A_LORA_W = 96
A_LORA_A = 96
A_LORA_G = 256
A_COLS = 3 * BRANCH_W + A_LORA_W + A_LORA_A + A_LORA_G
B_COLS = 3 * BRANCH_W + 4
C_COLS = 3 * BRANCH_W
D_COLS = 4 * BRANCH_W
HD = 128
NH = 4
RWKV_GN_EPS = 64e-5
RET_GN_EPS = 1e-6
LN_EPS = 1e-5
MOBA_BLOCK = 256
MOBA_TOPK = 3
RET_CHUNK = 128
RWKV_CHUNK = 64
ROPE_BASE = 10000.0
FFN_HIDDEN = 5632
ALPHA = (2 * DEPTH) ** 0.25

Z_G = 0
Z_A = 4 * D_MODEL
Z_D = Z_A + 2048
Z_C = Z_D + 2048
Z_B = Z_C + 1536
Z_COLS = Z_B + 1664

TM = 256
LANES = 128
NEG = -1e30
VMEM_LIMIT = 56 << 20

NN = (((1,), (0,)), ((), ()))
NT = (((1,), (1,)), ((), ()))
TN = (((0,), (0,)), ((), ()))


def _cp(sem):
    return pltpu.CompilerParams(dimension_semantics=sem, vmem_limit_bytes=VMEM_LIMIT)


def _dot(a, b, dims=NN):
    return lax.dot_general(a, b, dims, preferred_element_type=F32)


def _split2(x):
    hi = x.astype(BF16)
    return hi, (x - hi.astype(F32)).astype(BF16)


def _split3(x):
    hi = x.astype(BF16)
    r1 = x - hi.astype(F32)
    mid = r1.astype(BF16)
    return hi, mid, (r1 - mid.astype(F32)).astype(BF16)


def _dot3(a, b, dims=NN):
    ah, al = _split2(a)
    bh, bl = _split2(b)
    return _dot(ah, bh, dims) + (_dot(ah, bl, dims) + _dot(al, bh, dims))


def _dot_exact_l(m_bf16, x, n=3):
    parts = _split3(x) if n == 3 else _split2(x)
    acc = _dot(m_bf16, parts[0])
    for p in parts[1:]:
        acc = acc + _dot(m_bf16, p)
    return acc


def _dot_exact_r(x, m_bf16, n=3):
    parts = _split3(x) if n == 3 else _split2(x)
    acc = _dot(parts[0], m_bf16)
    for p in parts[1:]:
        acc = acc + _dot(p, m_bf16)
    return acc


def _ln(x, eps):
    mu = jnp.mean(x, axis=-1, keepdims=True)
    xc = x - mu
    var = jnp.mean(xc * xc, axis=-1, keepdims=True)
    return xc * lax.rsqrt(var + eps)


def _sigmoid(x):
    return 1.0 / (1.0 + jnp.exp(-x))


def _log_sigmoid(x):
    return jnp.minimum(x, 0.0) - jnp.log(1.0 + jnp.exp(-jnp.abs(x)))


def _iota(shape, dim):
    return lax.broadcasted_iota(jnp.int32, shape, dim)


def _ada_kernel(c_ref, w_ref, b_ref, o_ref):
    c = c_ref[...]
    s = (c * _sigmoid(c)).astype(BF16)
    o_ref[...] = _dot(s, w_ref[...].astype(BF16)) + b_ref[...]


def _ada_call(c_all, w_ada, b_ada, l):
    R, D = c_all.shape
    N = w_ada.shape[2]
    tn = 2048
    return pl.pallas_call(
        _ada_kernel, grid=(N // tn,),
        in_specs=[pl.BlockSpec((R, D), lambda j: (0, 0)),
                  pl.BlockSpec((None, D, tn), lambda j: (l, 0, j)),
                  pl.BlockSpec((None, 1, tn), lambda j: (l, 0, j))],
        out_specs=pl.BlockSpec((R, tn), lambda j: (0, j)),
        out_shape=jax.ShapeDtypeStruct((R, N), F32),
        compiler_params=_cp(("arbitrary",)), name="ada")(c_all, w_ada, b_ada)


def _lnmod_kernel(x_ref, scp_ref, shp_ref, scs_ref, shs_ref, o_ref, *, n_p):
    is_s = pl.program_id(0) >= n_p
    sc = jnp.where(is_s, scs_ref[...], scp_ref[0:1, :])
    sh = jnp.where(is_s, shs_ref[...], shp_ref[0:1, :])
    o_ref[...] = (_ln(x_ref[...], LN_EPS) * (1.0 + sc) + sh).astype(BF16)


def _mod_specs(n_p, cols):
    specs = []
    for cb in cols:
        specs.append(pl.BlockSpec((8, D_MODEL), lambda i, cb=cb: (0, cb)))
    for cb in cols:
        specs.append(pl.BlockSpec((TM, D_MODEL), lambda i, cb=cb: (jnp.maximum(i - n_p, 0), cb)))
    return specs


def _lnmod_call(x, ada_p, mod_s, n_p, sc_col, sh_col):
    M, D = x.shape
    return pl.pallas_call(
        functools.partial(_lnmod_kernel, n_p=n_p), grid=(M // TM,),
        in_specs=[pl.BlockSpec((TM, D), lambda i: (i, 0))] + _mod_specs(n_p, (sc_col, sh_col)),
        out_specs=pl.BlockSpec((TM, D), lambda i: (i, 0)),
        out_shape=jax.ShapeDtypeStruct((M, D), BF16),
        compiler_params=_cp(("arbitrary",)), name="lnmod")(x, ada_p, ada_p, mod_s, mod_s)


def _mm_kernel(a_ref, w_ref, o_ref, wb_ref):
    @pl.when(pl.program_id(1) == 0)
    def _():
        wb_ref[...] = w_ref[...].astype(BF16)

    o_ref[...] = _dot(a_ref[...].astype(BF16), wb_ref[...]).astype(o_ref.dtype)


def _mm_call(a, w, w_index, tm, tn, out_dtype, name):
    M, K = a.shape
    N = w.shape[-1]
    if w.ndim == 3:
        w_spec = pl.BlockSpec((None, K, tn), lambda j, i: (w_index, 0, j))
    else:
        w_spec = pl.BlockSpec((K, tn), lambda j, i: (0, j))
    return pl.pallas_call(
        _mm_kernel, grid=(N // tn, M // tm),
        in_specs=[pl.BlockSpec((tm, K), lambda j, i: (i, 0)), w_spec],
        out_specs=pl.BlockSpec((tm, tn), lambda j, i: (i, j)),
        out_shape=jax.ShapeDtypeStruct((M, N), out_dtype),
        scratch_shapes=[pltpu.VMEM((K, tn), BF16)],
        compiler_params=_cp(("arbitrary", "arbitrary")), name=name)(a, w)


def _resid_kernel(*refs, n_p, with_h):
    if with_h:
        (x_ref, mix_ref, gp_ref, scp_ref, shp_ref, gs_ref, scs_ref, shs_ref, lg_ref, lb_ref,
         x1_ref, h_ref) = refs
    else:
        x_ref, mix_ref, gp_ref, gs_ref, lg_ref, lb_ref, x1_ref = refs
    is_s = pl.program_id(0) >= n_p
    g = jnp.where(is_s, gs_ref[...], gp_ref[0:1, :])
    y = ALPHA * x_ref[...] + (1.0 + g) * mix_ref[...]
    x1 = _ln(y, LN_EPS) * lg_ref[...] + lb_ref[...]
    x1_ref[...] = x1
    if with_h:
        sc = jnp.where(is_s, scs_ref[...], scp_ref[0:1, :])
        sh = jnp.where(is_s, shs_ref[...], shp_ref[0:1, :])
        h_ref[...] = (_ln(x1, LN_EPS) * (1.0 + sc) + sh).astype(BF16)


def _resid_call(x, mix, ada_p, mod_s, ln_g, ln_b, l, n_p, g_col, sc_col=None, sh_col=None):
    M, D = x.shape
    with_h = sc_col is not None
    cols = (g_col, sc_col, sh_col) if with_h else (g_col,)
    row = pl.BlockSpec((TM, D), lambda i: (i, 0))
    vec = pl.BlockSpec((None, 1, D), lambda i: (l, 0, 0))
    in_specs = [row, row] + _mod_specs(n_p, cols) + [vec, vec]
    args = [x, mix] + [ada_p] * len(cols) + [mod_s] * len(cols) + [ln_g, ln_b]
    out_shape = [jax.ShapeDtypeStruct((M, D), F32)]
    out_specs = [row]
    if with_h:
        out_shape.append(jax.ShapeDtypeStruct((M, D), BF16))
        out_specs.append(row)
    res = pl.pallas_call(
        functools.partial(_resid_kernel, n_p=n_p, with_h=with_h), grid=(M // TM,),
        in_specs=in_specs, out_specs=out_specs, out_shape=out_shape,
        compiler_params=_cp(("arbitrary",)), name="resid_h" if with_h else "resid")(*args)
    return res if with_h else res[0]


def _ffn_kernel(h_ref, wg_ref, wi_ref, o_ref, wgb_ref, wib_ref):
    @pl.when(pl.program_id(1) == 0)
    def _():
        wgb_ref[...] = wg_ref[...].astype(BF16)
        wib_ref[...] = wi_ref[...].astype(BF16)

    h = h_ref[...]
    a = _dot(h, wgb_ref[...])
    b = _dot(h, wib_ref[...])
    o_ref[...] = (a * _sigmoid(a) * b).astype(BF16)


def _ffn_call(h, wg, wi, l, tm, tn):
    M, K = h.shape
    N = wg.shape[-1]
    w_spec = pl.BlockSpec((None, K, tn), lambda j, i: (l, 0, j))
    return pl.pallas_call(
        _ffn_kernel, grid=(N // tn, M // tm),
        in_specs=[pl.BlockSpec((tm, K), lambda j, i: (i, 0)), w_spec, w_spec],
        out_specs=pl.BlockSpec((tm, tn), lambda j, i: (i, j)),
        out_shape=jax.ShapeDtypeStruct((M, N), BF16),
        scratch_shapes=[pltpu.VMEM((K, tn), BF16), pltpu.VMEM((K, tn), BF16)],
        compiler_params=_cp(("arbitrary", "arbitrary")), name="ffn_act")(h, wg, wi)


def _merge_kernel(ya_ref, yb_ref, yc_ref, yd_ref, wb_ref, g0_ref, g1_ref, g2_ref, g3_ref, o_ref, wbb_ref):
    @pl.when(pl.program_id(1) == 0)
    def _():
        wbb_ref[...] = wb_ref[...].astype(BF16)

    acc = None
    for n, (y_ref, g_ref) in enumerate(((ya_ref, g0_ref), (yb_ref, g1_ref), (yc_ref, g2_ref), (yd_ref, g3_ref))):
        up = _dot(y_ref[...].astype(BF16), wbb_ref[n])
        t = _sigmoid(g_ref[...]) * up
        acc = t if acc is None else acc + t
    o_ref[...] = acc.astype(BF16)


def _merge_call(ys, w_branch, z, l, tm, tn):
    M = z.shape[0]
    W = BRANCH_W
    D = D_MODEL
    nj = D // tn
    y_spec = pl.BlockSpec((tm, W), lambda j, i: (i, 0))
    g_specs = [pl.BlockSpec((tm, tn), lambda j, i, n=n: (i, (Z_G + n * D) // tn + j)) for n in range(4)]
    return pl.pallas_call(
        _merge_kernel, grid=(nj, M // tm),
        in_specs=[y_spec] * 4 + [pl.BlockSpec((None, 4, W, tn), lambda j, i: (l, 0, 0, j))] + g_specs,
        out_specs=pl.BlockSpec((tm, tn), lambda j, i: (i, j)),
        out_shape=jax.ShapeDtypeStruct((M, D), BF16),
        scratch_shapes=[pltpu.VMEM((4, W, tn), BF16)],
        compiler_params=_cp(("arbitrary", "arbitrary")), name="merge")(*ys, w_branch, z, z, z, z)


def _rwkv_pre_kernel(za_ref, prev_ref, fill_ref, mu_ref, w0_ref, a0_ref, kk_ref, ka_ref, w2_ref, a2_ref,
                     g2_ref, e_ref, r_o, lw_o, k_o, v_o, kn_o, b_o, g_o, *, n_p, seq_s):
    i = pl.program_id(0)
    is_s = i >= n_p
    za = za_ref[...]
    rows = _iota(za.shape, 0)
    prev = jnp.where(rows == 0, prev_ref[7:8, :], pltpu.roll(za, 1, axis=0))
    base = jnp.where(is_s, n_p * TM, 0)
    pmask = jnp.where(is_s, seq_s - 1, 0x3FFFFFFF)
    start = ((rows + (i * TM - base)) & pmask) == 0
    fill = jnp.where(is_s, fill_ref[...], 0.0)
    prev = jnp.where(start, fill, prev)
    zm = za + mu_ref[...] * (prev - za)
    W = BRANCH_W
    r = zm[:, 0:W]
    k = zm[:, W:2 * W]
    v = zm[:, 2 * W:3 * W]
    x = zm[:, 3 * W:4 * W]
    wl = _dot(jnp.tanh(x).astype(BF16), w2_ref[...].astype(BF16))
    y = -(w0_ref[...] + wl)
    softplus = jnp.maximum(y, 0.0) + jnp.log(1.0 + jnp.exp(-jnp.abs(y)))
    w = -softplus - 0.5
    logw = -jnp.exp(w)
    a = _sigmoid(a0_ref[...] + _dot(x.astype(BF16), a2_ref[...].astype(BF16)))
    g = _dot(_sigmoid(x).astype(BF16), g2_ref[...].astype(BF16))
    kk0 = k * kk_ref[...]
    ss = _dot_exact_r(kk0 * kk0, e_ref[...], n=3)
    kn = kk0 / jnp.maximum(jnp.sqrt(ss), 1e-12)
    k2 = k * (1.0 + (a - 1.0) * ka_ref[...])
    b = kn * a
    for h in range(A_HEADS):
        sl = slice(h * A_HD, (h + 1) * A_HD)
        r_o[h] = r[:, sl]
        lw_o[h] = logw[:, sl]
        k_o[h] = k2[:, sl]
        v_o[h] = v[:, sl]
        kn_o[h] = kn[:, sl]
        b_o[h] = b[:, sl]
        g_o[h] = g[:, sl]


def _rwkv_pre_call(z, fill_s, mu, w0, a0, k_k, k_a, w2p, a2p, g2p, e64, n_p, seq_s):
    M = z.shape[0]
    W = BRANCH_W
    cb = Z_A // 2048
    vec = lambda n: pl.BlockSpec((1, n), lambda i: (0, 0))
    mat = pl.BlockSpec((W, W), lambda i: (0, 0))
    out = jax.ShapeDtypeStruct((A_HEADS, M, A_HD), F32)
    ospec = pl.BlockSpec((A_HEADS, TM, A_HD), lambda i: (0, i, 0))
    return pl.pallas_call(
        functools.partial(_rwkv_pre_kernel, n_p=n_p, seq_s=seq_s), grid=(M // TM,),
        in_specs=[pl.BlockSpec((TM, 2048), lambda i: (i, cb)),
                  pl.BlockSpec((8, 2048), lambda i: (jnp.maximum(i * (TM // 8) - 1, 0), cb)),
                  pl.BlockSpec((TM, 2048), lambda i: (jnp.maximum(i - n_p, 0), 0)),
                  vec(2048), vec(W), vec(W), vec(W), vec(W), mat, mat, mat, mat],
        out_specs=[ospec] * 7, out_shape=[out] * 7,
        compiler_params=_cp(("arbitrary",)), name="rwkv_pre")(
            z, z, fill_s, mu, w0, a0, k_k, k_a, w2p, a2p, g2p, e64)


def _rwkv_chunk_kernel(r_ref, lw_ref, k_ref, v_ref, kn_ref, b_ref, g_ref, s0_ref, rk_ref, lng_ref, lnb_ref,
                       y_ref, sfin_ref, S_ref, *, C, nc):
    c = pl.program_id(1)

    @pl.when(c == 0)
    def _():
        S_ref[...] = s0_ref[...]

    H = range(A_HEADS)
    row = _iota((C, C), 0)
    col = _iota((C, C), 1)
    tri = (row >= col).astype(BF16)
    eye = (row == col).astype(F32)
    row2 = _iota((2 * C, C), 0)
    col2 = _iota((2 * C, C), 1)
    low2 = jnp.where(row2 < C, row2 - 1, row2 - C) >= col2
    eye_k = _iota((A_HD, A_HD), 0) == _iota((A_HD, A_HD), 1)
    r = [r_ref[h] for h in H]
    lw = [lw_ref[h] for h in H]
    k = [k_ref[h] for h in H]
    v = [v_ref[h] for h in H]
    kn = [kn_ref[h] for h in H]
    b = [b_ref[h] for h in H]
    S0 = [S_ref[h] for h in H]
    L = [_dot_exact_l(tri, lw[h]) for h in H]
    eL = [jnp.exp(L[h]) for h in H]
    eN = [jnp.exp(-L[h]) for h in H]
    eE = [jnp.exp(L[h][C - 1:C, :] - L[h]) for h in H]
    lhs = [jnp.concatenate([kn[h] * jnp.exp(L[h] - lw[h]), r[h] * eL[h]], axis=0) for h in H]
    Ab = [jnp.where(low2, _dot3(lhs[h], b[h] * eN[h], NT), 0.0) for h in H]
    Ak = [jnp.where(low2, _dot3(lhs[h], k[h] * eN[h], NT), 0.0) for h in H]
    N = [Ab[h][0:C] for h in H]
    X = [eye - N[h] for h in H]
    P = [_dot3(N[h], N[h]) for h in H]
    n = 2
    while n < C:
        X = [X[h] + _dot3(X[h], P[h]) for h in H]
        n *= 2
        if n < C:
            P = [_dot3(P[h], P[h]) for h in H]
    AV = [_dot3(Ak[h], v[h]) for h in H]
    KS = [_dot3(lhs[h], S0[h]) for h in H]
    U = [_dot3(X[h], KS[h][0:C] + AV[h][0:C]) for h in H]
    Y = [KS[h][C:] + AV[h][C:] - _dot3(Ab[h][C:], U[h]) for h in H]
    gcol = [jnp.sum(jnp.where(eye_k, jnp.broadcast_to(eL[h][C - 1:C, :], (A_HD, A_HD)), 0.0), axis=1, keepdims=True)
            for h in H]
    S1 = [S0[h] * gcol[h] + _dot3(jnp.concatenate([k[h] * eE[h], -(b[h] * eE[h])], axis=0),
                                  jnp.concatenate([v[h], U[h]], axis=0), TN) for h in H]
    out = []
    for h in H:
        yn = _ln(Y[h], RWKV_GN_EPS) * lng_ref[h] + lnb_ref[h]
        bonus = jnp.sum(r[h] * k[h] * rk_ref[h], axis=1, keepdims=True) * v[h]
        out.append((yn + bonus) * g_ref[h])
    y_ref[...] = jnp.concatenate(out, axis=1)
    for h in H:
        S_ref[h] = S1[h]

    @pl.when(c == nc - 1)
    def _():
        for h in H:
            sfin_ref[h] = S1[h]


def _rwkv_chunk_call(pre, s0t, r_k, ln_g, ln_b, row0, B, T, C):
    nc = T // C
    blk0 = row0 // C
    in_spec = pl.BlockSpec((A_HEADS, C, A_HD), lambda b, c: (0, blk0 + b * nc + c, 0))
    hvec = pl.BlockSpec((A_HEADS, 1, A_HD), lambda b, c: (0, 0, 0))
    st = pl.BlockSpec((None, A_HEADS, A_HD, A_HD), lambda b, c: (b, 0, 0, 0))
    return pl.pallas_call(
        functools.partial(_rwkv_chunk_kernel, C=C, nc=nc), grid=(B, nc),
        in_specs=[in_spec] * 7 + [st, hvec, hvec, hvec],
        out_specs=[pl.BlockSpec((C, BRANCH_W), lambda b, c: (b * nc + c, 0)), st],
        out_shape=[jax.ShapeDtypeStruct((B * T, BRANCH_W), F32),
                   jax.ShapeDtypeStruct((B, A_HEADS, A_HD, A_HD), F32)],
        scratch_shapes=[pltpu.VMEM((A_HEADS, A_HD, A_HD), F32)],
        compiler_params=_cp(("arbitrary", "arbitrary")), name="rwkv_chunk_c%d" % C)(
            *pre, s0t, r_k, ln_g, ln_b)


def _foxcum_kernel(fl_ref, bf_ref, lf_ref, cum_ref, carry_ref):
    @pl.when(pl.program_id(0) == 0)
    def _():
        carry_ref[...] = jnp.zeros_like(carry_ref)

    lane = _iota((TM, LANES), 1)
    lf = jnp.where(lane < NH, _log_sigmoid(fl_ref[...] + bf_ref[...]), 0.0)
    tri = (_iota((TM, TM), 0) >= _iota((TM, TM), 1)).astype(BF16)
    cum = _dot_exact_l(tri, lf) + carry_ref[...]
    lf_ref[...] = lf
    cum_ref[...] = cum
    carry_ref[...] = cum[TM - 1:TM, :]


def _foxcum_call(z, bf):
    M = z.shape[0]
    cb = (Z_B + 3 * BRANCH_W) // LANES
    spec = pl.BlockSpec((TM, LANES), lambda i: (i, 0))
    return pl.pallas_call(
        _foxcum_kernel, grid=(M // TM,),
        in_specs=[pl.BlockSpec((TM, LANES), lambda i: (i, cb)), pl.BlockSpec((1, LANES), lambda i: (0, 0))],
        out_specs=[spec, spec], out_shape=[jax.ShapeDtypeStruct((M, LANES), F32)] * 2,
        scratch_shapes=[pltpu.VMEM((1, LANES), F32)],
        compiler_params=_cp(("arbitrary",)), name="fox_cum")(z, bf)


def _attn_prep_kernel(fq_ref, fk_ref, fv_ref, mq_ref, mk_ref, mv_ref, cum_ref, cprev_ref, km_ref,
                      fqT_o, fkA_o, fvT_o, mqT_o, mk_o, mvT_o, sel_o, *, scale):
    i = pl.program_id(0)
    lane = _iota((TM, LANES), 1)
    base = jnp.where(i == 0, 0.0, cprev_ref[7:8, :])
    crel = (cum_ref[...] - base) * (1.0 / scale)
    fq = fq_ref[...]
    fk = fk_ref[...]
    fv = fv_ref[...]
    mq = mq_ref[...]
    mk = mk_ref[...]
    mv = mv_ref[...]
    km = km_ref[...]
    blk = _iota((km.shape[0], TM), 0)
    blk_f = blk.astype(F32)
    for h in range(NH):
        sl = slice(h * HD, (h + 1) * HD)
        c3 = [p.astype(F32) for p in _split3(jnp.broadcast_to(crel[:, h:h + 1], (TM, LANES)))]
        qb = jnp.where(lane == 0, c3[0], jnp.where(lane == 1, c3[1], jnp.where(lane == 2, c3[2],
                                                                             jnp.where(lane < 6, 1.0, 0.0))))
        kb = jnp.where(lane < 3, 1.0, jnp.where(lane == 3, -c3[0], jnp.where(lane == 4, -c3[1],
                                                                              jnp.where(lane == 5, -c3[2], 0.0))))
        fqT_o[h, 0:HD, :] = fq[:, sl].T.astype(BF16)
        fqT_o[h, HD:2 * HD, :] = qb.T.astype(BF16)
        fkA_o[h, :, 0:HD] = fk[:, sl].astype(BF16)
        fkA_o[h, :, HD:2 * HD] = kb.astype(BF16)
        fvT_o[h] = fv[:, sl].T.astype(BF16)
        mqT = mq[:, sl].T
        mqT_o[h] = mqT.astype(BF16)
        mk_o[h] = mk[:, sl].astype(BF16)
        mvT_o[h] = mv[:, sl].T.astype(BF16)
        sc = _dot3(km[:, sl], mqT)
        sel_o[h] = _top3_select(sc, blk < i, blk_f, axis=0)


def _attn_prep_call(z, cum, kmean, Tp):
    W = BRANCH_W
    nbp = kmean.shape[0]
    cbB = Z_B // W
    cbC = Z_C // W
    zs = lambda cb: pl.BlockSpec((TM, W), lambda i: (i, cb))
    colT = lambda rows: pl.BlockSpec((NH, rows, TM), lambda i: (0, 0, i))
    rowm = lambda cols: pl.BlockSpec((NH, TM, cols), lambda i: (0, i, 0))
    sd = jax.ShapeDtypeStruct
    return pl.pallas_call(
        functools.partial(_attn_prep_kernel, scale=HD ** -0.5), grid=(Tp // TM,),
        in_specs=[zs(cbB), zs(cbB + 1), zs(cbB + 2), zs(cbC), zs(cbC + 1), zs(cbC + 2),
                  pl.BlockSpec((TM, LANES), lambda i: (i, 0)),
                  pl.BlockSpec((8, LANES), lambda i: (jnp.maximum(i * (TM // 8) - 1, 0), 0)),
                  pl.BlockSpec((nbp, W), lambda i: (0, 0))],
        out_specs=[colT(2 * HD), rowm(2 * HD), colT(HD), colT(HD), rowm(HD), colT(HD), colT(nbp)],
        out_shape=[sd((NH, 2 * HD, Tp), BF16), sd((NH, Tp, 2 * HD), BF16), sd((NH, HD, Tp), BF16),
                   sd((NH, HD, Tp), BF16), sd((NH, Tp, HD), BF16), sd((NH, HD, Tp), BF16),
                   sd((NH, nbp, Tp), F32)],
        compiler_params=_cp(("arbitrary",)), name="attn_prep")(z, z, z, z, z, z, cum, cum, kmean)


def _flash_kernel(ti_ref, tj_ref, base_ref, qT_ref, kA_ref, vT_ref, *rest, scale, nq, moba):
    if moba:
        sel_ref, o_ref, m_ref, l_ref, acc_ref = rest
    else:
        o_ref, m_ref, l_ref, acc_ref = rest
    step = pl.program_id(0)
    i = ti_ref[step]
    j = tj_ref[step]
    T = kA_ref.shape[1]

    @pl.when(j == 0)
    def _():
        m_ref[...] = jnp.full(m_ref.shape, NEG, F32)
        l_ref[...] = jnp.zeros(l_ref.shape, F32)
        acc_ref[...] = jnp.zeros(acc_ref.shape, F32)

    causal = (_iota((T, T), 0) - _iota((T, T), 1)) <= jnp.where(j < i, 1 << 30, 0)
    for h in range(NH):
        sT = _dot(kA_ref[h], qT_ref[h]) * scale
        if not moba:
            sT = sT + (base_ref[h * nq + i] - base_ref[h * nq + j])
        sT = jnp.where(causal, sT, NEG)
        if moba:
            picked = jnp.where(j < i, sel_ref[h, pl.ds(j, 1), :], 1.0)
            sT = sT + (picked - 1.0) * (-NEG)
        m_prev = m_ref[h]
        m_new = jnp.maximum(m_prev, jnp.max(sT, axis=0, keepdims=True))
        alpha = jnp.exp(m_prev - m_new)
        p = jnp.exp(sT - m_new)
        l_ref[h] = alpha * l_ref[h] + jnp.sum(p, axis=0, keepdims=True)
        acc_ref[h] = alpha * acc_ref[h] + _dot(vT_ref[h], p.astype(BF16))
        m_ref[h] = m_new

    @pl.when(j == i)
    def _():
        for h in range(NH):
            o_ref[:, h * HD:(h + 1) * HD] = (acc_ref[h] / l_ref[h]).T


def _flash_call(qT, kA, vT, base, sel, Tp, name):
    T = MOBA_BLOCK
    nq = Tp // T
    Kc = kA.shape[2]
    moba = sel is not None
    ti = np.concatenate([np.full(i + 1, i, np.int32) for i in range(nq)])
    tj = np.concatenate([np.arange(i + 1, dtype=np.int32) for i in range(nq)])
    in_specs = [pl.BlockSpec((NH, Kc, T), lambda s, ti, tj, bs: (0, 0, ti[s])),
                pl.BlockSpec((NH, T, Kc), lambda s, ti, tj, bs: (0, tj[s], 0)),
                pl.BlockSpec((NH, HD, T), lambda s, ti, tj, bs: (0, 0, tj[s]))]
    args = [qT, kA, vT]
    if moba:
        in_specs.append(pl.BlockSpec((NH, sel.shape[1], T), lambda s, ti, tj, bs: (0, 0, ti[s])))
        args.append(sel)
    return pl.pallas_call(
        functools.partial(_flash_kernel, scale=HD ** -0.5, nq=nq, moba=moba),
        grid_spec=pltpu.PrefetchScalarGridSpec(
            num_scalar_prefetch=3, grid=(len(ti),), in_specs=in_specs,
            out_specs=pl.BlockSpec((T, NH * HD), lambda s, ti, tj, bs: (ti[s], 0)),
            scratch_shapes=[pltpu.VMEM((NH, 1, T), F32), pltpu.VMEM((NH, 1, T), F32),
                            pltpu.VMEM((NH, HD, T), F32)]),
        out_shape=jax.ShapeDtypeStruct((Tp, NH * HD), F32),
        compiler_params=_cp(("arbitrary",)), name=name)(jnp.asarray(ti), jnp.asarray(tj), base, *args)


def _top3_select(sc, valid, idx_f, axis=1):
    sc = jnp.where(valid, sc, -jnp.inf)
    sel = jnp.zeros(sc.shape, F32)
    for _ in range(MOBA_TOPK):
        mx = jnp.max(sc, axis=axis, keepdims=True)
        idx = jnp.min(jnp.where(sc == mx, idx_f, 1e9), axis=axis, keepdims=True)
        hit = idx_f == idx
        sel = jnp.where(hit & valid, 1.0, sel)
        sc = jnp.where(hit, -jnp.inf, sc)
    return sel


def _blockmean_kernel(k_ref, o_ref):
    o_ref[...] = jnp.sum(k_ref[...], axis=0, keepdims=True) * (1.0 / MOBA_BLOCK)


def _blockmean_call(z, Tp):
    nb = Tp // MOBA_BLOCK
    W = BRANCH_W
    cb = Z_C // W + 1
    return pl.pallas_call(
        _blockmean_kernel, grid=(nb,),
        in_specs=[pl.BlockSpec((MOBA_BLOCK, W), lambda i: (i, cb))],
        out_specs=pl.BlockSpec((None, 1, W), lambda i: (i, 0, 0)),
        out_shape=jax.ShapeDtypeStruct((nb, 1, W), F32),
        compiler_params=_cp(("arbitrary",)), name="moba_kmean")(z)


def _ret_kernel(q_ref, k_ref, v_ref, gd_ref, cos_ref, sin_ref, din_ref, qd_ref, kd_ref, cd_ref, s0_ref, lng_ref,
                y_ref, sfin_ref, S_ref, *, nc):
    c = pl.program_id(1)

    @pl.when(c == 0)
    def _():
        S_ref[...] = s0_ref[...]

    q = q_ref[...]
    k = k_ref[...]
    v = v_ref[...]
    gd = gd_ref[...]
    cos = cos_ref[...]
    sin = sin_ref[...]
    lng = lng_ref[...]
    for h in range(NH):
        sl = slice(h * HD, (h + 1) * HD)
        qh = q[:, sl]
        kh = k[:, sl]
        qr = qh * cos + pltpu.roll(qh, HD // 2, axis=1) * sin
        kr = (kh * cos + pltpu.roll(kh, HD // 2, axis=1) * sin) * (HD ** -0.5)
        vb = v[:, sl].astype(BF16)
        qb = qr.astype(BF16)
        att = _dot(qb, kr.astype(BF16), NT) * din_ref[h]
        S = S_ref[h]
        o = _dot(att.astype(BF16), vb) + _dot(qb, S.astype(BF16)) * qd_ref[h]
        S_ref[h] = S * cd_ref[h] + _dot((kr * kd_ref[h]).astype(BF16), vb, TN)
        y_ref[:, sl] = _ln(o, RET_GN_EPS) * lng[:, sl] * (gd[:, sl] * _sigmoid(gd[:, sl]))

    @pl.when(c == nc - 1)
    def _():
        sfin_ref[...] = S_ref[...]


def _ret_tables(C):
    lg = np.log(1.0 - 2.0 ** (-5.0 - np.arange(NH, dtype=np.float32))).astype(np.float32)
    i = np.arange(C, dtype=np.float32)
    diff = i[:, None] - i[None, :]
    din = np.where(diff[None] >= 0, np.exp(np.maximum(diff, 0.0)[None] * lg[:, None, None]), 0.0)
    qd = np.exp((i[None, :] + 1.0) * lg[:, None])
    kd = np.exp((C - 1.0 - i)[None, :] * lg[:, None])
    cd = np.exp(C * lg)
    bc = lambda t: np.ascontiguousarray(np.broadcast_to(t[:, :, None], (NH, C, HD))).astype(np.float32)
    cdb = np.ascontiguousarray(np.broadcast_to(cd[:, None, None], (NH, 1, HD))).astype(np.float32)
    return din.astype(np.float32), bc(qd), bc(kd), cdb


def _rope_tables(pos0, T):
    half = HD // 2
    inv = 1.0 / (ROPE_BASE ** (jnp.arange(half, dtype=F32) / half))
    ang = (pos0 + jnp.arange(T)).astype(F32)[:, None] * inv[None, :]
    cos = jnp.cos(ang)
    sin = jnp.sin(ang)
    return jnp.concatenate([cos, cos], axis=1), jnp.concatenate([-sin, sin], axis=1)


def _ret_call(z, s0, ln_g, l, row0, B, T, C, pos0):
    nc = T // C
    W = BRANCH_W
    cb = Z_D // W
    blk0 = row0 // C
    cos, sin = _rope_tables(pos0, T)
    din, qd, kd, cd = _ret_tables(C)
    zs = lambda off: pl.BlockSpec((C, W), lambda b, c: (blk0 + b * nc + c, cb + off))
    tab = pl.BlockSpec((C, HD), lambda b, c: (c, 0))
    full = lambda shape: pl.BlockSpec(shape, lambda b, c: (0,) * len(shape))
    st = pl.BlockSpec((None, NH, HD, HD), lambda b, c: (b, 0, 0, 0))
    return pl.pallas_call(
        functools.partial(_ret_kernel, nc=nc), grid=(B, nc),
        in_specs=[zs(0), zs(1), zs(2), zs(3), tab, tab, full((NH, C, C)), full((NH, C, HD)), full((NH, C, HD)),
                  full((NH, 1, HD)), st, pl.BlockSpec((None, 1, W), lambda b, c: (l, 0, 0))],
        out_specs=[pl.BlockSpec((C, W), lambda b, c: (b * nc + c, 0)), st],
        out_shape=[jax.ShapeDtypeStruct((B * T, W), F32), jax.ShapeDtypeStruct((B, NH, HD, HD), F32)],
        scratch_shapes=[pltpu.VMEM((NH, HD, HD), F32)],
        compiler_params=_cp(("arbitrary", "arbitrary")), name="ret_c%d" % C)(
            z, z, z, z, cos, sin, din, qd, kd, cd, s0, ln_g)


def _lfsuf_kernel(pt_ref, lf_hbm, ts_ref, tot_ref, o_ref, buf, sem, *, npg, base):
    b = pl.program_id(0)

    def page_copy(p):
        return pltpu.make_async_copy(lf_hbm.at[pl.ds(NH * (base + pt_ref[b * npg + p]), NH), :],
                                     buf.at[pl.ds(NH * p, NH), :], sem.at[p])

    for p in range(npg):
        page_copy(p).start()
    for p in range(npg):
        page_copy(p).wait()
    loc = None
    tot = None
    for h in range(NH):
        x = buf[pl.ds(h, npg, stride=NH), :]
        a = _dot_exact_r(x, ts_ref[h])
        t = _dot_exact_r(x, tot_ref[h])
        loc = a if loc is None else loc + a
        tot = t if tot is None else tot + t
    later = (_iota((npg, npg), 1) > _iota((npg, npg), 0)).astype(BF16)
    o_ref[...] = loc + _dot_exact_l(later, tot)


def _lfsuf_call(pt_flat, lf_rows, ts, tot, B, npg, base):
    PW = PAGE_SIZE * NH
    sel = pl.BlockSpec((NH, PAGE_SIZE, PW), lambda b, pt: (0, 0, 0))
    return pl.pallas_call(
        functools.partial(_lfsuf_kernel, npg=npg, base=base),
        grid_spec=pltpu.PrefetchScalarGridSpec(
            num_scalar_prefetch=1, grid=(B,),
            in_specs=[pl.BlockSpec(memory_space=pl.ANY), sel, sel],
            out_specs=pl.BlockSpec((None, npg, PW), lambda b, pt: (b, 0, 0)),
            scratch_shapes=[pltpu.VMEM((NH * npg, PAGE_SIZE), F32), pltpu.SemaphoreType.DMA((npg,))]),
        out_shape=jax.ShapeDtypeStruct((B, npg, PW), F32),
        compiler_params=_cp(("arbitrary",)), name="fox_logf_suffix")(pt_flat, lf_rows, ts, tot)


def _suffix_matrices():
    r = np.arange(PAGE_SIZE)
    dst_r = np.repeat(r, NH)
    dst_h = np.tile(np.arange(NH), PAGE_SIZE)
    same = np.arange(NH)[:, None, None] == dst_h[None, None, :]
    ts = same & (r[None, :, None] > dst_r[None, None, :])
    return jnp.asarray(ts, BF16), jnp.asarray(np.broadcast_to(same, ts.shape), BF16)


def _sattn_kernel(pt_ref, fq_ref, fkn_ref, fvn_ref, lfn_ref, mq_ref, mkn_ref, mvn_ref, R_ref,
                  fk0, fk1, fv0, fv1, mk0, mk1, mv0, mv1,
                  yb_ref, yc_ref,
                  fqb, mqb, mqf, cnb, fm, fl, facc, sc_all, m_all, l_all, o_all, *, nb, S, scale):
    n = pl.program_id(1)
    R4 = NH * S
    PW = PAGE_SIZE * NH
    own_head = (_iota((R4, PW), 1) % NH) == (_iota((R4, PW), 0) // S)
    lane = _iota((R4, LANES), 1)

    def stack_heads(x):
        return jnp.concatenate([x[:, h * HD:(h + 1) * HD] for h in range(NH)], axis=0)

    def head_sums(kpage):
        return jnp.sum(kpage.reshape(PW // 8, 8, HD), axis=0)

    @pl.when(n == 0)
    def _():
        fqb[...] = stack_heads(fq_ref[...]).astype(BF16)
        mq = stack_heads(mq_ref[...])
        mqb[...] = mq.astype(BF16)
        mqf[...] = mq
        tri = (_iota((S, S), 0) >= _iota((S, S), 1)).astype(BF16)
        cn = _dot_exact_l(tri, lfn_ref[...])
        cnb[...] = jnp.concatenate([jnp.broadcast_to(cn[:, h:h + 1], (S, LANES)) for h in range(NH)], axis=0)
        fm[...] = jnp.full(fm.shape, NEG, F32)
        fl[...] = jnp.zeros(fl.shape, F32)
        facc[...] = jnp.zeros(facc.shape, F32)
        sc_all[...] = jnp.full(sc_all.shape, -jnp.inf, F32)
        m_all[...] = jnp.full(m_all.shape, NEG, F32)
        l_all[...] = jnp.zeros(l_all.shape, F32)

    def fox_update(s, v_bf):
        m_prev = fm[...]
        m_new = jnp.maximum(m_prev, jnp.max(s, axis=1, keepdims=True))
        alpha = jnp.exp(m_prev - m_new)
        p = jnp.exp(s - m_new)
        fl[...] = alpha * fl[...] + jnp.sum(p, axis=1, keepdims=True)
        facc[...] = alpha * facc[...] + _dot(p.astype(BF16), v_bf)
        fm[...] = m_new

    cn_b = cnb[...]
    cn_w = jnp.concatenate([cn_b] * NH, axis=1)
    for half, (k_ref, v_ref) in enumerate(((fk0, fv0), (fk1, fv1))):
        s = _dot(fqb[...], k_ref[...].astype(BF16), NT) * scale
        s = s + cn_w + R_ref[pl.ds(2 * n + half, 1), :]
        fox_update(jnp.where(own_head, s, NEG), v_ref[...].astype(BF16))

    k0 = mk0[...]
    k1 = mk1[...]
    ks = head_sums(k0) + head_sums(k1)
    kmean = (ks[0:NH] + ks[NH:2 * NH]) * (1.0 / MOBA_BLOCK)
    kmean = jnp.concatenate([jnp.broadcast_to(kmean[h:h + 1], (S, HD)) for h in range(NH)], axis=0)
    sc_col = jnp.sum(mqf[...] * kmean, axis=1, keepdims=True)
    s0 = jnp.where(own_head, _dot(mqb[...], k0.astype(BF16), NT) * scale, NEG)
    s1 = jnp.where(own_head, _dot(mqb[...], k1.astype(BF16), NT) * scale, NEG)
    m_b = jnp.maximum(jnp.max(s0, axis=1, keepdims=True), jnp.max(s1, axis=1, keepdims=True))
    p0 = jnp.exp(s0 - m_b)
    p1 = jnp.exp(s1 - m_b)
    here = lane == n
    sc_all[...] = jnp.where(here, sc_col, sc_all[...])
    m_all[...] = jnp.where(here, m_b, m_all[...])
    l_all[...] = jnp.where(here, jnp.sum(p0, axis=1, keepdims=True) + jnp.sum(p1, axis=1, keepdims=True), l_all[...])
    o_all[n] = _dot(p0.astype(BF16), mv0[...].astype(BF16)) + _dot(p1.astype(BF16), mv1[...].astype(BF16))

    @pl.when(n == nb - 1)
    def _():
        ri = _iota((R4, R4), 0)
        ci = _iota((R4, R4), 1)
        causal = ((ri // S) == (ci // S)) & ((ci % S) <= (ri % S))
        cn_col = cn_b[:, 0:1]
        cn_row = jnp.sum(jnp.where(ri == ci, jnp.broadcast_to(cn_col, (R4, R4)), 0.0), axis=0, keepdims=True)
        s = _dot(fqb[...], stack_heads(fkn_ref[...]).astype(BF16), NT) * scale
        s = s + (cn_col - cn_row)
        fox_update(jnp.where(causal, s, NEG), stack_heads(fvn_ref[...]).astype(BF16))
        out = facc[...] / fl[...]
        for h in range(NH):
            yb_ref[:, h * HD:(h + 1) * HD] = out[h * S:(h + 1) * S, :]

        s = _dot(mqb[...], stack_heads(mkn_ref[...]).astype(BF16), NT) * scale
        s = jnp.where(causal, s, NEG)
        m_o = jnp.max(s, axis=1, keepdims=True)
        p = jnp.exp(s - m_o)
        l_o = jnp.sum(p, axis=1, keepdims=True)
        o_o = _dot(p.astype(BF16), stack_heads(mvn_ref[...]).astype(BF16))
        sel = _top3_select(sc_all[...], lane < nb, lane.astype(F32)) > 0.5
        m_past = m_all[...]
        m_tot = jnp.maximum(jnp.max(jnp.where(sel, m_past, NEG), axis=1, keepdims=True), m_o)
        w = jnp.where(sel, jnp.exp(m_past - m_tot), 0.0)
        w_o = jnp.exp(m_o - m_tot)
        den = jnp.sum(w * l_all[...], axis=1, keepdims=True) + w_o * l_o
        num = w_o * o_o
        for blk in range(nb):
            num = num + w[:, blk:blk + 1] * o_all[blk]
        out = num / den
        for h in range(NH):
            yc_ref[:, h * HD:(h + 1) * HD] = out[h * S:(h + 1) * S, :]


def _sattn_call(pt_flat, z, lf, R, caches, l, Mp, B, S, npg, n_pool):
    fk, fv, mk, mv = caches
    W = BRANCH_W
    PW = PAGE_SIZE * NH
    nb = npg // 2
    R4 = NH * S
    rb = Mp // S
    base = l * n_pool
    zrow = lambda cb: pl.BlockSpec((S, W), lambda b, n, pt: (rb + b, cb))
    page = lambda half: pl.BlockSpec((PW, HD), lambda b, n, pt: (base + pt[b * npg + 2 * n + half], 0))
    cbB = Z_B // W
    cbC = Z_C // W
    in_specs = [zrow(cbB), zrow(cbB + 1), zrow(cbB + 2),
                pl.BlockSpec((S, LANES), lambda b, n, pt: (rb + b, 0)),
                zrow(cbC), zrow(cbC + 1), zrow(cbC + 2),
                pl.BlockSpec((None, npg, PW), lambda b, n, pt: (b, 0, 0)),
                page(0), page(1), page(0), page(1), page(0), page(1), page(0), page(1)]
    out_spec = pl.BlockSpec((S, W), lambda b, n, pt: (b, 0))
    scratch = [pltpu.VMEM((R4, HD), BF16), pltpu.VMEM((R4, HD), BF16), pltpu.VMEM((R4, HD), F32),
               pltpu.VMEM((R4, LANES), F32),
               pltpu.VMEM((R4, 1), F32), pltpu.VMEM((R4, 1), F32), pltpu.VMEM((R4, HD), F32),
               pltpu.VMEM((R4, LANES), F32), pltpu.VMEM((R4, LANES), F32), pltpu.VMEM((R4, LANES), F32),
               pltpu.VMEM((nb, R4, HD), F32)]
    return pl.pallas_call(
        functools.partial(_sattn_kernel, nb=nb, S=S, scale=HD ** -0.5),
        grid_spec=pltpu.PrefetchScalarGridSpec(
            num_scalar_prefetch=1, grid=(B, nb), in_specs=in_specs,
            out_specs=[out_spec, out_spec], scratch_shapes=scratch),
        out_shape=[jax.ShapeDtypeStruct((B * S, W), F32)] * 2,
        compiler_params=_cp(("arbitrary", "arbitrary")), name="sample_attn")(
            pt_flat, z, z, z, lf, z, z, z, R, fk, fk, fv, fv, mk, mk, mv, mv)


def _pick_tile(m, cands):
    for t in cands:
        if m % t == 0:
            return t
    raise ValueError("no row tile divides %d" % m)


def _pad_cols(x, n):
    return jnp.pad(x, ((0, 0), (0, n - x.shape[1])))


def kernel(x_prompt, x_sample, c_prompt, c_sample, page_table, cache_fox_k, cache_fox_v, cache_fox_logf, cache_moba_k, cache_moba_v, state_rwkv, state_rwkv_shift, state_ret, w_ada, b_ada, w_in, rwkv_mu, rwkv_w0, rwkv_w2, rwkv_a0, rwkv_a2, rwkv_g2, rwkv_k_k, rwkv_k_a, rwkv_r_k, rwkv_ln_g, rwkv_ln_b, fox_bf, ret_ln_g, w_branch, w_o, ln1_g, ln1_b, ln2_g, ln2_b, w_ffn_gate, w_ffn_in, w_ffn_out):
    Bp, Tp, D = x_prompt.shape
    Bs, S, _ = x_sample.shape
    depth = w_in.shape[0]
    n_pool = cache_fox_k.shape[1]
    npg = page_table.shape[1]
    past = npg * PAGE_SIZE
    Mp, Ms = Bp * Tp, Bs * S
    M = Mp + Ms
    assert Bp == 1 and D == D_MODEL and depth == DEPTH
    assert Mp % TM == 0 and Ms % TM == 0 and Tp % MOBA_BLOCK == 0 and Tp // MOBA_BLOCK <= LANES
    assert Tp % RWKV_CHUNK == 0 and Tp % RET_CHUNK == 0 and S == 8 and past % MOBA_BLOCK == 0
    assert TM == MOBA_BLOCK and 2 * NH == 8 and MOBA_BLOCK == 2 * PAGE_SIZE
    n_p = Mp // TM
    W = BRANCH_W
    tm_big = _pick_tile(M, (768, 512, 256))
    tm_small = _pick_tile(M, (384, 256))

    x = jnp.concatenate([x_prompt.reshape(Mp, D), x_sample.reshape(Ms, D)], axis=0)
    c_rows = 8 * (-(-(Bp + Bs) // 8))
    c_all = jnp.pad(jnp.concatenate([c_prompt, c_sample], axis=0), ((0, c_rows - Bp - Bs), (0, 0)))
    pt_flat = page_table.reshape(-1).astype(jnp.int32)
    cache2 = lambda t: t.reshape(depth * n_pool * PAGE_SIZE * NH, HD)
    caches = (cache2(cache_fox_k), cache2(cache_fox_v), cache2(cache_moba_k), cache2(cache_moba_v))
    lf_rows = jnp.swapaxes(cache_fox_logf, 2, 3).reshape(depth * n_pool * NH, PAGE_SIZE)
    ts_m, tot_m = _suffix_matrices()
    e64 = jnp.asarray(np.kron(np.eye(A_HEADS), np.ones((A_HD, A_HD))), BF16)
    b_ada3 = b_ada.reshape(depth, 1, -1)
    vec3 = lambda t: t.reshape(depth, 1, -1)
    ln1_g3, ln1_b3, ln2_g3, ln2_b3, ret_g3 = map(vec3, (ln1_g, ln1_b, ln2_g, ln2_b, ret_ln_g))

    per_layer = []
    for l in range(depth):
        ada = _ada_call(c_all, w_ada, b_ada3, l)
        ada_p = ada[0:8]
        mod_s = jnp.repeat(ada[Bp:Bp + Bs], S, axis=0)
        h = _lnmod_call(x, ada_p, mod_s, n_p, 1, 0)
        wl = w_in[l]
        o_b, o_c, o_d = A_COLS, A_COLS + B_COLS, A_COLS + B_COLS + C_COLS
        o_g = o_d + D_COLS
        w_in_p = jnp.concatenate(
            [wl[:, o_g:], _pad_cols(wl[:, :o_b], 2048), wl[:, o_d:o_g], wl[:, o_c:o_d],
             _pad_cols(wl[:, o_b:o_c], 1664)], axis=1)
        z = _mm_call(h, w_in_p, 0, tm_big, 1408, F32, "w_in")

        fill_s = jnp.repeat(_pad_cols(state_rwkv_shift[l], 2048), S, axis=0)
        pad_rows = lambda w2, r0: jnp.zeros((W, W), F32).at[r0:r0 + w2.shape[0]].set(w2)
        pre = _rwkv_pre_call(
            z, fill_s, _pad_cols(rwkv_mu[l][None], 2048), rwkv_w0[l][None], rwkv_a0[l][None],
            rwkv_k_k[l][None], rwkv_k_a[l][None], pad_rows(rwkv_w2[l], 0), pad_rows(rwkv_a2[l], A_LORA_W),
            pad_rows(rwkv_g2[l], A_LORA_W + A_LORA_A), e64, n_p, S)
        hv = lambda t: t.reshape(A_HEADS, 1, A_HD)
        rk, lg, lb = hv(rwkv_r_k[l]), hv(rwkv_ln_g[l]), hv(rwkv_ln_b[l])
        ya_p, sT_p = _rwkv_chunk_call(pre, jnp.zeros((Bp, A_HEADS, A_HD, A_HD), F32), rk, lg, lb,
                                      0, Bp, Tp, RWKV_CHUNK)
        ya_s, sT_s = _rwkv_chunk_call(pre, jnp.swapaxes(state_rwkv[l], -1, -2), rk, lg, lb, Mp, Bs, S, S)
        ya = jnp.concatenate([ya_p, ya_s], axis=0)

        lf, cum = _foxcum_call(z, _pad_cols(fox_bf[l][None], LANES))
        nbk = Tp // MOBA_BLOCK
        kmean = _blockmean_call(z, Tp).reshape(nbk, W)
        kmean = jnp.pad(kmean, ((0, 8 * (-(-nbk // 8)) - nbk), (0, 0)))
        fqT, fkA, fvT, mqT, mkb, mvT, sel = _attn_prep_call(z, cum, kmean, Tp)
        base = jnp.concatenate([jnp.zeros((1, NH), F32), cum[MOBA_BLOCK - 1:Mp - 1:MOBA_BLOCK, :NH]], axis=0)
        base = jnp.transpose(base).reshape(-1)
        yb_p = _flash_call(fqT, fkA, fvT, base, None, Tp, "fox_prompt")
        yc_p = _flash_call(mqT, mkb, mvT, base, sel, Tp, "moba_prompt")
        R = _lfsuf_call(pt_flat, lf_rows, ts_m, tot_m, Bs, npg, l * n_pool)
        yb_s, yc_s = _sattn_call(pt_flat, z, lf, R, caches, l, Mp, Bs, S, npg, n_pool)
        yb = jnp.concatenate([yb_p, yb_s], axis=0)
        yc = jnp.concatenate([yc_p, yc_s], axis=0)

        yd_p, rS_p = _ret_call(z, jnp.zeros((Bp, NH, HD, HD), F32), ret_g3, l, 0, Bp, Tp, RET_CHUNK, 0)
        yd_s, rS_s = _ret_call(z, state_ret[l], ret_g3, l, Mp, Bs, S, S, past)
        yd = jnp.concatenate([yd_p, yd_s], axis=0)

        mixin = _merge_call((ya, yb, yc, yd), w_branch, z, l, tm_big, 512)
        mix = _mm_call(mixin, w_o, l, tm_big, 512, F32, "w_o")
        x1, h2 = _resid_call(x, mix, ada_p, mod_s, ln1_g3, ln1_b3, l, n_p, 2, 4, 3)
        act = _ffn_call(h2, w_ffn_gate, w_ffn_in, l, tm_big, 512)
        f = _mm_call(act, w_ffn_out, l, tm_small, 512, F32, "ffn_out")
        x = _resid_call(x1, f, ada_p, mod_s, ln2_g3, ln2_b3, l, n_p, 5)

        zb = Z_B
        zc = Z_C
        hp = lambda t: t.reshape(Bp, Tp, NH, HD)
        hs = lambda t: t.reshape(Bs, S, NH, HD)
        per_layer.append(dict(
            fox_k_p=hp(z[:Mp, zb + W:zb + 2 * W]), fox_v_p=hp(z[:Mp, zb + 2 * W:zb + 3 * W]),
            fox_lf_p=lf[:Mp, :NH].reshape(Bp, Tp, NH),
            moba_k_p=hp(z[:Mp, zc + W:zc + 2 * W]), moba_v_p=hp(z[:Mp, zc + 2 * W:zc + 3 * W]),
            rwkv_S_p=jnp.swapaxes(sT_p, -1, -2), shift_p=z[Mp - 1:Mp, Z_A:Z_A + A_COLS], ret_S_p=rS_p,
            fox_k_s=hs(z[Mp:, zb + W:zb + 2 * W]), fox_v_s=hs(z[Mp:, zb + 2 * W:zb + 3 * W]),
            fox_lf_s=lf[Mp:, :NH].reshape(Bs, S, NH),
            moba_k_s=hs(z[Mp:, zc + W:zc + 2 * W]), moba_v_s=hs(z[Mp:, zc + 2 * W:zc + 3 * W]),
            rwkv_S_s=jnp.swapaxes(sT_s, -1, -2),
            shift_s=z[Mp:, Z_A:Z_A + A_COLS].reshape(Bs, S, A_COLS)[:, S - 1], ret_S_s=rS_s))

    st = lambda name: jnp.stack([p[name] for p in per_layer])
    return (x[:Mp].reshape(Bp, Tp, D), x[Mp:].reshape(Bs, S, D),
            st("fox_k_p"), st("fox_v_p"), st("fox_lf_p"), st("moba_k_p"), st("moba_v_p"),
            st("rwkv_S_p"), st("shift_p"), st("ret_S_p"),
            st("fox_k_s"), st("fox_v_s"), st("fox_lf_s"), st("moba_k_s"), st("moba_v_s"),
            st("rwkv_S_s"), st("shift_s"), st("ret_S_s"))
```

```python
import functools

import numpy as np
import jax
import jax.numpy as jnp
from jax import lax
from jax.experimental import pallas as pl
from jax.experimental.pallas import tpu as pltpu

F32 = jnp.float32
BF16 = jnp.bfloat16

D_MODEL = 2048
DEPTH = 2
PAGE_SIZE = 128
BRANCH_W = 512
A_HD = 64
A_HEADS = 8
A_LORA_W = 96
A_LORA_A = 96
A_LORA_G = 256
A_COLS = 3 * BRANCH_W + A_LORA_W + A_LORA_A + A_LORA_G
B_COLS = 3 * BRANCH_W + 4
C_COLS = 3 * BRANCH_W
D_COLS = 4 * BRANCH_W
HD = 128
NH = 4
RWKV_GN_EPS = 64e-5
RET_GN_EPS = 1e-6
LN_EPS = 1e-5
MOBA_BLOCK = 256
MOBA_TOPK = 3
RET_CHUNK = 128
RWKV_CHUNK = 64
ROPE_BASE = 10000.0
FFN_HIDDEN = 5632
ALPHA = (2 * DEPTH) ** 0.25

Z_G = 0
Z_A = 4 * D_MODEL
Z_D = Z_A + 2048
Z_C = Z_D + 2048
Z_B = Z_C + 1536
Z_COLS = Z_B + 1664

TM = 256
FLASH_TQ = 512
SAMPLE_PAGES = 4
LANES = 128
NEG = -1e30
LOG2E = 1.4426950408889634
VMEM_LIMIT = 56 << 20

NN = (((1,), (0,)), ((), ()))
NT = (((1,), (1,)), ((), ()))
TN = (((0,), (0,)), ((), ()))


def _cp(sem):
    return pltpu.CompilerParams(dimension_semantics=sem, vmem_limit_bytes=VMEM_LIMIT)


def _dot(a, b, dims=NN):
    return lax.dot_general(a, b, dims, preferred_element_type=F32)


def _split2(x):
    hi = x.astype(BF16)
    return hi, (x - hi.astype(F32)).astype(BF16)


def _split3(x):
    hi = x.astype(BF16)
    r1 = x - hi.astype(F32)
    mid = r1.astype(BF16)
    return hi, mid, (r1 - mid.astype(F32)).astype(BF16)


def _dot3(a, b, dims=NN):
    ah, al = _split2(a)
    bh, bl = _split2(b)
    return _dot(ah, bh, dims) + (_dot(ah, bl, dims) + _dot(al, bh, dims))


def _dot1(a, b, dims=NN):
    return _dot(a.astype(BF16), b.astype(BF16), dims)


def _dot_exact_l(m_bf16, x, n=3):
    parts = _split3(x) if n == 3 else _split2(x)
    acc = _dot(m_bf16, parts[0])
    for p in parts[1:]:
        acc = acc + _dot(m_bf16, p)
    return acc


def _dot_exact_r(x, m_bf16, n=3):
    parts = _split3(x) if n == 3 else _split2(x)
    acc = _dot(parts[0], m_bf16)
    for p in parts[1:]:
        acc = acc + _dot(p, m_bf16)
    return acc


def _ln(x, eps):
    mu = jnp.mean(x, axis=-1, keepdims=True)
    xc = x - mu
    var = jnp.mean(xc * xc, axis=-1, keepdims=True)
    return xc * lax.rsqrt(var + eps)


def _sigmoid(x):
    return 1.0 / (1.0 + jnp.exp(-x))


def _log_sigmoid(x):
    return jnp.minimum(x, 0.0) - jnp.log(1.0 + jnp.exp(-jnp.abs(x)))


def _iota(shape, dim):
    return lax.broadcasted_iota(jnp.int32, shape, dim)


def _ada_kernel(c_ref, w_ref, b_ref, o_ref):
    c = c_ref[...]
    s = (c * _sigmoid(c)).astype(BF16)
    o_ref[...] = _dot(s, w_ref[...].astype(BF16)) + b_ref[...]


def _ada_call(c_all, w_ada, b_ada, l):
    R, D = c_all.shape
    N = w_ada.shape[2]
    tn = 2048
    return pl.pallas_call(
        _ada_kernel, grid=(N // tn,),
        in_specs=[pl.BlockSpec((R, D), lambda j: (0, 0)),
                  pl.BlockSpec((None, D, tn), lambda j: (l, 0, j)),
                  pl.BlockSpec((None, 1, tn), lambda j: (l, 0, j))],
        out_specs=pl.BlockSpec((R, tn), lambda j: (0, j)),
        out_shape=jax.ShapeDtypeStruct((R, N), F32),
        compiler_params=_cp(("arbitrary",)), name="ada")(c_all, w_ada, b_ada)


def _lnmod_kernel(x_ref, scp_ref, shp_ref, scs_ref, shs_ref, o_ref, *, n_p):
    is_s = pl.program_id(0) >= n_p
    sc = jnp.where(is_s, scs_ref[...], scp_ref[0:1, :])
    sh = jnp.where(is_s, shs_ref[...], shp_ref[0:1, :])
    o_ref[...] = (_ln(x_ref[...], LN_EPS) * (1.0 + sc) + sh).astype(BF16)


def _mod_specs(n_p, cols):
    specs = []
    for cb in cols:
        specs.append(pl.BlockSpec((8, D_MODEL), lambda i, cb=cb: (0, cb)))
    for cb in cols:
        specs.append(pl.BlockSpec((TM, D_MODEL), lambda i, cb=cb: (jnp.maximum(i - n_p, 0), cb)))
    return specs


def _lnmod_call(x, ada_p, mod_s, n_p, sc_col, sh_col):
    M, D = x.shape
    return pl.pallas_call(
        functools.partial(_lnmod_kernel, n_p=n_p), grid=(M // TM,),
        in_specs=[pl.BlockSpec((TM, D), lambda i: (i, 0))] + _mod_specs(n_p, (sc_col, sh_col)),
        out_specs=pl.BlockSpec((TM, D), lambda i: (i, 0)),
        out_shape=jax.ShapeDtypeStruct((M, D), BF16),
        compiler_params=_cp(("arbitrary",)), name="lnmod")(x, ada_p, ada_p, mod_s, mod_s)


def _mm_kernel(a_ref, w_ref, o_ref, wb_ref):
    @pl.when(pl.program_id(1) == 0)
    def _():
        wb_ref[...] = w_ref[...].astype(BF16)

    o_ref[...] = _dot(a_ref[...].astype(BF16), wb_ref[...]).astype(o_ref.dtype)


def _mm_call(a, w, w_index, tm, tn, out_dtype, name):
    M, K = a.shape
    N = w.shape[-1]
    if w.ndim == 3:
        w_spec = pl.BlockSpec((None, K, tn), lambda j, i: (w_index, 0, j))
    else:
        w_spec = pl.BlockSpec((K, tn), lambda j, i: (0, j))
    return pl.pallas_call(
        _mm_kernel, grid=(N // tn, M // tm),
        in_specs=[pl.BlockSpec((tm, K), lambda j, i: (i, 0)), w_spec],
        out_specs=pl.BlockSpec((tm, tn), lambda j, i: (i, j)),
        out_shape=jax.ShapeDtypeStruct((M, N), out_dtype),
        scratch_shapes=[pltpu.VMEM((K, tn), BF16)],
        compiler_params=_cp(("arbitrary", "arbitrary")), name=name)(a, w)


def _resid_kernel(*refs, n_p, with_h):
    if with_h:
        (x_ref, mix_ref, gp_ref, scp_ref, shp_ref, gs_ref, scs_ref, shs_ref, lg_ref, lb_ref,
         x1_ref, h_ref) = refs
    else:
        x_ref, mix_ref, gp_ref, gs_ref, lg_ref, lb_ref, x1_ref = refs
    is_s = pl.program_id(0) >= n_p
    g = jnp.where(is_s, gs_ref[...], gp_ref[0:1, :])
    y = ALPHA * x_ref[...] + (1.0 + g) * mix_ref[...]
    x1 = _ln(y, LN_EPS) * lg_ref[...] + lb_ref[...]
    x1_ref[...] = x1
    if with_h:
        sc = jnp.where(is_s, scs_ref[...], scp_ref[0:1, :])
        sh = jnp.where(is_s, shs_ref[...], shp_ref[0:1, :])
        h_ref[...] = (_ln(x1, LN_EPS) * (1.0 + sc) + sh).astype(BF16)


def _resid_call(x, mix, ada_p, mod_s, ln_g, ln_b, l, n_p, g_col, sc_col=None, sh_col=None):
    M, D = x.shape
    with_h = sc_col is not None
    cols = (g_col, sc_col, sh_col) if with_h else (g_col,)
    row = pl.BlockSpec((TM, D), lambda i: (i, 0))
    vec = pl.BlockSpec((None, 1, D), lambda i: (l, 0, 0))
    in_specs = [row, row] + _mod_specs(n_p, cols) + [vec, vec]
    args = [x, mix] + [ada_p] * len(cols) + [mod_s] * len(cols) + [ln_g, ln_b]
    out_shape = [jax.ShapeDtypeStruct((M, D), F32)]
    out_specs = [row]
    if with_h:
        out_shape.append(jax.ShapeDtypeStruct((M, D), BF16))
        out_specs.append(row)
    res = pl.pallas_call(
        functools.partial(_resid_kernel, n_p=n_p, with_h=with_h), grid=(M // TM,),
        in_specs=in_specs, out_specs=out_specs, out_shape=out_shape,
        compiler_params=_cp(("arbitrary",)), name="resid_h" if with_h else "resid")(*args)
    return res if with_h else res[0]


def _ffn_kernel(h_ref, wg_ref, wi_ref, o_ref, wgb_ref, wib_ref):
    @pl.when(pl.program_id(1) == 0)
    def _():
        wgb_ref[...] = wg_ref[...].astype(BF16)
        wib_ref[...] = wi_ref[...].astype(BF16)

    h = h_ref[...]
    a = _dot(h, wgb_ref[...])
    b = _dot(h, wib_ref[...])
    o_ref[...] = (a * _sigmoid(a) * b).astype(BF16)


def _ffn_call(h, wg, wi, l, tm, tn):
    M, K = h.shape
    N = wg.shape[-1]
    w_spec = pl.BlockSpec((None, K, tn), lambda j, i: (l, 0, j))
    return pl.pallas_call(
        _ffn_kernel, grid=(N // tn, M // tm),
        in_specs=[pl.BlockSpec((tm, K), lambda j, i: (i, 0)), w_spec, w_spec],
        out_specs=pl.BlockSpec((tm, tn), lambda j, i: (i, j)),
        out_shape=jax.ShapeDtypeStruct((M, N), BF16),
        scratch_shapes=[pltpu.VMEM((K, tn), BF16), pltpu.VMEM((K, tn), BF16)],
        compiler_params=_cp(("arbitrary", "arbitrary")), name="ffn_act")(h, wg, wi)


def _merge_kernel(ya_ref, yb_ref, yc_ref, yd_ref, wb_ref, g0_ref, g1_ref, g2_ref, g3_ref, o_ref, wbb_ref):
    @pl.when(pl.program_id(1) == 0)
    def _():
        wbb_ref[...] = wb_ref[...].astype(BF16)

    acc = None
    for n, (y_ref, g_ref) in enumerate(((ya_ref, g0_ref), (yb_ref, g1_ref), (yc_ref, g2_ref), (yd_ref, g3_ref))):
        up = _dot(y_ref[...].astype(BF16), wbb_ref[n])
        t = _sigmoid(g_ref[...]) * up
        acc = t if acc is None else acc + t
    o_ref[...] = acc.astype(BF16)


def _merge_call(ys, w_branch, z, l, tm, tn):
    M = z.shape[0]
    W = BRANCH_W
    D = D_MODEL
    nj = D // tn
    y_spec = pl.BlockSpec((tm, W), lambda j, i: (i, 0))
    g_specs = [pl.BlockSpec((tm, tn), lambda j, i, n=n: (i, (Z_G + n * D) // tn + j)) for n in range(4)]
    return pl.pallas_call(
        _merge_kernel, grid=(nj, M // tm),
        in_specs=[y_spec] * 4 + [pl.BlockSpec((None, 4, W, tn), lambda j, i: (l, 0, 0, j))] + g_specs,
        out_specs=pl.BlockSpec((tm, tn), lambda j, i: (i, j)),
        out_shape=jax.ShapeDtypeStruct((M, D), BF16),
        scratch_shapes=[pltpu.VMEM((4, W, tn), BF16)],
        compiler_params=_cp(("arbitrary", "arbitrary")), name="merge")(*ys, w_branch, z, z, z, z)


def _rwkv_pre_kernel(za_ref, prev_ref, fill_ref, mu_ref, w0_ref, a0_ref, kk_ref, ka_ref, w2_ref, a2_ref,
                     g2_ref, e_ref, r_o, lw_o, k_o, v_o, kn_o, b_o, g_o, *, n_p, seq_s):
    i = pl.program_id(0)
    is_s = i >= n_p
    za = za_ref[...]
    rows = _iota(za.shape, 0)
    prev = jnp.where(rows == 0, prev_ref[7:8, :], pltpu.roll(za, 1, axis=0))
    base = jnp.where(is_s, n_p * TM, 0)
    pmask = jnp.where(is_s, seq_s - 1, 0x3FFFFFFF)
    start = ((rows + (i * TM - base)) & pmask) == 0
    fill = jnp.where(is_s, fill_ref[...], 0.0)
    prev = jnp.where(start, fill, prev)
    zm = za + mu_ref[...] * (prev - za)
    W = BRANCH_W
    r = zm[:, 0:W]
    k = zm[:, W:2 * W]
    v = zm[:, 2 * W:3 * W]
    x = zm[:, 3 * W:4 * W]
    wl = _dot(jnp.tanh(x).astype(BF16), w2_ref[...].astype(BF16))
    y = -(w0_ref[...] + wl)
    softplus = jnp.maximum(y, 0.0) + jnp.log(1.0 + jnp.exp(-jnp.abs(y)))
    w = -softplus - 0.5
    logw = -jnp.exp(w)
    a = _sigmoid(a0_ref[...] + _dot(x.astype(BF16), a2_ref[...].astype(BF16)))
    g = _dot(_sigmoid(x).astype(BF16), g2_ref[...].astype(BF16))
    kk0 = k * kk_ref[...]
    ss = _dot_exact_r(kk0 * kk0, e_ref[...], n=3)
    kn = kk0 / jnp.maximum(jnp.sqrt(ss), 1e-12)
    k2 = k * (1.0 + (a - 1.0) * ka_ref[...])
    b = kn * a
    for h in range(A_HEADS):
        sl = slice(h * A_HD, (h + 1) * A_HD)
        r_o[h] = r[:, sl]
        lw_o[h] = logw[:, sl]
        k_o[h] = k2[:, sl]
        v_o[h] = v[:, sl]
        kn_o[h] = kn[:, sl]
        b_o[h] = b[:, sl]
        g_o[h] = g[:, sl]


def _rwkv_pre_call(z, fill_s, mu, w0, a0, k_k, k_a, w2p, a2p, g2p, e64, n_p, seq_s):
    M = z.shape[0]
    W = BRANCH_W
    cb = Z_A // 2048
    vec = lambda n: pl.BlockSpec((1, n), lambda i: (0, 0))
    mat = pl.BlockSpec((W, W), lambda i: (0, 0))
    out = jax.ShapeDtypeStruct((A_HEADS, M, A_HD), F32)
    ospec = pl.BlockSpec((A_HEADS, TM, A_HD), lambda i: (0, i, 0))
    return pl.pallas_call(
        functools.partial(_rwkv_pre_kernel, n_p=n_p, seq_s=seq_s), grid=(M // TM,),
        in_specs=[pl.BlockSpec((TM, 2048), lambda i: (i, cb)),
                  pl.BlockSpec((8, 2048), lambda i: (jnp.maximum(i * (TM // 8) - 1, 0), cb)),
                  pl.BlockSpec((TM, 2048), lambda i: (jnp.maximum(i - n_p, 0), 0)),
                  vec(2048), vec(W), vec(W), vec(W), vec(W), mat, mat, mat, mat],
        out_specs=[ospec] * 7, out_shape=[out] * 7,
        compiler_params=_cp(("arbitrary",)), name="rwkv_pre")(
            z, z, fill_s, mu, w0, a0, k_k, k_a, w2p, a2p, g2p, e64)


def _rwkv_chunk_kernel(r_ref, lw_ref, k_ref, v_ref, kn_ref, b_ref, g_ref, s0_ref, rk_ref, lng_ref, lnb_ref,
                       y_ref, sfin_ref, S_ref, *, C, nc):
    c = pl.program_id(1)

    @pl.when(c == 0)
    def _():
        S_ref[...] = s0_ref[...]

    H = range(A_HEADS)
    row = _iota((C, C), 0)
    col = _iota((C, C), 1)
    tri = (row >= col).astype(BF16)
    eye = (row == col).astype(F32)
    row2 = _iota((2 * C, C), 0)
    col2 = _iota((2 * C, C), 1)
    low2 = jnp.where(row2 < C, row2 - 1, row2 - C) >= col2
    eye_k = _iota((A_HD, A_HD), 0) == _iota((A_HD, A_HD), 1)
    r = [r_ref[h] for h in H]
    lw = [lw_ref[h] for h in H]
    k = [k_ref[h] for h in H]
    v = [v_ref[h] for h in H]
    kn = [kn_ref[h] for h in H]
    b = [b_ref[h] for h in H]
    S0 = [S_ref[h] for h in H]
    L = [_dot_exact_l(tri, lw[h]) for h in H]
    eL = [jnp.exp(L[h]) for h in H]
    eN = [jnp.exp(-L[h]) for h in H]
    eE = [jnp.exp(L[h][C - 1:C, :] - L[h]) for h in H]
    lhs = [jnp.concatenate([kn[h] * jnp.exp(L[h] - lw[h]), r[h] * eL[h]], axis=0) for h in H]
    Ab = [jnp.where(low2, _dot1(lhs[h], b[h] * eN[h], NT), 0.0) for h in H]
    Ak = [jnp.where(low2, _dot1(lhs[h], k[h] * eN[h], NT), 0.0) for h in H]
    N = [Ab[h][0:C] for h in H]
    X = [eye - N[h] for h in H]
    P = [_dot1(N[h], N[h]) for h in H]
    n = 2
    while n < C:
        X = [X[h] + _dot1(X[h], P[h]) for h in H]
        n *= 2
        if n < C:
            P = [_dot1(P[h], P[h]) for h in H]
    AV = [_dot1(Ak[h], v[h]) for h in H]
    KS = [_dot1(lhs[h], S0[h]) for h in H]
    U = [_dot1(X[h], KS[h][0:C] + AV[h][0:C]) for h in H]
    Y = [KS[h][C:] + AV[h][C:] - _dot1(Ab[h][C:], U[h]) for h in H]
    gcol = [jnp.sum(jnp.where(eye_k, jnp.broadcast_to(eL[h][C - 1:C, :], (A_HD, A_HD)), 0.0), axis=1, keepdims=True)
            for h in H]
    S1 = [S0[h] * gcol[h] + _dot1(jnp.concatenate([k[h] * eE[h], -(b[h] * eE[h])], axis=0),
                                  jnp.concatenate([v[h], U[h]], axis=0), TN) for h in H]
    out = []
    for h in H:
        yn = _ln(Y[h], RWKV_GN_EPS) * lng_ref[h] + lnb_ref[h]
        bonus = jnp.sum(r[h] * k[h] * rk_ref[h], axis=1, keepdims=True) * v[h]
        out.append((yn + bonus) * g_ref[h])
    y_ref[...] = jnp.concatenate(out, axis=1)
    for h in H:
        S_ref[h] = S1[h]

    @pl.when(c == nc - 1)
    def _():
        for h in H:
            sfin_ref[h] = S1[h]


def _rwkv_chunk_call(pre, s0t, r_k, ln_g, ln_b, row0, B, T, C):
    nc = T // C
    blk0 = row0 // C
    in_spec = pl.BlockSpec((A_HEADS, C, A_HD), lambda b, c: (0, blk0 + b * nc + c, 0))
    hvec = pl.BlockSpec((A_HEADS, 1, A_HD), lambda b, c: (0, 0, 0))
    st = pl.BlockSpec((None, A_HEADS, A_HD, A_HD), lambda b, c: (b, 0, 0, 0))
    return pl.pallas_call(
        functools.partial(_rwkv_chunk_kernel, C=C, nc=nc), grid=(B, nc),
        in_specs=[in_spec] * 7 + [st, hvec, hvec, hvec],
        out_specs=[pl.BlockSpec((C, BRANCH_W), lambda b, c: (b * nc + c, 0)), st],
        out_shape=[jax.ShapeDtypeStruct((B * T, BRANCH_W), F32),
                   jax.ShapeDtypeStruct((B, A_HEADS, A_HD, A_HD), F32)],
        scratch_shapes=[pltpu.VMEM((A_HEADS, A_HD, A_HD), F32)],
        compiler_params=_cp(("arbitrary", "arbitrary")), name="rwkv_chunk_c%d" % C)(
            *pre, s0t, r_k, ln_g, ln_b)


def _foxcum_kernel(fl_ref, bf_ref, lf_ref, cum_ref, carry_ref):
    @pl.when(pl.program_id(0) == 0)
    def _():
        carry_ref[...] = jnp.zeros_like(carry_ref)

    lane = _iota((TM, LANES), 1)
    lf = jnp.where(lane < NH, _log_sigmoid(fl_ref[...] + bf_ref[...]), 0.0)
    tri = (_iota((TM, TM), 0) >= _iota((TM, TM), 1)).astype(BF16)
    cum = _dot_exact_l(tri, lf) + carry_ref[...]
    lf_ref[...] = lf
    cum_ref[...] = cum
    carry_ref[...] = cum[TM - 1:TM, :]


def _foxcum_call(z, bf):
    M = z.shape[0]
    cb = (Z_B + 3 * BRANCH_W) // LANES
    spec = pl.BlockSpec((TM, LANES), lambda i: (i, 0))
    return pl.pallas_call(
        _foxcum_kernel, grid=(M // TM,),
        in_specs=[pl.BlockSpec((TM, LANES), lambda i: (i, cb)), pl.BlockSpec((1, LANES), lambda i: (0, 0))],
        out_specs=[spec, spec], out_shape=[jax.ShapeDtypeStruct((M, LANES), F32)] * 2,
        scratch_shapes=[pltpu.VMEM((1, LANES), F32)],
        compiler_params=_cp(("arbitrary",)), name="fox_cum")(z, bf)


def _attn_prep_kernel(fq_ref, fk_ref, fv_ref, mq_ref, mk_ref, mv_ref, cum_ref, cprev_ref, cqprev_ref, km_ref,
                      fqT_o, fkA_o, fvT_o, mqT_o, mk_o, mvT_o, sel_o, *, scale):
    i = pl.program_id(0)
    lane = _iota((TM, LANES), 1)
    cum = cum_ref[...]
    qscale = scale * LOG2E
    crel_k = (cum - jnp.where(i == 0, 0.0, cprev_ref[7:8, :])) * LOG2E
    crel_q = (cum - jnp.where(i < FLASH_TQ // TM, 0.0, cqprev_ref[7:8, :])) * LOG2E
    fq = fq_ref[...]
    fk = fk_ref[...]
    fv = fv_ref[...]
    mq = mq_ref[...]
    mk = mk_ref[...]
    mv = mv_ref[...]
    km = km_ref[...]
    blk = _iota((km.shape[0], TM), 0)
    blk_f = blk.astype(F32)
    for h in range(NH):
        sl = slice(h * HD, (h + 1) * HD)
        q3 = [p.astype(F32) for p in _split3(jnp.broadcast_to(crel_q[:, h:h + 1], (TM, LANES)))]
        k3 = [p.astype(F32) for p in _split3(jnp.broadcast_to(crel_k[:, h:h + 1], (TM, LANES)))]
        qb = jnp.where(lane == 0, q3[0], jnp.where(lane == 1, q3[1], jnp.where(lane == 2, q3[2],
                                                                             jnp.where(lane < 6, 1.0, 0.0))))
        kb = jnp.where(lane < 3, 1.0, jnp.where(lane == 3, -k3[0], jnp.where(lane == 4, -k3[1],
                                                                              jnp.where(lane == 5, -k3[2], 0.0))))
        fqT_o[h, 0:HD, :] = (fq[:, sl] * qscale).T.astype(BF16)
        fqT_o[h, HD:2 * HD, :] = qb.T.astype(BF16)
        fkA_o[h, :, 0:HD] = fk[:, sl].astype(BF16)
        fkA_o[h, :, HD:2 * HD] = kb.astype(BF16)
        fvT_o[h] = fv[:, sl].T.astype(BF16)
        mqT = mq[:, sl].T
        mqT_o[h] = (mqT * qscale).astype(BF16)
        mk_o[h] = mk[:, sl].astype(BF16)
        mvT_o[h] = mv[:, sl].T.astype(BF16)
        sc = _dot3(km[:, sl], mqT)
        sel_o[h] = _top3_select(sc, blk < i, blk_f, axis=0)


def _attn_prep_call(z, cum, kmean, Tp):
    W = BRANCH_W
    nbp = kmean.shape[0]
    cbB = Z_B // W
    cbC = Z_C // W
    rq = FLASH_TQ // TM
    zs = lambda cb: pl.BlockSpec((TM, W), lambda i: (i, cb))
    colT = lambda rows: pl.BlockSpec((NH, rows, TM), lambda i: (0, 0, i))
    rowm = lambda cols: pl.BlockSpec((NH, TM, cols), lambda i: (0, i, 0))
    sd = jax.ShapeDtypeStruct
    return pl.pallas_call(
        functools.partial(_attn_prep_kernel, scale=HD ** -0.5), grid=(Tp // TM,),
        in_specs=[zs(cbB), zs(cbB + 1), zs(cbB + 2), zs(cbC), zs(cbC + 1), zs(cbC + 2),
                  pl.BlockSpec((TM, LANES), lambda i: (i, 0)),
                  pl.BlockSpec((8, LANES), lambda i: (jnp.maximum(i * (TM // 8) - 1, 0), 0)),
                  pl.BlockSpec((8, LANES), lambda i: (jnp.maximum((i // rq) * (FLASH_TQ // 8) - 1, 0), 0)),
                  pl.BlockSpec((nbp, W), lambda i: (0, 0))],
        out_specs=[colT(2 * HD), rowm(2 * HD), colT(HD), colT(HD), rowm(HD), colT(HD), colT(nbp)],
        out_shape=[sd((NH, 2 * HD, Tp), BF16), sd((NH, Tp, 2 * HD), BF16), sd((NH, HD, Tp), BF16),
                   sd((NH, HD, Tp), BF16), sd((NH, Tp, HD), BF16), sd((NH, HD, Tp), BF16),
                   sd((NH, nbp, Tp), F32)],
        compiler_params=_cp(("arbitrary",)), name="attn_prep")(z, z, z, z, z, z, cum, cum, cum, kmean)


def _flash_kernel(ti_ref, tj_ref, bq_ref, bk_ref, qT_ref, kA_ref, vT_ref, *rest, moba):
    if moba:
        sel_ref, o_ref, m_ref, l_ref, acc_ref = rest
    else:
        o_ref, m_ref, l_ref, acc_ref = rest
    step = pl.program_id(0)
    i = ti_ref[step]
    j = tj_ref[step]
    T = kA_ref.shape[1]
    Tq = qT_ref.shape[2]
    ratio = Tq // T
    nqt = bq_ref.shape[0] // NH
    nkt = bk_ref.shape[0] // NH

    @pl.when(j == 0)
    def _():
        m_ref[...] = jnp.full(m_ref.shape, NEG, F32)
        l_ref[...] = jnp.zeros(l_ref.shape, F32)
        acc_ref[...] = jnp.zeros(acc_ref.shape, F32)

    H = range(NH)

    def step_body(on_diagonal):
        sT = [_dot(kA_ref[h], qT_ref[h]) for h in H]
        if not moba:
            off = [(bq_ref[h * nqt + i] - bk_ref[h * nkt + j]) * LOG2E for h in H]
            sT = [sT[h] + off[h] for h in H]
        if on_diagonal:
            causal = (_iota((T, Tq), 0) - _iota((T, Tq), 1)) <= (i * Tq - j * T)
            sT = [jnp.where(causal, sT[h], NEG) for h in H]
        if moba:
            own = (_iota((1, Tq), 1) // T + i * ratio) == j
            sT = [sT[h] + (jnp.where(own, 1.0, sel_ref[h, pl.ds(j, 1), :]) - 1.0) * (-NEG) for h in H]
        m_prev = [m_ref[h] for h in H]
        m_new = [jnp.maximum(m_prev[h], jnp.max(sT[h], axis=0, keepdims=True)) for h in H]
        p = [jnp.exp2(sT[h] - m_new[h]) for h in H]
        alpha = [jnp.exp2(m_prev[h] - m_new[h]) for h in H]
        pv = [_dot(vT_ref[h], p[h].astype(BF16)) for h in H]
        for h in H:
            l_ref[h] = alpha[h] * l_ref[h] + jnp.sum(p[h], axis=0, keepdims=True)
            acc_ref[h] = alpha[h] * acc_ref[h] + pv[h]
            m_ref[h] = m_new[h]

    crosses = (j + 1) * T - 1 > i * Tq
    pl.when(crosses)(functools.partial(step_body, True))
    pl.when(jnp.logical_not(crosses))(functools.partial(step_body, False))

    @pl.when(j == (i + 1) * ratio - 1)
    def _():
        for h in range(NH):
            o_ref[:, h * HD:(h + 1) * HD] = (acc_ref[h] / l_ref[h]).T


def _flash_call(qT, kA, vT, base_q, base_k, sel, Tp, name):
    T = MOBA_BLOCK
    Tq = FLASH_TQ
    ratio = Tq // T
    Kc = kA.shape[2]
    moba = sel is not None
    ti = np.concatenate([np.full((i + 1) * ratio, i, np.int32) for i in range(Tp // Tq)])
    tj = np.concatenate([np.arange((i + 1) * ratio, dtype=np.int32) for i in range(Tp // Tq)])
    in_specs = [pl.BlockSpec((NH, Kc, Tq), lambda s, ti, tj, bq, bk: (0, 0, ti[s])),
                pl.BlockSpec((NH, T, Kc), lambda s, ti, tj, bq, bk: (0, tj[s], 0)),
                pl.BlockSpec((NH, HD, T), lambda s, ti, tj, bq, bk: (0, 0, tj[s]))]
    args = [qT, kA, vT]
    if moba:
        in_specs.append(pl.BlockSpec((NH, sel.shape[1], Tq), lambda s, ti, tj, bq, bk: (0, 0, ti[s])))
        args.append(sel)
    return pl.pallas_call(
        functools.partial(_flash_kernel, moba=moba),
        grid_spec=pltpu.PrefetchScalarGridSpec(
            num_scalar_prefetch=4, grid=(len(ti),), in_specs=in_specs,
            out_specs=pl.BlockSpec((Tq, NH * HD), lambda s, ti, tj, bq, bk: (ti[s], 0)),
            scratch_shapes=[pltpu.VMEM((NH, 1, Tq), F32), pltpu.VMEM((NH, 1, Tq), F32),
                            pltpu.VMEM((NH, HD, Tq), F32)]),
        out_shape=jax.ShapeDtypeStruct((Tp, NH * HD), F32),
        compiler_params=_cp(("arbitrary",)), name=name)(
            jnp.asarray(ti), jnp.asarray(tj), base_q, base_k, *args)


def _top3_select(sc, valid, idx_f, axis=1):
    sc = jnp.where(valid, sc, -jnp.inf)
    sel = jnp.zeros(sc.shape, F32)
    for _ in range(MOBA_TOPK):
        mx = jnp.max(sc, axis=axis, keepdims=True)
        idx = jnp.min(jnp.where(sc == mx, idx_f, 1e9), axis=axis, keepdims=True)
        hit = idx_f == idx
        sel = jnp.where(hit & valid, 1.0, sel)
        sc = jnp.where(hit, -jnp.inf, sc)
    return sel


def _blockmean_kernel(k_ref, o_ref):
    o_ref[...] = jnp.sum(k_ref[...], axis=0, keepdims=True) * (1.0 / MOBA_BLOCK)


def _blockmean_call(z, Tp):
    nb = Tp // MOBA_BLOCK
    W = BRANCH_W
    cb = Z_C // W + 1
    return pl.pallas_call(
        _blockmean_kernel, grid=(nb,),
        in_specs=[pl.BlockSpec((MOBA_BLOCK, W), lambda i: (i, cb))],
        out_specs=pl.BlockSpec((None, 1, W), lambda i: (i, 0, 0)),
        out_shape=jax.ShapeDtypeStruct((nb, 1, W), F32),
        compiler_params=_cp(("arbitrary",)), name="moba_kmean")(z)


def _ret_kernel(q_ref, k_ref, v_ref, gd_ref, cos_ref, sin_ref, din_ref, qd_ref, kd_ref, cd_ref, s0_ref, lng_ref,
                y_ref, sfin_ref, S_ref, *, nc):
    c = pl.program_id(1)

    @pl.when(c == 0)
    def _():
        S_ref[...] = s0_ref[...]

    q = q_ref[...]
    k = k_ref[...]
    v = v_ref[...]
    gd = gd_ref[...]
    cos = cos_ref[...]
    sin = sin_ref[...]
    lng = lng_ref[...]
    for h in range(NH):
        sl = slice(h * HD, (h + 1) * HD)
        qh = q[:, sl]
        kh = k[:, sl]
        qr = qh * cos + pltpu.roll(qh, HD // 2, axis=1) * sin
        kr = (kh * cos + pltpu.roll(kh, HD // 2, axis=1) * sin) * (HD ** -0.5)
        vb = v[:, sl].astype(BF16)
        qb = qr.astype(BF16)
        att = _dot(qb, kr.astype(BF16), NT) * din_ref[h]
        S = S_ref[h]
        o = _dot(att.astype(BF16), vb) + _dot(qb, S.astype(BF16)) * qd_ref[h]
        S_ref[h] = S * cd_ref[h] + _dot((kr * kd_ref[h]).astype(BF16), vb, TN)
        y_ref[:, sl] = _ln(o, RET_GN_EPS) * lng[:, sl] * (gd[:, sl] * _sigmoid(gd[:, sl]))

    @pl.when(c == nc - 1)
    def _():
        sfin_ref[...] = S_ref[...]


def _ret_tables(C):
    lg = np.log(1.0 - 2.0 ** (-5.0 - np.arange(NH, dtype=np.float32))).astype(np.float32)
    i = np.arange(C, dtype=np.float32)
    diff = i[:, None] - i[None, :]
    din = np.where(diff[None] >= 0, np.exp(np.maximum(diff, 0.0)[None] * lg[:, None, None]), 0.0)
    qd = np.exp((i[None, :] + 1.0) * lg[:, None])
    kd = np.exp((C - 1.0 - i)[None, :] * lg[:, None])
    cd = np.exp(C * lg)
    bc = lambda t: np.ascontiguousarray(np.broadcast_to(t[:, :, None], (NH, C, HD))).astype(np.float32)
    cdb = np.ascontiguousarray(np.broadcast_to(cd[:, None, None], (NH, 1, HD))).astype(np.float32)
    return din.astype(np.float32), bc(qd), bc(kd), cdb


def _rope_tables(pos0, T):
    half = HD // 2
    inv = 1.0 / (ROPE_BASE ** (jnp.arange(half, dtype=F32) / half))
    ang = (pos0 + jnp.arange(T)).astype(F32)[:, None] * inv[None, :]
    cos = jnp.cos(ang)
    sin = jnp.sin(ang)
    return jnp.concatenate([cos, cos], axis=1), jnp.concatenate([-sin, sin], axis=1)


def _ret_call(z, s0, ln_g, l, row0, B, T, C, pos0):
    nc = T // C
    W = BRANCH_W
    cb = Z_D // W
    blk0 = row0 // C
    cos, sin = _rope_tables(pos0, T)
    din, qd, kd, cd = _ret_tables(C)
    zs = lambda off: pl.BlockSpec((C, W), lambda b, c: (blk0 + b * nc + c, cb + off))
    tab = pl.BlockSpec((C, HD), lambda b, c: (c, 0))
    full = lambda shape: pl.BlockSpec(shape, lambda b, c: (0,) * len(shape))
    st = pl.BlockSpec((None, NH, HD, HD), lambda b, c: (b, 0, 0, 0))
    return pl.pallas_call(
        functools.partial(_ret_kernel, nc=nc), grid=(B, nc),
        in_specs=[zs(0), zs(1), zs(2), zs(3), tab, tab, full((NH, C, C)), full((NH, C, HD)), full((NH, C, HD)),
                  full((NH, 1, HD)), st, pl.BlockSpec((None, 1, W), lambda b, c: (l, 0, 0))],
        out_specs=[pl.BlockSpec((C, W), lambda b, c: (b * nc + c, 0)), st],
        out_shape=[jax.ShapeDtypeStruct((B * T, W), F32), jax.ShapeDtypeStruct((B, NH, HD, HD), F32)],
        scratch_shapes=[pltpu.VMEM((NH, HD, HD), F32)],
        compiler_params=_cp(("arbitrary", "arbitrary")), name="ret_c%d" % C)(
            z, z, z, z, cos, sin, din, qd, kd, cd, s0, ln_g)


def _lfsuf_kernel(pt_ref, lf_hbm, ts_ref, tot_ref, o_ref, buf, sem, *, npg, base):
    b = pl.program_id(0)

    def page_copy(p):
        return pltpu.make_async_copy(lf_hbm.at[pl.ds(NH * (base + pt_ref[b * npg + p]), NH), :],
                                     buf.at[pl.ds(NH * p, NH), :], sem.at[p])

    for p in range(npg):
        page_copy(p).start()
    for p in range(npg):
        page_copy(p).wait()
    loc = None
    tot = None
    for h in range(NH):
        x = buf[pl.ds(h, npg, stride=NH), :]
        a = _dot_exact_r(x, ts_ref[h])
        t = _dot_exact_r(x, tot_ref[h])
        loc = a if loc is None else loc + a
        tot = t if tot is None else tot + t
    later = (_iota((npg, npg), 1) > _iota((npg, npg), 0)).astype(BF16)
    o_ref[...] = loc + _dot_exact_l(later, tot)


def _lfsuf_call(pt_flat, lf_rows, ts, tot, B, npg, base):
    PW = PAGE_SIZE * NH
    sel = pl.BlockSpec((NH, PAGE_SIZE, PW), lambda b, pt: (0, 0, 0))
    return pl.pallas_call(
        functools.partial(_lfsuf_kernel, npg=npg, base=base),
        grid_spec=pltpu.PrefetchScalarGridSpec(
            num_scalar_prefetch=1, grid=(B,),
            in_specs=[pl.BlockSpec(memory_space=pl.ANY), sel, sel],
            out_specs=pl.BlockSpec((None, npg, PW), lambda b, pt: (b, 0, 0)),
            scratch_shapes=[pltpu.VMEM((NH * npg, PAGE_SIZE), F32), pltpu.SemaphoreType.DMA((npg,))]),
        out_shape=jax.ShapeDtypeStruct((B, npg, PW), F32),
        compiler_params=_cp(("arbitrary",)), name="fox_logf_suffix")(pt_flat, lf_rows, ts, tot)


def _suffix_matrices():
    r = np.arange(PAGE_SIZE)
    dst_r = np.repeat(r, NH)
    dst_h = np.tile(np.arange(NH), PAGE_SIZE)
    same = np.arange(NH)[:, None, None] == dst_h[None, None, :]
    ts = same & (r[None, :, None] > dst_r[None, None, :])
    return jnp.asarray(ts, BF16), jnp.asarray(np.broadcast_to(same, ts.shape), BF16)


def _sattn_kernel(pt_ref, fq_ref, fkn_ref, fvn_ref, lfn_ref, mq_ref, mkn_ref, mvn_ref, R_ref, *rest, nb, S, scale):
    P = SAMPLE_PAGES
    fk, fv, mk, mv = (rest[t * P:(t + 1) * P] for t in range(4))
    yb_ref, yc_ref, fqb, mqb, mqf, cnb, fm, fl, facc, sc_all, m_all, l_all, o_all = rest[4 * P:]
    n = pl.program_id(1)
    R4 = NH * S
    PW = PAGE_SIZE * NH
    own_head = (_iota((R4, PW), 1) % NH) == (_iota((R4, PW), 0) // S)
    lane = _iota((R4, LANES), 1)

    def stack_heads(x):
        return jnp.concatenate([x[:, h * HD:(h + 1) * HD] for h in range(NH)], axis=0)

    def head_sums(kpage):
        return jnp.sum(kpage.reshape(PW // 8, 8, HD), axis=0)

    @pl.when(n == 0)
    def _():
        fqb[...] = stack_heads(fq_ref[...]).astype(BF16)
        mq = stack_heads(mq_ref[...])
        mqb[...] = mq.astype(BF16)
        mqf[...] = mq
        tri = (_iota((S, S), 0) >= _iota((S, S), 1)).astype(BF16)
        cn = _dot_exact_l(tri, lfn_ref[...])
        cnb[...] = jnp.concatenate([jnp.broadcast_to(cn[:, h:h + 1], (S, LANES)) for h in range(NH)], axis=0)
        fm[...] = jnp.full(fm.shape, NEG, F32)
        fl[...] = jnp.zeros(fl.shape, F32)
        facc[...] = jnp.zeros(facc.shape, F32)
        sc_all[...] = jnp.full(sc_all.shape, -jnp.inf, F32)
        m_all[...] = jnp.full(m_all.shape, NEG, F32)
        l_all[...] = jnp.zeros(l_all.shape, F32)

    def fox_update(s_list, v_list):
        m_prev = fm[...]
        m_new = m_prev
        for s in s_list:
            m_new = jnp.maximum(m_new, jnp.max(s, axis=1, keepdims=True))
        alpha = jnp.exp(m_prev - m_new)
        p_list = [jnp.exp(s - m_new) for s in s_list]
        l_new = alpha * fl[...]
        acc = alpha * facc[...]
        for p, v_bf in zip(p_list, v_list):
            l_new = l_new + jnp.sum(p, axis=1, keepdims=True)
            acc = acc + _dot(p.astype(BF16), v_bf)
        fl[...] = l_new
        facc[...] = acc
        fm[...] = m_new

    cn_b = cnb[...]
    cn_w = jnp.concatenate([cn_b] * NH, axis=1)
    fq_b = fqb[...]
    s_fox = [_dot(fq_b, fk[g][...].astype(BF16), NT) * scale for g in range(P)]
    s_fox = [jnp.where(own_head, s_fox[g] + cn_w + R_ref[pl.ds(P * n + g, 1), :], NEG) for g in range(P)]
    fox_update(s_fox, [fv[g][...].astype(BF16) for g in range(P)])

    mq_b = mqb[...]
    mq_f = mqf[...]
    kpg = [mk[g][...] for g in range(P)]
    s_mo = [jnp.where(own_head, _dot(mq_b, kpg[g].astype(BF16), NT) * scale, NEG) for g in range(P)]
    for blk in range(P // 2):
        g0, g1 = 2 * blk, 2 * blk + 1
        ks = head_sums(kpg[g0]) + head_sums(kpg[g1])
        kmean = (ks[0:NH] + ks[NH:2 * NH]) * (1.0 / MOBA_BLOCK)
        kmean = jnp.concatenate([jnp.broadcast_to(kmean[h:h + 1], (S, HD)) for h in range(NH)], axis=0)
        sc_col = jnp.sum(mq_f * kmean, axis=1, keepdims=True)
        m_b = jnp.maximum(jnp.max(s_mo[g0], axis=1, keepdims=True), jnp.max(s_mo[g1], axis=1, keepdims=True))
        p0 = jnp.exp(s_mo[g0] - m_b)
        p1 = jnp.exp(s_mo[g1] - m_b)
        here = lane == (P // 2) * n + blk
        sc_all[...] = jnp.where(here, sc_col, sc_all[...])
        m_all[...] = jnp.where(here, m_b, m_all[...])
        l_all[...] = jnp.where(here, jnp.sum(p0, axis=1, keepdims=True) + jnp.sum(p1, axis=1, keepdims=True),
                               l_all[...])
        o_all[(P // 2) * n + blk] = (_dot(p0.astype(BF16), mv[g0][...].astype(BF16))
                                     + _dot(p1.astype(BF16), mv[g1][...].astype(BF16)))

    @pl.when(n == pl.num_programs(1) - 1)
    def _():
        ri = _iota((R4, R4), 0)
        ci = _iota((R4, R4), 1)
        causal = ((ri // S) == (ci // S)) & ((ci % S) <= (ri % S))
        cn_col = cn_b[:, 0:1]
        cn_row = jnp.sum(jnp.where(ri == ci, jnp.broadcast_to(cn_col, (R4, R4)), 0.0), axis=0, keepdims=True)
        s = _dot(fqb[...], stack_heads(fkn_ref[...]).astype(BF16), NT) * scale
        s = s + (cn_col - cn_row)
        fox_update([jnp.where(causal, s, NEG)], [stack_heads(fvn_ref[...]).astype(BF16)])
        out = facc[...] / fl[...]
        for h in range(NH):
            yb_ref[:, h * HD:(h + 1) * HD] = out[h * S:(h + 1) * S, :]

        s = _dot(mqb[...], stack_heads(mkn_ref[...]).astype(BF16), NT) * scale
        s = jnp.where(causal, s, NEG)
        m_o = jnp.max(s, axis=1, keepdims=True)
        p = jnp.exp(s - m_o)
        l_o = jnp.sum(p, axis=1, keepdims=True)
        o_o = _dot(p.astype(BF16), stack_heads(mvn_ref[...]).astype(BF16))
        sel = _top3_select(sc_all[...], lane < nb, lane.astype(F32)) > 0.5
        m_past = m_all[...]
        m_tot = jnp.maximum(jnp.max(jnp.where(sel, m_past, NEG), axis=1, keepdims=True), m_o)
        w = jnp.where(sel, jnp.exp(m_past - m_tot), 0.0)
        w_o = jnp.exp(m_o - m_tot)
        den = jnp.sum(w * l_all[...], axis=1, keepdims=True) + w_o * l_o
        num = w_o * o_o
        for blk in range(nb):
            num = num + w[:, blk:blk + 1] * o_all[blk]
        out = num / den
        for h in range(NH):
            yc_ref[:, h * HD:(h + 1) * HD] = out[h * S:(h + 1) * S, :]


def _sattn_call(pt_flat, z, lf, R, caches, l, Mp, B, S, npg, n_pool):
    fk, fv, mk, mv = caches
    W = BRANCH_W
    PW = PAGE_SIZE * NH
    nb = npg // 2
    R4 = NH * S
    rb = Mp // S
    base = l * n_pool
    zrow = lambda cb: pl.BlockSpec((S, W), lambda b, n, pt: (rb + b, cb))
    P = SAMPLE_PAGES
    page = lambda g: pl.BlockSpec((PW, HD), lambda b, n, pt: (base + pt[b * npg + P * n + g], 0))
    pages = [page(g) for g in range(P)]
    cbB = Z_B // W
    cbC = Z_C // W
    in_specs = [zrow(cbB), zrow(cbB + 1), zrow(cbB + 2),
                pl.BlockSpec((S, LANES), lambda b, n, pt: (rb + b, 0)),
                zrow(cbC), zrow(cbC + 1), zrow(cbC + 2),
                pl.BlockSpec((None, npg, PW), lambda b, n, pt: (b, 0, 0))] + pages * 4
    out_spec = pl.BlockSpec((S, W), lambda b, n, pt: (b, 0))
    scratch = [pltpu.VMEM((R4, HD), BF16), pltpu.VMEM((R4, HD), BF16), pltpu.VMEM((R4, HD), F32),
               pltpu.VMEM((R4, LANES), F32),
               pltpu.VMEM((R4, 1), F32), pltpu.VMEM((R4, 1), F32), pltpu.VMEM((R4, HD), F32),
               pltpu.VMEM((R4, LANES), F32), pltpu.VMEM((R4, LANES), F32), pltpu.VMEM((R4, LANES), F32),
               pltpu.VMEM((nb, R4, HD), F32)]
    return pl.pallas_call(
        functools.partial(_sattn_kernel, nb=nb, S=S, scale=HD ** -0.5),
        grid_spec=pltpu.PrefetchScalarGridSpec(
            num_scalar_prefetch=1, grid=(B, npg // P), in_specs=in_specs,
            out_specs=[out_spec, out_spec], scratch_shapes=scratch),
        out_shape=[jax.ShapeDtypeStruct((B * S, W), F32)] * 2,
        compiler_params=_cp(("arbitrary", "arbitrary")), name="sample_attn")(
            pt_flat, z, z, z, lf, z, z, z, R, *([fk] * P + [fv] * P + [mk] * P + [mv] * P))


def _pick_tile(m, cands):
    for t in cands:
        if m % t == 0:
            return t
    raise ValueError("no row tile divides %d" % m)


def _pad_cols(x, n):
    return jnp.pad(x, ((0, 0), (0, n - x.shape[1])))


def kernel(x_prompt, x_sample, c_prompt, c_sample, page_table, cache_fox_k, cache_fox_v, cache_fox_logf, cache_moba_k, cache_moba_v, state_rwkv, state_rwkv_shift, state_ret, w_ada, b_ada, w_in, rwkv_mu, rwkv_w0, rwkv_w2, rwkv_a0, rwkv_a2, rwkv_g2, rwkv_k_k, rwkv_k_a, rwkv_r_k, rwkv_ln_g, rwkv_ln_b, fox_bf, ret_ln_g, w_branch, w_o, ln1_g, ln1_b, ln2_g, ln2_b, w_ffn_gate, w_ffn_in, w_ffn_out):
    Bp, Tp, D = x_prompt.shape
    Bs, S, _ = x_sample.shape
    depth = w_in.shape[0]
    n_pool = cache_fox_k.shape[1]
    npg = page_table.shape[1]
    past = npg * PAGE_SIZE
    Mp, Ms = Bp * Tp, Bs * S
    M = Mp + Ms
    assert Bp == 1 and D == D_MODEL and depth == DEPTH
    assert Mp % TM == 0 and Ms % TM == 0 and Tp % MOBA_BLOCK == 0 and Tp // MOBA_BLOCK <= LANES
    assert Tp % RWKV_CHUNK == 0 and Tp % RET_CHUNK == 0 and S == 8 and past % MOBA_BLOCK == 0
    assert TM == MOBA_BLOCK and 2 * NH == 8 and MOBA_BLOCK == 2 * PAGE_SIZE
    assert Tp % FLASH_TQ == 0 and FLASH_TQ % MOBA_BLOCK == 0
    assert npg % SAMPLE_PAGES == 0 and SAMPLE_PAGES % 2 == 0 and npg // 2 <= LANES
    n_p = Mp // TM
    W = BRANCH_W
    tm_big = _pick_tile(M, (768, 512, 256))
    tm_small = _pick_tile(M, (384, 256))

    x = jnp.concatenate([x_prompt.reshape(Mp, D), x_sample.reshape(Ms, D)], axis=0)
    c_rows = 8 * (-(-(Bp + Bs) // 8))
    c_all = jnp.pad(jnp.concatenate([c_prompt, c_sample], axis=0), ((0, c_rows - Bp - Bs), (0, 0)))
    pt_flat = page_table.reshape(-1).astype(jnp.int32)
    cache2 = lambda t: t.reshape(depth * n_pool * PAGE_SIZE * NH, HD)
    caches = (cache2(cache_fox_k), cache2(cache_fox_v), cache2(cache_moba_k), cache2(cache_moba_v))
    lf_rows = jnp.swapaxes(cache_fox_logf, 2, 3).reshape(depth * n_pool * NH, PAGE_SIZE)
    ts_m, tot_m = _suffix_matrices()
    e64 = jnp.asarray(np.kron(np.eye(A_HEADS), np.ones((A_HD, A_HD))), BF16)
    b_ada3 = b_ada.reshape(depth, 1, -1)
    vec3 = lambda t: t.reshape(depth, 1, -1)
    ln1_g3, ln1_b3, ln2_g3, ln2_b3, ret_g3 = map(vec3, (ln1_g, ln1_b, ln2_g, ln2_b, ret_ln_g))

    per_layer = []
    for l in range(depth):
        ada = _ada_call(c_all, w_ada, b_ada3, l)
        ada_p = ada[0:8]
        mod_s = jnp.repeat(ada[Bp:Bp + Bs], S, axis=0)
        h = _lnmod_call(x, ada_p, mod_s, n_p, 1, 0)
        wl = w_in[l]
        o_b, o_c, o_d = A_COLS, A_COLS + B_COLS, A_COLS + B_COLS + C_COLS
        o_g = o_d + D_COLS
        w_in_p = jnp.concatenate(
            [wl[:, o_g:], _pad_cols(wl[:, :o_b], 2048), wl[:, o_d:o_g], wl[:, o_c:o_d],
             _pad_cols(wl[:, o_b:o_c], 1664)], axis=1)
        z = _mm_call(h, w_in_p, 0, tm_big, 1408, F32, "w_in")

        fill_s = jnp.repeat(_pad_cols(state_rwkv_shift[l], 2048), S, axis=0)
        pad_rows = lambda w2, r0: jnp.zeros((W, W), F32).at[r0:r0 + w2.shape[0]].set(w2)
        pre = _rwkv_pre_call(
            z, fill_s, _pad_cols(rwkv_mu[l][None], 2048), rwkv_w0[l][None], rwkv_a0[l][None],
            rwkv_k_k[l][None], rwkv_k_a[l][None], pad_rows(rwkv_w2[l], 0), pad_rows(rwkv_a2[l], A_LORA_W),
            pad_rows(rwkv_g2[l], A_LORA_W + A_LORA_A), e64, n_p, S)
        hv = lambda t: t.reshape(A_HEADS, 1, A_HD)
        rk, lg, lb = hv(rwkv_r_k[l]), hv(rwkv_ln_g[l]), hv(rwkv_ln_b[l])
        ya_p, sT_p = _rwkv_chunk_call(pre, jnp.zeros((Bp, A_HEADS, A_HD, A_HD), F32), rk, lg, lb,
                                      0, Bp, Tp, RWKV_CHUNK)
        ya_s, sT_s = _rwkv_chunk_call(pre, jnp.swapaxes(state_rwkv[l], -1, -2), rk, lg, lb, Mp, Bs, S, S)
        ya = jnp.concatenate([ya_p, ya_s], axis=0)

        lf, cum = _foxcum_call(z, _pad_cols(fox_bf[l][None], LANES))
        nbk = Tp // MOBA_BLOCK
        kmean = _blockmean_call(z, Tp).reshape(nbk, W)
        kmean = jnp.pad(kmean, ((0, 8 * (-(-nbk // 8)) - nbk), (0, 0)))
        fqT, fkA, fvT, mqT, mkb, mvT, sel = _attn_prep_call(z, cum, kmean, Tp)
        tile_base = lambda t: jnp.transpose(jnp.concatenate(
            [jnp.zeros((1, NH), F32), cum[t - 1:Mp - 1:t, :NH]], axis=0)).reshape(-1)
        base_q, base_k = tile_base(FLASH_TQ), tile_base(MOBA_BLOCK)
        yb_p = _flash_call(fqT, fkA, fvT, base_q, base_k, None, Tp, "fox_prompt")
        yc_p = _flash_call(mqT, mkb, mvT, base_q, base_k, sel, Tp, "moba_prompt")
        R = _lfsuf_call(pt_flat, lf_rows, ts_m, tot_m, Bs, npg, l * n_pool)
        yb_s, yc_s = _sattn_call(pt_flat, z, lf, R, caches, l, Mp, Bs, S, npg, n_pool)
        yb = jnp.concatenate([yb_p, yb_s], axis=0)
        yc = jnp.concatenate([yc_p, yc_s], axis=0)

        yd_p, rS_p = _ret_call(z, jnp.zeros((Bp, NH, HD, HD), F32), ret_g3, l, 0, Bp, Tp, RET_CHUNK, 0)
        yd_s, rS_s = _ret_call(z, state_ret[l], ret_g3, l, Mp, Bs, S, S, past)
        yd = jnp.concatenate([yd_p, yd_s], axis=0)

        mixin = _merge_call((ya, yb, yc, yd), w_branch, z, l, tm_big, 512)
        mix = _mm_call(mixin, w_o, l, tm_big, 512, F32, "w_o")
        x1, h2 = _resid_call(x, mix, ada_p, mod_s, ln1_g3, ln1_b3, l, n_p, 2, 4, 3)
        act = _ffn_call(h2, w_ffn_gate, w_ffn_in, l, tm_big, 512)
        f = _mm_call(act, w_ffn_out, l, tm_small, 512, F32, "ffn_out")
        x = _resid_call(x1, f, ada_p, mod_s, ln2_g3, ln2_b3, l, n_p, 5)

        zb = Z_B
        zc = Z_C
        hp = lambda t: t.reshape(Bp, Tp, NH, HD)
        hs = lambda t: t.reshape(Bs, S, NH, HD)
        per_layer.append(dict(
            fox_k_p=hp(z[:Mp, zb + W:zb + 2 * W]), fox_v_p=hp(z[:Mp, zb + 2 * W:zb + 3 * W]),
            fox_lf_p=lf[:Mp, :NH].reshape(Bp, Tp, NH),
            moba_k_p=hp(z[:Mp, zc + W:zc + 2 * W]), moba_v_p=hp(z[:Mp, zc + 2 * W:zc + 3 * W]),
            rwkv_S_p=jnp.swapaxes(sT_p, -1, -2), shift_p=z[Mp - 1:Mp, Z_A:Z_A + A_COLS], ret_S_p=rS_p,
            fox_k_s=hs(z[Mp:, zb + W:zb + 2 * W]), fox_v_s=hs(z[Mp:, zb + 2 * W:zb + 3 * W]),
            fox_lf_s=lf[Mp:, :NH].reshape(Bs, S, NH),
            moba_k_s=hs(z[Mp:, zc + W:zc + 2 * W]), moba_v_s=hs(z[Mp:, zc + 2 * W:zc + 3 * W]),
            rwkv_S_s=jnp.swapaxes(sT_s, -1, -2),
            shift_s=z[Mp:, Z_A:Z_A + A_COLS].reshape(Bs, S, A_COLS)[:, S - 1], ret_S_s=rS_s))

    st = lambda name: jnp.stack([p[name] for p in per_layer])
    return (x[:Mp].reshape(Bp, Tp, D), x[Mp:].reshape(Bs, S, D),
            st("fox_k_p"), st("fox_v_p"), st("fox_lf_p"), st("moba_k_p"), st("moba_v_p"),
            st("rwkv_S_p"), st("shift_p"), st("ret_S_p"),
            st("fox_k_s"), st("fox_v_s"), st("fox_lf_s"), st("moba_k_s"), st("moba_v_s"),
            st("rwkv_S_s"), st("shift_s"), st("ret_S_s"))
```

```python
import functools

import numpy as np
import jax
import jax.numpy as jnp
from jax import lax
from jax.experimental import pallas as pl
from jax.experimental.pallas import tpu as pltpu

F32 = jnp.float32
BF16 = jnp.bfloat16

D_MODEL = 2048
DEPTH = 2
PAGE_SIZE = 128
BRANCH_W = 512
A_HD = 64
A_HEADS = 8
A_LORA_W = 96
A_LORA_A = 96
A_LORA_G = 256
A_COLS = 3 * BRANCH_W + A_LORA_W + A_LORA_A + A_LORA_G
B_COLS = 3 * BRANCH_W + 4
C_COLS = 3 * BRANCH_W
D_COLS = 4 * BRANCH_W
HD = 128
NH = 4
RWKV_GN_EPS = 64e-5
RET_GN_EPS = 1e-6
LN_EPS = 1e-5
MOBA_BLOCK = 256
MOBA_TOPK = 3
RET_CHUNK = 128
RWKV_CHUNK = 64
ROPE_BASE = 10000.0
FFN_HIDDEN = 5632
ALPHA = (2 * DEPTH) ** 0.25

Z_G = 0
Z_A = 4 * D_MODEL
Z_D = Z_A + 2048
Z_C = Z_D + 2048
Z_B = Z_C + 1536
Z_COLS = Z_B + 1664

TM = 256
FLASH_TQ = 512
SAMPLE_PAGES = 8
LANES = 128
NEG = -1e30
LOG2E = 1.4426950408889634
VMEM_LIMIT = 56 << 20

NN = (((1,), (0,)), ((), ()))
NT = (((1,), (1,)), ((), ()))
TN = (((0,), (0,)), ((), ()))


def _cp(sem):
    return pltpu.CompilerParams(dimension_semantics=sem, vmem_limit_bytes=VMEM_LIMIT)


def _dot(a, b, dims=NN):
    return lax.dot_general(a, b, dims, preferred_element_type=F32)


def _split2(x):
    hi = x.astype(BF16)
    return hi, (x - hi.astype(F32)).astype(BF16)


def _split3(x):
    hi = x.astype(BF16)
    r1 = x - hi.astype(F32)
    mid = r1.astype(BF16)
    return hi, mid, (r1 - mid.astype(F32)).astype(BF16)


def _dot3(a, b, dims=NN):
    ah, al = _split2(a)
    bh, bl = _split2(b)
    return _dot(ah, bh, dims) + (_dot(ah, bl, dims) + _dot(al, bh, dims))


def _dot1(a, b, dims=NN):
    return _dot(a.astype(BF16), b.astype(BF16), dims)


def _dot_exact_l(m_bf16, x, n=3):
    parts = _split3(x) if n == 3 else _split2(x)
    acc = _dot(m_bf16, parts[0])
    for p in parts[1:]:
        acc = acc + _dot(m_bf16, p)
    return acc


def _dot_exact_r(x, m_bf16, n=3):
    parts = _split3(x) if n == 3 else _split2(x)
    acc = _dot(parts[0], m_bf16)
    for p in parts[1:]:
        acc = acc + _dot(p, m_bf16)
    return acc


def _ln(x, eps):
    mu = jnp.mean(x, axis=-1, keepdims=True)
    xc = x - mu
    var = jnp.mean(xc * xc, axis=-1, keepdims=True)
    return xc * lax.rsqrt(var + eps)


def _sigmoid(x):
    return 1.0 / (1.0 + jnp.exp(-x))


def _log_sigmoid(x):
    return jnp.minimum(x, 0.0) - jnp.log(1.0 + jnp.exp(-jnp.abs(x)))


def _iota(shape, dim):
    return lax.broadcasted_iota(jnp.int32, shape, dim)


def _ada_kernel(c_ref, w_ref, b_ref, o_ref):
    c = c_ref[...]
    s = (c * _sigmoid(c)).astype(BF16)
    o_ref[...] = _dot(s, w_ref[...].astype(BF16)) + b_ref[...]


def _ada_call(c_all, w_ada, b_ada, l):
    R, D = c_all.shape
    N = w_ada.shape[2]
    tn = 2048
    return pl.pallas_call(
        _ada_kernel, grid=(N // tn,),
        in_specs=[pl.BlockSpec((R, D), lambda j: (0, 0)),
                  pl.BlockSpec((None, D, tn), lambda j: (l, 0, j)),
                  pl.BlockSpec((None, 1, tn), lambda j: (l, 0, j))],
        out_specs=pl.BlockSpec((R, tn), lambda j: (0, j)),
        out_shape=jax.ShapeDtypeStruct((R, N), F32),
        compiler_params=_cp(("arbitrary",)), name="ada")(c_all, w_ada, b_ada)


def _lnmod_kernel(x_ref, scp_ref, shp_ref, scs_ref, shs_ref, o_ref, *, n_p):
    is_s = pl.program_id(0) >= n_p
    sc = jnp.where(is_s, scs_ref[...], scp_ref[0:1, :])
    sh = jnp.where(is_s, shs_ref[...], shp_ref[0:1, :])
    o_ref[...] = (_ln(x_ref[...], LN_EPS) * (1.0 + sc) + sh).astype(BF16)


def _mod_specs(n_p, cols):
    specs = []
    for cb in cols:
        specs.append(pl.BlockSpec((8, D_MODEL), lambda i, cb=cb: (0, cb)))
    for cb in cols:
        specs.append(pl.BlockSpec((TM, D_MODEL), lambda i, cb=cb: (jnp.maximum(i - n_p, 0), cb)))
    return specs


def _lnmod_call(x, ada_p, mod_s, n_p, sc_col, sh_col):
    M, D = x.shape
    return pl.pallas_call(
        functools.partial(_lnmod_kernel, n_p=n_p), grid=(M // TM,),
        in_specs=[pl.BlockSpec((TM, D), lambda i: (i, 0))] + _mod_specs(n_p, (sc_col, sh_col)),
        out_specs=pl.BlockSpec((TM, D), lambda i: (i, 0)),
        out_shape=jax.ShapeDtypeStruct((M, D), BF16),
        compiler_params=_cp(("arbitrary",)), name="lnmod")(x, ada_p, ada_p, mod_s, mod_s)


def _mm_kernel(a_ref, w_ref, o_ref, wb_ref):
    @pl.when(pl.program_id(1) == 0)
    def _():
        wb_ref[...] = w_ref[...].astype(BF16)

    o_ref[...] = _dot(a_ref[...].astype(BF16), wb_ref[...]).astype(o_ref.dtype)


def _mmb_kernel(a_ref, w_ref, o_ref):
    o_ref[...] = _dot(a_ref[...], w_ref[...]).astype(o_ref.dtype)


def _mmb_call(a, w, w_index, tm, tn, out_dtype, name):
    M, K = a.shape
    N = w.shape[-1]
    return pl.pallas_call(
        _mmb_kernel, grid=(N // tn, M // tm),
        in_specs=[pl.BlockSpec((tm, K), lambda j, i: (i, 0)),
                  pl.BlockSpec((None, K, tn), lambda j, i: (w_index, 0, j))],
        out_specs=pl.BlockSpec((tm, tn), lambda j, i: (i, j)),
        out_shape=jax.ShapeDtypeStruct((M, N), out_dtype),
        compiler_params=_cp(("arbitrary", "arbitrary")), name=name)(a, w)


_O_B, _O_C, _O_D = A_COLS, A_COLS + B_COLS, A_COLS + B_COLS + C_COLS
_O_G = _O_D + D_COLS
_W_IN_SEGMENTS = ((_O_G, _O_G + 4 * D_MODEL, Z_G), (0, _O_B, Z_A), (_O_D, _O_G, Z_D), (_O_C, _O_D, Z_C),
                  (_O_B, _O_C, Z_B))
_W_IN_PADS = ((Z_A + A_COLS, Z_D), (Z_B + B_COLS, Z_COLS))
W_IN_ROWS = 128


def _win_relayout_kernel(w_ref, o_ref):
    w = w_ref[...]
    for s0, s1, d0 in _W_IN_SEGMENTS:
        o_ref[:, d0:d0 + (s1 - s0)] = w[:, s0:s1].astype(BF16)
    for p0, p1 in _W_IN_PADS:
        o_ref[:, p0:p1] = jnp.zeros((w.shape[0], p1 - p0), BF16)


def _win_relayout_call(w_in):
    depth, K, N = w_in.shape
    return pl.pallas_call(
        _win_relayout_kernel, grid=(depth, K // W_IN_ROWS),
        in_specs=[pl.BlockSpec((None, W_IN_ROWS, N), lambda l, r: (l, r, 0))],
        out_specs=pl.BlockSpec((None, W_IN_ROWS, Z_COLS), lambda l, r: (l, r, 0)),
        out_shape=jax.ShapeDtypeStruct((depth, K, Z_COLS), BF16),
        compiler_params=_cp(("arbitrary", "arbitrary")), name="w_in_relayout")(w_in)


def _mm_call(a, w, w_index, tm, tn, out_dtype, name):
    M, K = a.shape
    N = w.shape[-1]
    if w.ndim == 3:
        w_spec = pl.BlockSpec((None, K, tn), lambda j, i: (w_index, 0, j))
    else:
        w_spec = pl.BlockSpec((K, tn), lambda j, i: (0, j))
    return pl.pallas_call(
        _mm_kernel, grid=(N // tn, M // tm),
        in_specs=[pl.BlockSpec((tm, K), lambda j, i: (i, 0)), w_spec],
        out_specs=pl.BlockSpec((tm, tn), lambda j, i: (i, j)),
        out_shape=jax.ShapeDtypeStruct((M, N), out_dtype),
        scratch_shapes=[pltpu.VMEM((K, tn), BF16)],
        compiler_params=_cp(("arbitrary", "arbitrary")), name=name)(a, w)


def _resid_kernel(*refs, n_p, with_h):
    if with_h:
        (x_ref, mix_ref, gp_ref, scp_ref, shp_ref, gs_ref, scs_ref, shs_ref, lg_ref, lb_ref,
         x1_ref, h_ref) = refs
    else:
        x_ref, mix_ref, gp_ref, gs_ref, lg_ref, lb_ref, x1_ref = refs
    is_s = pl.program_id(0) >= n_p
    g = jnp.where(is_s, gs_ref[...], gp_ref[0:1, :])
    y = ALPHA * x_ref[...] + (1.0 + g) * mix_ref[...]
    x1 = _ln(y, LN_EPS) * lg_ref[...] + lb_ref[...]
    x1_ref[...] = x1
    if with_h:
        sc = jnp.where(is_s, scs_ref[...], scp_ref[0:1, :])
        sh = jnp.where(is_s, shs_ref[...], shp_ref[0:1, :])
        h_ref[...] = (_ln(x1, LN_EPS) * (1.0 + sc) + sh).astype(BF16)


def _resid_call(x, mix, ada_p, mod_s, ln_g, ln_b, l, n_p, g_col, sc_col=None, sh_col=None):
    M, D = x.shape
    with_h = sc_col is not None
    cols = (g_col, sc_col, sh_col) if with_h else (g_col,)
    row = pl.BlockSpec((TM, D), lambda i: (i, 0))
    vec = pl.BlockSpec((None, 1, D), lambda i: (l, 0, 0))
    in_specs = [row, row] + _mod_specs(n_p, cols) + [vec, vec]
    args = [x, mix] + [ada_p] * len(cols) + [mod_s] * len(cols) + [ln_g, ln_b]
    out_shape = [jax.ShapeDtypeStruct((M, D), F32)]
    out_specs = [row]
    if with_h:
        out_shape.append(jax.ShapeDtypeStruct((M, D), BF16))
        out_specs.append(row)
    res = pl.pallas_call(
        functools.partial(_resid_kernel, n_p=n_p, with_h=with_h), grid=(M // TM,),
        in_specs=in_specs, out_specs=out_specs, out_shape=out_shape,
        compiler_params=_cp(("arbitrary",)), name="resid_h" if with_h else "resid")(*args)
    return res if with_h else res[0]


def _ffn_kernel(h_ref, wg_ref, wi_ref, o_ref, wgb_ref, wib_ref):
    @pl.when(pl.program_id(1) == 0)
    def _():
        wgb_ref[...] = wg_ref[...].astype(BF16)
        wib_ref[...] = wi_ref[...].astype(BF16)

    h = h_ref[...]
    a = _dot(h, wgb_ref[...])
    b = _dot(h, wib_ref[...])
    o_ref[...] = (a * _sigmoid(a) * b).astype(BF16)


def _ffn_call(h, wg, wi, l, tm, tn):
    M, K = h.shape
    N = wg.shape[-1]
    w_spec = pl.BlockSpec((None, K, tn), lambda j, i: (l, 0, j))
    return pl.pallas_call(
        _ffn_kernel, grid=(N // tn, M // tm),
        in_specs=[pl.BlockSpec((tm, K), lambda j, i: (i, 0)), w_spec, w_spec],
        out_specs=pl.BlockSpec((tm, tn), lambda j, i: (i, j)),
        out_shape=jax.ShapeDtypeStruct((M, N), BF16),
        scratch_shapes=[pltpu.VMEM((K, tn), BF16), pltpu.VMEM((K, tn), BF16)],
        compiler_params=_cp(("arbitrary", "arbitrary")), name="ffn_act")(h, wg, wi)


def _merge_kernel(ya_ref, yb_ref, yc_ref, yd_ref, wb_ref, g0_ref, g1_ref, g2_ref, g3_ref, o_ref, wbb_ref):
    @pl.when(pl.program_id(1) == 0)
    def _():
        wbb_ref[...] = wb_ref[...].astype(BF16)

    acc = None
    for n, (y_ref, g_ref) in enumerate(((ya_ref, g0_ref), (yb_ref, g1_ref), (yc_ref, g2_ref), (yd_ref, g3_ref))):
        up = _dot(y_ref[...].astype(BF16), wbb_ref[n])
        t = _sigmoid(g_ref[...]) * up
        acc = t if acc is None else acc + t
    o_ref[...] = acc.astype(BF16)


def _merge_call(ys, w_branch, z, l, tm, tn):
    M = z.shape[0]
    W = BRANCH_W
    D = D_MODEL
    nj = D // tn
    y_spec = pl.BlockSpec((tm, W), lambda j, i: (i, 0))
    g_specs = [pl.BlockSpec((tm, tn), lambda j, i, n=n: (i, (Z_G + n * D) // tn + j)) for n in range(4)]
    return pl.pallas_call(
        _merge_kernel, grid=(nj, M // tm),
        in_specs=[y_spec] * 4 + [pl.BlockSpec((None, 4, W, tn), lambda j, i: (l, 0, 0, j))] + g_specs,
        out_specs=pl.BlockSpec((tm, tn), lambda j, i: (i, j)),
        out_shape=jax.ShapeDtypeStruct((M, D), BF16),
        scratch_shapes=[pltpu.VMEM((4, W, tn), BF16)],
        compiler_params=_cp(("arbitrary", "arbitrary")), name="merge")(*ys, w_branch, z, z, z, z)


def _rwkv_pre_kernel(za_ref, prev_ref, fill_ref, mu_ref, w0_ref, a0_ref, kk_ref, ka_ref, w2_ref, a2_ref,
                     g2_ref, e_ref, r_o, lw_o, k_o, v_o, kn_o, b_o, g_o, *, n_p, seq_s):
    i = pl.program_id(0)
    is_s = i >= n_p
    za = za_ref[...]
    rows = _iota(za.shape, 0)
    prev = jnp.where(rows == 0, prev_ref[7:8, :], pltpu.roll(za, 1, axis=0))
    base = jnp.where(is_s, n_p * TM, 0)
    pmask = jnp.where(is_s, seq_s - 1, 0x3FFFFFFF)
    start = ((rows + (i * TM - base)) & pmask) == 0
    fill = jnp.where(is_s, fill_ref[...], 0.0)
    prev = jnp.where(start, fill, prev)
    zm = za + mu_ref[...] * (prev - za)
    W = BRANCH_W
    r = zm[:, 0:W]
    k = zm[:, W:2 * W]
    v = zm[:, 2 * W:3 * W]
    x = zm[:, 3 * W:4 * W]
    wl = _dot(jnp.tanh(x).astype(BF16), w2_ref[...].astype(BF16))
    y = -(w0_ref[...] + wl)
    softplus = jnp.maximum(y, 0.0) + jnp.log(1.0 + jnp.exp(-jnp.abs(y)))
    w = -softplus - 0.5
    logw = -jnp.exp(w)
    a = _sigmoid(a0_ref[...] + _dot(x.astype(BF16), a2_ref[...].astype(BF16)))
    g = _dot(_sigmoid(x).astype(BF16), g2_ref[...].astype(BF16))
    kk0 = k * kk_ref[...]
    ss = _dot_exact_r(kk0 * kk0, e_ref[...], n=3)
    kn = kk0 / jnp.maximum(jnp.sqrt(ss), 1e-12)
    k2 = k * (1.0 + (a - 1.0) * ka_ref[...])
    b = kn * a
    for h in range(A_HEADS):
        sl = slice(h * A_HD, (h + 1) * A_HD)
        r_o[h] = r[:, sl]
        lw_o[h] = logw[:, sl]
        k_o[h] = k2[:, sl]
        v_o[h] = v[:, sl]
        kn_o[h] = kn[:, sl]
        b_o[h] = b[:, sl]
        g_o[h] = g[:, sl]


def _rwkv_pre_call(z, fill_s, mu, w0, a0, k_k, k_a, w2p, a2p, g2p, e64, n_p, seq_s):
    M = z.shape[0]
    W = BRANCH_W
    cb = Z_A // 2048
    vec = lambda n: pl.BlockSpec((1, n), lambda i: (0, 0))
    mat = pl.BlockSpec((W, W), lambda i: (0, 0))
    out = jax.ShapeDtypeStruct((A_HEADS, M, A_HD), F32)
    ospec = pl.BlockSpec((A_HEADS, TM, A_HD), lambda i: (0, i, 0))
    return pl.pallas_call(
        functools.partial(_rwkv_pre_kernel, n_p=n_p, seq_s=seq_s), grid=(M // TM,),
        in_specs=[pl.BlockSpec((TM, 2048), lambda i: (i, cb)),
                  pl.BlockSpec((8, 2048), lambda i: (jnp.maximum(i * (TM // 8) - 1, 0), cb)),
                  pl.BlockSpec((TM, 2048), lambda i: (jnp.maximum(i - n_p, 0), 0)),
                  vec(2048), vec(W), vec(W), vec(W), vec(W), mat, mat, mat, mat],
        out_specs=[ospec] * 7, out_shape=[out] * 7,
        compiler_params=_cp(("arbitrary",)), name="rwkv_pre")(
            z, z, fill_s, mu, w0, a0, k_k, k_a, w2p, a2p, g2p, e64)


def _rwkv_chunk_kernel(r_ref, lw_ref, k_ref, v_ref, kn_ref, b_ref, g_ref, s0_ref, rk_ref, lng_ref, lnb_ref,
                       *rest, C, nc):
    y_ref, sfin_ref, S_ref = rest[-3:]
    c = pl.program_id(1)

    @pl.when(c == 0)
    def _():
        S_ref[...] = s0_ref[...]

    H = range(A_HEADS)
    row = _iota((C, C), 0)
    col = _iota((C, C), 1)
    tri = (row >= col).astype(BF16)
    eye = (row == col).astype(F32)
    row2 = _iota((2 * C, C), 0)
    col2 = _iota((2 * C, C), 1)
    low2 = jnp.where(row2 < C, row2 - 1, row2 - C) >= col2
    eye_k = _iota((A_HD, A_HD), 0) == _iota((A_HD, A_HD), 1)
    r = [r_ref[h] for h in H]
    lw = [lw_ref[h] for h in H]
    k = [k_ref[h] for h in H]
    v = [v_ref[h] for h in H]
    kn = [kn_ref[h] for h in H]
    b = [b_ref[h] for h in H]
    S0 = [S_ref[h] for h in H]
    L = [_dot_exact_l(tri, lw[h]) for h in H]
    eL = [jnp.exp(L[h]) for h in H]
    eN = [jnp.exp(-L[h]) for h in H]
    eE = [jnp.exp(L[h][C - 1:C, :] - L[h]) for h in H]
    lhs = [jnp.concatenate([kn[h] * jnp.exp(L[h] - lw[h]), r[h] * eL[h]], axis=0) for h in H]
    Ab = [jnp.where(low2, _dot1(lhs[h], b[h] * eN[h], NT), 0.0) for h in H]
    Ak = [jnp.where(low2, _dot1(lhs[h], k[h] * eN[h], NT), 0.0) for h in H]
    N = [Ab[h][0:C] for h in H]
    X = [eye - N[h] for h in H]
    P = [_dot1(N[h], N[h]) for h in H]
    n = 2
    while n < C:
        X = [X[h] + _dot1(X[h], P[h]) for h in H]
        n *= 2
        if n < C:
            P = [_dot1(P[h], P[h]) for h in H]
    AV = [_dot1(Ak[h], v[h]) for h in H]
    KS = [_dot1(lhs[h], S0[h]) for h in H]
    U = [_dot1(X[h], KS[h][0:C] + AV[h][0:C]) for h in H]
    Y = [KS[h][C:] + AV[h][C:] - _dot1(Ab[h][C:], U[h]) for h in H]
    gcol = [jnp.sum(jnp.where(eye_k, jnp.broadcast_to(eL[h][C - 1:C, :], (A_HD, A_HD)), 0.0), axis=1, keepdims=True)
            for h in H]
    S1 = [S0[h] * gcol[h] + _dot1(jnp.concatenate([k[h] * eE[h], -(b[h] * eE[h])], axis=0),
                                  jnp.concatenate([v[h], U[h]], axis=0), TN) for h in H]
    out = []
    for h in H:
        yn = _ln(Y[h], RWKV_GN_EPS) * lng_ref[h] + lnb_ref[h]
        bonus = jnp.sum(r[h] * k[h] * rk_ref[h], axis=1, keepdims=True) * v[h]
        out.append((yn + bonus) * g_ref[h])
    y_ref[...] = jnp.concatenate(out, axis=1)
    for h in H:
        S_ref[h] = S1[h]

    @pl.when(c == nc - 1)
    def _():
        for h in H:
            sfin_ref[h] = S1[h]


def _rwkv_chunk_call(pre, s0t, r_k, ln_g, ln_b, row0, B, T, C, into):
    nc = T // C
    blk0 = row0 // C
    in_spec = pl.BlockSpec((A_HEADS, C, A_HD), lambda b, c: (0, blk0 + b * nc + c, 0))
    hvec = pl.BlockSpec((A_HEADS, 1, A_HD), lambda b, c: (0, 0, 0))
    st = pl.BlockSpec((None, A_HEADS, A_HD, A_HD), lambda b, c: (b, 0, 0, 0))
    return pl.pallas_call(
        functools.partial(_rwkv_chunk_kernel, C=C, nc=nc), grid=(B, nc),
        in_specs=[in_spec] * 7 + [st, hvec, hvec, hvec, pl.BlockSpec(memory_space=pl.ANY)],
        out_specs=[pl.BlockSpec((C, BRANCH_W), lambda b, c: (blk0 + b * nc + c, 0)), st],
        out_shape=[jax.ShapeDtypeStruct(into.shape, F32),
                   jax.ShapeDtypeStruct((B, A_HEADS, A_HD, A_HD), F32)],
        scratch_shapes=[pltpu.VMEM((A_HEADS, A_HD, A_HD), F32)],
        input_output_aliases={11: 0},
        compiler_params=_cp(("arbitrary", "arbitrary")), name="rwkv_chunk_c%d" % C)(
            *pre, s0t, r_k, ln_g, ln_b, into)


def _foxcum_kernel(fl_ref, bf_ref, lf_ref, cum_ref, carry_ref):
    @pl.when(pl.program_id(0) == 0)
    def _():
        carry_ref[...] = jnp.zeros_like(carry_ref)

    lane = _iota((TM, LANES), 1)
    lf = jnp.where(lane < NH, _log_sigmoid(fl_ref[...] + bf_ref[...]), 0.0)
    tri = (_iota((TM, TM), 0) >= _iota((TM, TM), 1)).astype(BF16)
    cum = _dot_exact_l(tri, lf) + carry_ref[...]
    lf_ref[...] = lf
    cum_ref[...] = cum
    carry_ref[...] = cum[TM - 1:TM, :]


def _foxcum_call(z, bf):
    M = z.shape[0]
    cb = (Z_B + 3 * BRANCH_W) // LANES
    spec = pl.BlockSpec((TM, LANES), lambda i: (i, 0))
    return pl.pallas_call(
        _foxcum_kernel, grid=(M // TM,),
        in_specs=[pl.BlockSpec((TM, LANES), lambda i: (i, cb)), pl.BlockSpec((1, LANES), lambda i: (0, 0))],
        out_specs=[spec, spec], out_shape=[jax.ShapeDtypeStruct((M, LANES), F32)] * 2,
        scratch_shapes=[pltpu.VMEM((1, LANES), F32)],
        compiler_params=_cp(("arbitrary",)), name="fox_cum")(z, bf)


def _attn_prep_kernel(fq_ref, fk_ref, fv_ref, mq_ref, mk_ref, mv_ref, cum_ref, cprev_ref, cqprev_ref, km_ref,
                      fqT_o, fkA_o, fvT_o, mqT_o, mk_o, mvT_o, sel_o, *, scale):
    i = pl.program_id(0)
    lane = _iota((TM, LANES), 1)
    cum = cum_ref[...]
    qscale = scale * LOG2E
    crel_k = (cum - jnp.where(i == 0, 0.0, cprev_ref[7:8, :])) * LOG2E
    crel_q = (cum - jnp.where(i < FLASH_TQ // TM, 0.0, cqprev_ref[7:8, :])) * LOG2E
    fq = fq_ref[...]
    fk = fk_ref[...]
    fv = fv_ref[...]
    mq = mq_ref[...]
    mk = mk_ref[...]
    mv = mv_ref[...]
    km = km_ref[...]
    blk = _iota((km.shape[0], TM), 0)
    blk_f = blk.astype(F32)
    for h in range(NH):
        sl = slice(h * HD, (h + 1) * HD)
        q3 = [p.astype(F32) for p in _split3(jnp.broadcast_to(crel_q[:, h:h + 1], (TM, LANES)))]
        k3 = [p.astype(F32) for p in _split3(jnp.broadcast_to(crel_k[:, h:h + 1], (TM, LANES)))]
        qb = jnp.where(lane == 0, q3[0], jnp.where(lane == 1, q3[1], jnp.where(lane == 2, q3[2],
                                                                             jnp.where(lane < 6, 1.0, 0.0))))
        kb = jnp.where(lane < 3, 1.0, jnp.where(lane == 3, -k3[0], jnp.where(lane == 4, -k3[1],
                                                                              jnp.where(lane == 5, -k3[2], 0.0))))
        fqT_o[h, 0:HD, :] = (fq[:, sl] * qscale).T.astype(BF16)
        fqT_o[h, HD:2 * HD, :] = qb.T.astype(BF16)
        fkA_o[h, :, 0:HD] = fk[:, sl].astype(BF16)
        fkA_o[h, :, HD:2 * HD] = kb.astype(BF16)
        fvT_o[h] = fv[:, sl].T.astype(BF16)
        mqT = mq[:, sl].T
        mqT_o[h] = (mqT * qscale).astype(BF16)
        mk_o[h] = mk[:, sl].astype(BF16)
        mvT_o[h] = mv[:, sl].T.astype(BF16)
        sc = _dot3(km[:, sl], mqT)
        sel_o[h] = _top3_select(sc, blk < i, blk_f, axis=0)


def _attn_prep_call(z, cum, kmean, Tp):
    W = BRANCH_W
    nbp = kmean.shape[0]
    cbB = Z_B // W
    cbC = Z_C // W
    rq = FLASH_TQ // TM
    zs = lambda cb: pl.BlockSpec((TM, W), lambda i: (i, cb))
    colT = lambda rows: pl.BlockSpec((NH, rows, TM), lambda i: (0, 0, i))
    rowm = lambda cols: pl.BlockSpec((NH, TM, cols), lambda i: (0, i, 0))
    sd = jax.ShapeDtypeStruct
    return pl.pallas_call(
        functools.partial(_attn_prep_kernel, scale=HD ** -0.5), grid=(Tp // TM,),
        in_specs=[zs(cbB), zs(cbB + 1), zs(cbB + 2), zs(cbC), zs(cbC + 1), zs(cbC + 2),
                  pl.BlockSpec((TM, LANES), lambda i: (i, 0)),
                  pl.BlockSpec((8, LANES), lambda i: (jnp.maximum(i * (TM // 8) - 1, 0), 0)),
                  pl.BlockSpec((8, LANES), lambda i: (jnp.maximum((i // rq) * (FLASH_TQ // 8) - 1, 0), 0)),
                  pl.BlockSpec((nbp, W), lambda i: (0, 0))],
        out_specs=[colT(2 * HD), rowm(2 * HD), colT(HD), colT(HD), rowm(HD), colT(HD), colT(nbp)],
        out_shape=[sd((NH, 2 * HD, Tp), BF16), sd((NH, Tp, 2 * HD), BF16), sd((NH, HD, Tp), BF16),
                   sd((NH, HD, Tp), BF16), sd((NH, Tp, HD), BF16), sd((NH, HD, Tp), BF16),
                   sd((NH, nbp, Tp), F32)],
        compiler_params=_cp(("arbitrary",)), name="attn_prep")(z, z, z, z, z, z, cum, cum, cum, kmean)


def _flash_kernel(ti_ref, tj_ref, bq_ref, bk_ref, qT_ref, kA_ref, vT_ref, *rest, moba):
    sel_ref = rest[0] if moba else None
    o_ref, m_ref, l_ref, acc_ref = rest[-4:]
    step = pl.program_id(0)
    i = ti_ref[step]
    j = tj_ref[step]
    T = kA_ref.shape[1]
    Tq = qT_ref.shape[2]
    ratio = Tq // T
    nqt = bq_ref.shape[0] // NH
    nkt = bk_ref.shape[0] // NH

    @pl.when(j == 0)
    def _():
        m_ref[...] = jnp.full(m_ref.shape, NEG, F32)
        l_ref[...] = jnp.zeros(l_ref.shape, F32)
        acc_ref[...] = jnp.zeros(acc_ref.shape, F32)

    H = range(NH)

    def step_body(on_diagonal):
        sT = [_dot(kA_ref[h], qT_ref[h]) for h in H]
        if not moba:
            off = [(bq_ref[h * nqt + i] - bk_ref[h * nkt + j]) * LOG2E for h in H]
            sT = [sT[h] + off[h] for h in H]
        if on_diagonal:
            causal = (_iota((T, Tq), 0) - _iota((T, Tq), 1)) <= (i * Tq - j * T)
            sT = [jnp.where(causal, sT[h], NEG) for h in H]
        if moba:
            own = (_iota((1, Tq), 1) // T + i * ratio) == j
            sT = [sT[h] + (jnp.where(own, 1.0, sel_ref[h, pl.ds(j, 1), :]) - 1.0) * (-NEG) for h in H]
        m_prev = [m_ref[h] for h in H]
        m_new = [jnp.maximum(m_prev[h], jnp.max(sT[h], axis=0, keepdims=True)) for h in H]
        p = [jnp.exp2(sT[h] - m_new[h]) for h in H]
        alpha = [jnp.exp2(m_prev[h] - m_new[h]) for h in H]
        pv = [_dot(vT_ref[h], p[h].astype(BF16)) for h in H]
        for h in H:
            l_ref[h] = alpha[h] * l_ref[h] + jnp.sum(p[h], axis=0, keepdims=True)
            acc_ref[h] = alpha[h] * acc_ref[h] + pv[h]
            m_ref[h] = m_new[h]

    crosses = (j + 1) * T - 1 > i * Tq
    pl.when(crosses)(functools.partial(step_body, True))
    pl.when(jnp.logical_not(crosses))(functools.partial(step_body, False))

    @pl.when(j == (i + 1) * ratio - 1)
    def _():
        for h in range(NH):
            o_ref[:, h * HD:(h + 1) * HD] = (acc_ref[h] / l_ref[h]).T


def _flash_call(qT, kA, vT, base_q, base_k, sel, Tp, into, name):
    T = MOBA_BLOCK
    Tq = FLASH_TQ
    ratio = Tq // T
    Kc = kA.shape[2]
    moba = sel is not None
    ti = np.concatenate([np.full((i + 1) * ratio, i, np.int32) for i in range(Tp // Tq)])
    tj = np.concatenate([np.arange((i + 1) * ratio, dtype=np.int32) for i in range(Tp // Tq)])
    in_specs = [pl.BlockSpec((NH, Kc, Tq), lambda s, ti, tj, bq, bk: (0, 0, ti[s])),
                pl.BlockSpec((NH, T, Kc), lambda s, ti, tj, bq, bk: (0, tj[s], 0)),
                pl.BlockSpec((NH, HD, T), lambda s, ti, tj, bq, bk: (0, 0, tj[s]))]
    args = [qT, kA, vT]
    if moba:
        in_specs.append(pl.BlockSpec((NH, sel.shape[1], Tq), lambda s, ti, tj, bq, bk: (0, 0, ti[s])))
        args.append(sel)
    in_specs.append(pl.BlockSpec(memory_space=pl.ANY))
    return pl.pallas_call(
        functools.partial(_flash_kernel, moba=moba),
        grid_spec=pltpu.PrefetchScalarGridSpec(
            num_scalar_prefetch=4, grid=(len(ti),), in_specs=in_specs,
            out_specs=pl.BlockSpec((Tq, NH * HD), lambda s, ti, tj, bq, bk: (ti[s], 0)),
            scratch_shapes=[pltpu.VMEM((NH, 1, Tq), F32), pltpu.VMEM((NH, 1, Tq), F32),
                            pltpu.VMEM((NH, HD, Tq), F32)]),
        out_shape=jax.ShapeDtypeStruct(into.shape, F32),
        input_output_aliases={4 + len(args): 0},
        compiler_params=_cp(("arbitrary",)), name=name)(
            jnp.asarray(ti), jnp.asarray(tj), base_q, base_k, *args, into)


def _top3_select(sc, valid, idx_f, axis=1):
    sc = jnp.where(valid, sc, -jnp.inf)
    sel = jnp.zeros(sc.shape, F32)
    for _ in range(MOBA_TOPK):
        mx = jnp.max(sc, axis=axis, keepdims=True)
        idx = jnp.min(jnp.where(sc == mx, idx_f, 1e9), axis=axis, keepdims=True)
        hit = idx_f == idx
        sel = jnp.where(hit & valid, 1.0, sel)
        sc = jnp.where(hit, -jnp.inf, sc)
    return sel


def _blockmean_kernel(k_ref, o_ref):
    o_ref[...] = jnp.sum(k_ref[...], axis=0, keepdims=True) * (1.0 / MOBA_BLOCK)


def _blockmean_call(z, Tp):
    nb = Tp // MOBA_BLOCK
    W = BRANCH_W
    cb = Z_C // W + 1
    return pl.pallas_call(
        _blockmean_kernel, grid=(nb,),
        in_specs=[pl.BlockSpec((MOBA_BLOCK, W), lambda i: (i, cb))],
        out_specs=pl.BlockSpec((None, 1, W), lambda i: (i, 0, 0)),
        out_shape=jax.ShapeDtypeStruct((nb, 1, W), F32),
        compiler_params=_cp(("arbitrary",)), name="moba_kmean")(z)


def _ret_kernel(q_ref, k_ref, v_ref, gd_ref, cos_ref, sin_ref, din_ref, qd_ref, kd_ref, cd_ref, s0_ref, lng_ref,
                *rest, nc):
    y_ref, sfin_ref, S_ref = rest[-3:]
    c = pl.program_id(1)

    @pl.when(c == 0)
    def _():
        S_ref[...] = s0_ref[...]

    q = q_ref[...]
    k = k_ref[...]
    v = v_ref[...]
    gd = gd_ref[...]
    cos = cos_ref[...]
    sin = sin_ref[...]
    lng = lng_ref[...]
    for h in range(NH):
        sl = slice(h * HD, (h + 1) * HD)
        qh = q[:, sl]
        kh = k[:, sl]
        qr = qh * cos + pltpu.roll(qh, HD // 2, axis=1) * sin
        kr = (kh * cos + pltpu.roll(kh, HD // 2, axis=1) * sin) * (HD ** -0.5)
        vb = v[:, sl].astype(BF16)
        qb = qr.astype(BF16)
        att = _dot(qb, kr.astype(BF16), NT) * din_ref[h]
        S = S_ref[h]
        o = _dot(att.astype(BF16), vb) + _dot(qb, S.astype(BF16)) * qd_ref[h]
        S_ref[h] = S * cd_ref[h] + _dot((kr * kd_ref[h]).astype(BF16), vb, TN)
        y_ref[:, sl] = _ln(o, RET_GN_EPS) * lng[:, sl] * (gd[:, sl] * _sigmoid(gd[:, sl]))

    @pl.when(c == nc - 1)
    def _():
        sfin_ref[...] = S_ref[...]


def _ret_tables(C):
    lg = np.log(1.0 - 2.0 ** (-5.0 - np.arange(NH, dtype=np.float32))).astype(np.float32)
    i = np.arange(C, dtype=np.float32)
    diff = i[:, None] - i[None, :]
    din = np.where(diff[None] >= 0, np.exp(np.maximum(diff, 0.0)[None] * lg[:, None, None]), 0.0)
    qd = np.exp((i[None, :] + 1.0) * lg[:, None])
    kd = np.exp((C - 1.0 - i)[None, :] * lg[:, None])
    cd = np.exp(C * lg)
    bc = lambda t: np.ascontiguousarray(np.broadcast_to(t[:, :, None], (NH, C, HD))).astype(np.float32)
    cdb = np.ascontiguousarray(np.broadcast_to(cd[:, None, None], (NH, 1, HD))).astype(np.float32)
    return din.astype(np.float32), bc(qd), bc(kd), cdb


def _rope_tables(pos0, T):
    half = HD // 2
    inv = 1.0 / (ROPE_BASE ** (jnp.arange(half, dtype=F32) / half))
    ang = (pos0 + jnp.arange(T)).astype(F32)[:, None] * inv[None, :]
    cos = jnp.cos(ang)
    sin = jnp.sin(ang)
    return jnp.concatenate([cos, cos], axis=1), jnp.concatenate([-sin, sin], axis=1)


def _ret_call(z, s0, ln_g, l, row0, B, T, C, pos0, into):
    nc = T // C
    W = BRANCH_W
    cb = Z_D // W
    blk0 = row0 // C
    cos, sin = _rope_tables(pos0, T)
    din, qd, kd, cd = _ret_tables(C)
    zs = lambda off: pl.BlockSpec((C, W), lambda b, c: (blk0 + b * nc + c, cb + off))
    tab = pl.BlockSpec((C, HD), lambda b, c: (c, 0))
    full = lambda shape: pl.BlockSpec(shape, lambda b, c: (0,) * len(shape))
    st = pl.BlockSpec((None, NH, HD, HD), lambda b, c: (b, 0, 0, 0))
    return pl.pallas_call(
        functools.partial(_ret_kernel, nc=nc), grid=(B, nc),
        in_specs=[zs(0), zs(1), zs(2), zs(3), tab, tab, full((NH, C, C)), full((NH, C, HD)), full((NH, C, HD)),
                  full((NH, 1, HD)), st, pl.BlockSpec((None, 1, W), lambda b, c: (l, 0, 0)),
                  pl.BlockSpec(memory_space=pl.ANY)],
        out_specs=[pl.BlockSpec((C, W), lambda b, c: (blk0 + b * nc + c, 0)), st],
        out_shape=[jax.ShapeDtypeStruct(into.shape, F32), jax.ShapeDtypeStruct((B, NH, HD, HD), F32)],
        scratch_shapes=[pltpu.VMEM((NH, HD, HD), F32)],
        input_output_aliases={12: 0},
        compiler_params=_cp(("arbitrary", "arbitrary")), name="ret_c%d" % C)(
            z, z, z, z, cos, sin, din, qd, kd, cd, s0, ln_g, into)


def _lfsuf_kernel(pt_ref, lf_hbm, ts_ref, tot_ref, o_ref, buf, sem, *, npg, base):
    b = pl.program_id(0)

    def page_copy(p):
        return pltpu.make_async_copy(lf_hbm.at[pl.ds(NH * (base + pt_ref[b * npg + p]), NH), :],
                                     buf.at[pl.ds(NH * p, NH), :], sem.at[p])

    for p in range(npg):
        page_copy(p).start()
    for p in range(npg):
        page_copy(p).wait()
    loc = None
    tot = None
    for h in range(NH):
        x = buf[pl.ds(h, npg, stride=NH), :]
        a = _dot_exact_r(x, ts_ref[h])
        t = _dot_exact_r(x, tot_ref[h])
        loc = a if loc is None else loc + a
        tot = t if tot is None else tot + t
    later = (_iota((npg, npg), 1) > _iota((npg, npg), 0)).astype(BF16)
    o_ref[...] = loc + _dot_exact_l(later, tot)


def _lfsuf_call(pt_flat, lf_rows, ts, tot, B, npg, base):
    PW = PAGE_SIZE * NH
    sel = pl.BlockSpec((NH, PAGE_SIZE, PW), lambda b, pt: (0, 0, 0))
    return pl.pallas_call(
        functools.partial(_lfsuf_kernel, npg=npg, base=base),
        grid_spec=pltpu.PrefetchScalarGridSpec(
            num_scalar_prefetch=1, grid=(B,),
            in_specs=[pl.BlockSpec(memory_space=pl.ANY), sel, sel],
            out_specs=pl.BlockSpec((None, npg, PW), lambda b, pt: (b, 0, 0)),
            scratch_shapes=[pltpu.VMEM((NH * npg, PAGE_SIZE), F32), pltpu.SemaphoreType.DMA((npg,))]),
        out_shape=jax.ShapeDtypeStruct((B, npg, PW), F32),
        compiler_params=_cp(("arbitrary",)), name="fox_logf_suffix")(pt_flat, lf_rows, ts, tot)


def _suffix_matrices():
    r = np.arange(PAGE_SIZE)
    dst_r = np.repeat(r, NH)
    dst_h = np.tile(np.arange(NH), PAGE_SIZE)
    same = np.arange(NH)[:, None, None] == dst_h[None, None, :]
    ts = same & (r[None, :, None] > dst_r[None, None, :])
    return jnp.asarray(ts, BF16), jnp.asarray(np.broadcast_to(same, ts.shape), BF16)


def _sattn_kernel(pt_ref, fq_ref, fkn_ref, fvn_ref, lfn_ref, mq_ref, mkn_ref, mvn_ref, R_ref, *rest, nb, S, scale):
    P = SAMPLE_PAGES
    fk, fv, mk, mv = (rest[t * P:(t + 1) * P] for t in range(4))
    yb_ref, yc_ref, fqb, mqb, mqf, cnb, fm, fl, facc, sc_all, m_all, l_all, o_all = rest[-13:]
    n = pl.program_id(1)
    R4 = NH * S
    PW = PAGE_SIZE * NH
    own_head = (_iota((R4, PW), 1) % NH) == (_iota((R4, PW), 0) // S)
    lane = _iota((R4, LANES), 1)

    def stack_heads(x):
        return jnp.concatenate([x[:, h * HD:(h + 1) * HD] for h in range(NH)], axis=0)

    def head_sums(kpage):
        return jnp.sum(kpage.reshape(PW // 8, 8, HD), axis=0)

    @pl.when(n == 0)
    def _():
        fqb[...] = stack_heads(fq_ref[...]).astype(BF16)
        mq = stack_heads(mq_ref[...])
        mqb[...] = mq.astype(BF16)
        mqf[...] = mq
        tri = (_iota((S, S), 0) >= _iota((S, S), 1)).astype(BF16)
        cn = _dot_exact_l(tri, lfn_ref[...])
        cnb[...] = jnp.concatenate([jnp.broadcast_to(cn[:, h:h + 1], (S, LANES)) for h in range(NH)], axis=0)
        fm[...] = jnp.full(fm.shape, NEG, F32)
        fl[...] = jnp.zeros(fl.shape, F32)
        facc[...] = jnp.zeros(facc.shape, F32)
        sc_all[...] = jnp.full(sc_all.shape, -jnp.inf, F32)
        m_all[...] = jnp.full(m_all.shape, NEG, F32)
        l_all[...] = jnp.zeros(l_all.shape, F32)

    def fox_update(s_list, v_list):
        m_prev = fm[...]
        m_new = m_prev
        for s in s_list:
            m_new = jnp.maximum(m_new, jnp.max(s, axis=1, keepdims=True))
        alpha = jnp.exp(m_prev - m_new)
        p_list = [jnp.exp(s - m_new) for s in s_list]
        l_new = alpha * fl[...]
        acc = alpha * facc[...]
        for p, v_bf in zip(p_list, v_list):
            l_new = l_new + jnp.sum(p, axis=1, keepdims=True)
            acc = acc + _dot(p.astype(BF16), v_bf)
        fl[...] = l_new
        facc[...] = acc
        fm[...] = m_new

    cn_b = cnb[...]
    cn_w = jnp.concatenate([cn_b] * NH, axis=1)
    fq_b = fqb[...]
    s_fox = [_dot(fq_b, fk[g][...].astype(BF16), NT) * scale for g in range(P)]
    s_fox = [jnp.where(own_head, s_fox[g] + cn_w + R_ref[pl.ds(P * n + g, 1), :], NEG) for g in range(P)]
    fox_update(s_fox, [fv[g][...].astype(BF16) for g in range(P)])

    mq_b = mqb[...]
    mq_f = mqf[...]
    kpg = [mk[g][...] for g in range(P)]
    s_mo = [jnp.where(own_head, _dot(mq_b, kpg[g].astype(BF16), NT) * scale, NEG) for g in range(P)]
    for blk in range(P // 2):
        g0, g1 = 2 * blk, 2 * blk + 1
        ks = head_sums(kpg[g0]) + head_sums(kpg[g1])
        kmean = (ks[0:NH] + ks[NH:2 * NH]) * (1.0 / MOBA_BLOCK)
        kmean = jnp.concatenate([jnp.broadcast_to(kmean[h:h + 1], (S, HD)) for h in range(NH)], axis=0)
        sc_col = jnp.sum(mq_f * kmean, axis=1, keepdims=True)
        m_b = jnp.maximum(jnp.max(s_mo[g0], axis=1, keepdims=True), jnp.max(s_mo[g1], axis=1, keepdims=True))
        p0 = jnp.exp(s_mo[g0] - m_b)
        p1 = jnp.exp(s_mo[g1] - m_b)
        here = lane == (P // 2) * n + blk
        sc_all[...] = jnp.where(here, sc_col, sc_all[...])
        m_all[...] = jnp.where(here, m_b, m_all[...])
        l_all[...] = jnp.where(here, jnp.sum(p0, axis=1, keepdims=True) + jnp.sum(p1, axis=1, keepdims=True),
                               l_all[...])
        o_all[(P // 2) * n + blk] = (_dot(p0.astype(BF16), mv[g0][...].astype(BF16))
                                     + _dot(p1.astype(BF16), mv[g1][...].astype(BF16)))

    @pl.when(n == pl.num_programs(1) - 1)
    def _():
        ri = _iota((R4, R4), 0)
        ci = _iota((R4, R4), 1)
        causal = ((ri // S) == (ci // S)) & ((ci % S) <= (ri % S))
        cn_col = cn_b[:, 0:1]
        cn_row = jnp.sum(jnp.where(ri == ci, jnp.broadcast_to(cn_col, (R4, R4)), 0.0), axis=0, keepdims=True)
        s = _dot(fqb[...], stack_heads(fkn_ref[...]).astype(BF16), NT) * scale
        s = s + (cn_col - cn_row)
        fox_update([jnp.where(causal, s, NEG)], [stack_heads(fvn_ref[...]).astype(BF16)])
        out = facc[...] / fl[...]
        for h in range(NH):
            yb_ref[:, h * HD:(h + 1) * HD] = out[h * S:(h + 1) * S, :]

        s = _dot(mqb[...], stack_heads(mkn_ref[...]).astype(BF16), NT) * scale
        s = jnp.where(causal, s, NEG)
        m_o = jnp.max(s, axis=1, keepdims=True)
        p = jnp.exp(s - m_o)
        l_o = jnp.sum(p, axis=1, keepdims=True)
        o_o = _dot(p.astype(BF16), stack_heads(mvn_ref[...]).astype(BF16))
        sel = _top3_select(sc_all[...], lane < nb, lane.astype(F32)) > 0.5
        m_past = m_all[...]
        m_tot = jnp.maximum(jnp.max(jnp.where(sel, m_past, NEG), axis=1, keepdims=True), m_o)
        w = jnp.where(sel, jnp.exp(m_past - m_tot), 0.0)
        w_o = jnp.exp(m_o - m_tot)
        den = jnp.sum(w * l_all[...], axis=1, keepdims=True) + w_o * l_o
        num = w_o * o_o
        for blk in range(nb):
            num = num + w[:, blk:blk + 1] * o_all[blk]
        out = num / den
        for h in range(NH):
            yc_ref[:, h * HD:(h + 1) * HD] = out[h * S:(h + 1) * S, :]


def _sattn_call(pt_flat, z, lf, R, caches, l, Mp, B, S, npg, n_pool, yb_into, yc_into):
    fk, fv, mk, mv = caches
    W = BRANCH_W
    PW = PAGE_SIZE * NH
    nb = npg // 2
    R4 = NH * S
    rb = Mp // S
    base = l * n_pool
    zrow = lambda cb: pl.BlockSpec((S, W), lambda b, n, pt: (rb + b, cb))
    P = SAMPLE_PAGES
    page = lambda g: pl.BlockSpec((PW, HD), lambda b, n, pt: (base + pt[b * npg + P * n + g], 0))
    pages = [page(g) for g in range(P)]
    cbB = Z_B // W
    cbC = Z_C // W
    in_specs = [zrow(cbB), zrow(cbB + 1), zrow(cbB + 2),
                pl.BlockSpec((S, LANES), lambda b, n, pt: (rb + b, 0)),
                zrow(cbC), zrow(cbC + 1), zrow(cbC + 2),
                pl.BlockSpec((None, npg, PW), lambda b, n, pt: (b, 0, 0))] + pages * 4
    in_specs += [pl.BlockSpec(memory_space=pl.ANY)] * 2
    n_in = 1 + len(in_specs)
    out_spec = pl.BlockSpec((S, W), lambda b, n, pt: (rb + b, 0))
    scratch = [pltpu.VMEM((R4, HD), BF16), pltpu.VMEM((R4, HD), BF16), pltpu.VMEM((R4, HD), F32),
               pltpu.VMEM((R4, LANES), F32),
               pltpu.VMEM((R4, 1), F32), pltpu.VMEM((R4, 1), F32), pltpu.VMEM((R4, HD), F32),
               pltpu.VMEM((R4, LANES), F32), pltpu.VMEM((R4, LANES), F32), pltpu.VMEM((R4, LANES), F32),
               pltpu.VMEM((nb, R4, HD), F32)]
    return pl.pallas_call(
        functools.partial(_sattn_kernel, nb=nb, S=S, scale=HD ** -0.5),
        grid_spec=pltpu.PrefetchScalarGridSpec(
            num_scalar_prefetch=1, grid=(B, npg // P), in_specs=in_specs,
            out_specs=[out_spec, out_spec], scratch_shapes=scratch),
        out_shape=[jax.ShapeDtypeStruct(yb_into.shape, F32), jax.ShapeDtypeStruct(yc_into.shape, F32)],
        input_output_aliases={n_in - 2: 0, n_in - 1: 1},
        compiler_params=_cp(("arbitrary", "arbitrary")), name="sample_attn")(
            pt_flat, z, z, z, lf, z, z, z, R, *([fk] * P + [fv] * P + [mk] * P + [mv] * P), yb_into, yc_into)


def _pick_tile(m, cands):
    for t in cands:
        if m % t == 0:
            return t
    raise ValueError("no row tile divides %d" % m)


def _pad_cols(x, n):
    return jnp.pad(x, ((0, 0), (0, n - x.shape[1])))


def kernel(x_prompt, x_sample, c_prompt, c_sample, page_table, cache_fox_k, cache_fox_v, cache_fox_logf, cache_moba_k, cache_moba_v, state_rwkv, state_rwkv_shift, state_ret, w_ada, b_ada, w_in, rwkv_mu, rwkv_w0, rwkv_w2, rwkv_a0, rwkv_a2, rwkv_g2, rwkv_k_k, rwkv_k_a, rwkv_r_k, rwkv_ln_g, rwkv_ln_b, fox_bf, ret_ln_g, w_branch, w_o, ln1_g, ln1_b, ln2_g, ln2_b, w_ffn_gate, w_ffn_in, w_ffn_out):
    Bp, Tp, D = x_prompt.shape
    Bs, S, _ = x_sample.shape
    depth = w_in.shape[0]
    n_pool = cache_fox_k.shape[1]
    npg = page_table.shape[1]
    past = npg * PAGE_SIZE
    Mp, Ms = Bp * Tp, Bs * S
    M = Mp + Ms
    assert Bp == 1 and D == D_MODEL and depth == DEPTH
    assert Mp % TM == 0 and Ms % TM == 0 and Tp % MOBA_BLOCK == 0 and Tp // MOBA_BLOCK <= LANES
    assert Tp % RWKV_CHUNK == 0 and Tp % RET_CHUNK == 0 and S == 8 and past % MOBA_BLOCK == 0
    assert TM == MOBA_BLOCK and 2 * NH == 8 and MOBA_BLOCK == 2 * PAGE_SIZE
    assert Tp % FLASH_TQ == 0 and FLASH_TQ % MOBA_BLOCK == 0
    assert npg % SAMPLE_PAGES == 0 and SAMPLE_PAGES % 2 == 0 and npg // 2 <= LANES
    n_p = Mp // TM
    W = BRANCH_W
    tm_big = _pick_tile(M, (768, 512, 256))
    tm_small = _pick_tile(M, (384, 256))

    x = jnp.concatenate([x_prompt.reshape(Mp, D), x_sample.reshape(Ms, D)], axis=0)
    c_rows = 8 * (-(-(Bp + Bs) // 8))
    c_all = jnp.pad(jnp.concatenate([c_prompt, c_sample], axis=0), ((0, c_rows - Bp - Bs), (0, 0)))
    pt_flat = page_table.reshape(-1).astype(jnp.int32)
    cache2 = lambda t: t.reshape(depth * n_pool * PAGE_SIZE * NH, HD)
    caches = (cache2(cache_fox_k), cache2(cache_fox_v), cache2(cache_moba_k), cache2(cache_moba_v))
    lf_rows = jnp.swapaxes(cache_fox_logf, 2, 3).reshape(depth * n_pool * NH, PAGE_SIZE)
    ts_m, tot_m = _suffix_matrices()
    e64 = jnp.asarray(np.kron(np.eye(A_HEADS), np.ones((A_HD, A_HD))), BF16)
    b_ada3 = b_ada.reshape(depth, 1, -1)
    vec3 = lambda t: t.reshape(depth, 1, -1)
    ln1_g3, ln1_b3, ln2_g3, ln2_b3, ret_g3 = map(vec3, (ln1_g, ln1_b, ln2_g, ln2_b, ret_ln_g))

    assert w_in.shape[1] % W_IN_ROWS == 0 and w_in.shape[2] == _O_G + 4 * D_MODEL
    w_in_b = _win_relayout_call(w_in)

    per_layer = []
    for l in range(depth):
        ada = _ada_call(c_all, w_ada, b_ada3, l)
        ada_p = ada[0:8]
        mod_s = jnp.repeat(ada[Bp:Bp + Bs], S, axis=0)
        h = _lnmod_call(x, ada_p, mod_s, n_p, 1, 0)
        z = _mmb_call(h, w_in_b, l, tm_big, 1408, F32, "w_in")

        fill_s = jnp.repeat(_pad_cols(state_rwkv_shift[l], 2048), S, axis=0)
        pad_rows = lambda w2, r0: jnp.zeros((W, W), F32).at[r0:r0 + w2.shape[0]].set(w2)
        pre = _rwkv_pre_call(
            z, fill_s, _pad_cols(rwkv_mu[l][None], 2048), rwkv_w0[l][None], rwkv_a0[l][None],
            rwkv_k_k[l][None], rwkv_k_a[l][None], pad_rows(rwkv_w2[l], 0), pad_rows(rwkv_a2[l], A_LORA_W),
            pad_rows(rwkv_g2[l], A_LORA_W + A_LORA_A), e64, n_p, S)
        hv = lambda t: t.reshape(A_HEADS, 1, A_HD)
        rk, lg, lb = hv(rwkv_r_k[l]), hv(rwkv_ln_g[l]), hv(rwkv_ln_b[l])
        blank = jnp.zeros((M, W), F32)
        ya, sT_p = _rwkv_chunk_call(pre, jnp.zeros((Bp, A_HEADS, A_HD, A_HD), F32), rk, lg, lb,
                                    0, Bp, Tp, RWKV_CHUNK, blank)
        ya, sT_s = _rwkv_chunk_call(pre, jnp.swapaxes(state_rwkv[l], -1, -2), rk, lg, lb, Mp, Bs, S, S, ya)

        lf, cum = _foxcum_call(z, _pad_cols(fox_bf[l][None], LANES))
        nbk = Tp // MOBA_BLOCK
        kmean = _blockmean_call(z, Tp).reshape(nbk, W)
        kmean = jnp.pad(kmean, ((0, 8 * (-(-nbk // 8)) - nbk), (0, 0)))
        fqT, fkA, fvT, mqT, mkb, mvT, sel = _attn_prep_call(z, cum, kmean, Tp)
        tile_base = lambda t: jnp.transpose(jnp.concatenate(
            [jnp.zeros((1, NH), F32), cum[t - 1:Mp - 1:t, :NH]], axis=0)).reshape(-1)
        base_q, base_k = tile_base(FLASH_TQ), tile_base(MOBA_BLOCK)
        yb = _flash_call(fqT, fkA, fvT, base_q, base_k, None, Tp, blank, "fox_prompt")
        yc = _flash_call(mqT, mkb, mvT, base_q, base_k, sel, Tp, blank, "moba_prompt")
        R = _lfsuf_call(pt_flat, lf_rows, ts_m, tot_m, Bs, npg, l * n_pool)
        yb, yc = _sattn_call(pt_flat, z, lf, R, caches, l, Mp, Bs, S, npg, n_pool, yb, yc)

        yd, rS_p = _ret_call(z, jnp.zeros((Bp, NH, HD, HD), F32), ret_g3, l, 0, Bp, Tp, RET_CHUNK, 0, blank)
        yd, rS_s = _ret_call(z, state_ret[l], ret_g3, l, Mp, Bs, S, S, past, yd)

        mixin = _merge_call((ya, yb, yc, yd), w_branch, z, l, tm_big, 512)
        mix = _mm_call(mixin, w_o, l, tm_big, 512, F32, "w_o")
        x1, h2 = _resid_call(x, mix, ada_p, mod_s, ln1_g3, ln1_b3, l, n_p, 2, 4, 3)
        act = _ffn_call(h2, w_ffn_gate, w_ffn_in, l, tm_big, 512)
        f = _mm_call(act, w_ffn_out, l, tm_small, 512, F32, "ffn_out")
        x = _resid_call(x1, f, ada_p, mod_s, ln2_g3, ln2_b3, l, n_p, 5)

        zb = Z_B
        zc = Z_C
        hp = lambda t: t.reshape(Bp, Tp, NH, HD)
        hs = lambda t: t.reshape(Bs, S, NH, HD)
        per_layer.append(dict(
            fox_k_p=hp(z[:Mp, zb + W:zb + 2 * W]), fox_v_p=hp(z[:Mp, zb + 2 * W:zb + 3 * W]),
            fox_lf_p=lf[:Mp, :NH].reshape(Bp, Tp, NH),
            moba_k_p=hp(z[:Mp, zc + W:zc + 2 * W]), moba_v_p=hp(z[:Mp, zc + 2 * W:zc + 3 * W]),
            rwkv_S_p=jnp.swapaxes(sT_p, -1, -2), shift_p=z[Mp - 1:Mp, Z_A:Z_A + A_COLS], ret_S_p=rS_p,
            fox_k_s=hs(z[Mp:, zb + W:zb + 2 * W]), fox_v_s=hs(z[Mp:, zb + 2 * W:zb + 3 * W]),
            fox_lf_s=lf[Mp:, :NH].reshape(Bs, S, NH),
            moba_k_s=hs(z[Mp:, zc + W:zc + 2 * W]), moba_v_s=hs(z[Mp:, zc + 2 * W:zc + 3 * W]),
            rwkv_S_s=jnp.swapaxes(sT_s, -1, -2),
            shift_s=z[Mp:, Z_A:Z_A + A_COLS].reshape(Bs, S, A_COLS)[:, S - 1], ret_S_s=rS_s))

    st = lambda name: jnp.stack([p[name] for p in per_layer])
    return (x[:Mp].reshape(Bp, Tp, D), x[Mp:].reshape(Bs, S, D),
            st("fox_k_p"), st("fox_v_p"), st("fox_lf_p"), st("moba_k_p"), st("moba_v_p"),
            st("rwkv_S_p"), st("shift_p"), st("ret_S_p"),
            st("fox_k_s"), st("fox_v_s"), st("fox_lf_s"), st("moba_k_s"), st("moba_v_s"),
            st("rwkv_S_s"), st("shift_s"), st("ret_S_s"))
```

```python
import functools

import numpy as np
import jax
import jax.numpy as jnp
from jax import lax
from jax.experimental import pallas as pl
from jax.experimental.pallas import tpu as pltpu

F32 = jnp.float32
BF16 = jnp.bfloat16

D_MODEL = 2048
DEPTH = 2
PAGE_SIZE = 128
BRANCH_W = 512
A_HD = 64
A_HEADS = 8
A_LORA_W = 96
A_LORA_A = 96
A_LORA_G = 256
A_COLS = 3 * BRANCH_W + A_LORA_W + A_LORA_A + A_LORA_G
B_COLS = 3 * BRANCH_W + 4
C_COLS = 3 * BRANCH_W
D_COLS = 4 * BRANCH_W
HD = 128
NH = 4
RWKV_GN_EPS = 64e-5
RET_GN_EPS = 1e-6
LN_EPS = 1e-5
MOBA_BLOCK = 256
MOBA_TOPK = 3
RET_CHUNK = 128
RWKV_CHUNK = 64
ROPE_BASE = 10000.0
FFN_HIDDEN = 5632
ALPHA = (2 * DEPTH) ** 0.25

Z_G = 0
Z_A = 4 * D_MODEL
Z_D = Z_A + 2048
Z_C = Z_D + 2048
Z_B = Z_C + 1536
Z_COLS = Z_B + 2048

TM = 256
FLASH_TQ = 512
SAMPLE_PAGES = 8
VROWS = HD + 16
LANES = 128
NEG = -1e30
LOG2E = 1.4426950408889634
VMEM_LIMIT = 56 << 20

NN = (((1,), (0,)), ((), ()))
NT = (((1,), (1,)), ((), ()))
TN = (((0,), (0,)), ((), ()))


def _cp(sem):
    return pltpu.CompilerParams(dimension_semantics=sem, vmem_limit_bytes=VMEM_LIMIT)


def _dot(a, b, dims=NN):
    return lax.dot_general(a, b, dims, preferred_element_type=F32)


def _split2(x):
    hi = x.astype(BF16)
    return hi, (x - hi.astype(F32)).astype(BF16)


def _split3(x):
    hi = x.astype(BF16)
    r1 = x - hi.astype(F32)
    mid = r1.astype(BF16)
    return hi, mid, (r1 - mid.astype(F32)).astype(BF16)


def _dot3(a, b, dims=NN):
    ah, al = _split2(a)
    bh, bl = _split2(b)
    return _dot(ah, bh, dims) + (_dot(ah, bl, dims) + _dot(al, bh, dims))


def _dot1(a, b, dims=NN):
    return _dot(a.astype(BF16), b.astype(BF16), dims)


def _dot_exact_l(m_bf16, x, n=3):
    parts = _split3(x) if n == 3 else _split2(x)
    acc = _dot(m_bf16, parts[0])
    for p in parts[1:]:
        acc = acc + _dot(m_bf16, p)
    return acc


def _dot_exact_r(x, m_bf16, n=3):
    parts = _split3(x) if n == 3 else _split2(x)
    acc = _dot(parts[0], m_bf16)
    for p in parts[1:]:
        acc = acc + _dot(p, m_bf16)
    return acc


def _ln(x, eps):
    mu = jnp.mean(x, axis=-1, keepdims=True)
    xc = x - mu
    var = jnp.mean(xc * xc, axis=-1, keepdims=True)
    return xc * lax.rsqrt(var + eps)


def _sigmoid(x):
    return 1.0 / (1.0 + jnp.exp(-x))


def _log_sigmoid(x):
    return jnp.minimum(x, 0.0) - jnp.log(1.0 + jnp.exp(-jnp.abs(x)))


def _iota(shape, dim):
    return lax.broadcasted_iota(jnp.int32, shape, dim)


def _ada_kernel(c_ref, w_ref, b_ref, o_ref):
    c = c_ref[...]
    s = (c * _sigmoid(c)).astype(BF16)
    o_ref[...] = _dot(s, w_ref[...].astype(BF16)) + b_ref[...]


def _ada_call(c_all, w_ada, b_ada, l):
    R, D = c_all.shape
    N = w_ada.shape[2]
    tn = 2048
    return pl.pallas_call(
        _ada_kernel, grid=(N // tn,),
        in_specs=[pl.BlockSpec((R, D), lambda j: (0, 0)),
                  pl.BlockSpec((None, D, tn), lambda j: (l, 0, j)),
                  pl.BlockSpec((None, 1, tn), lambda j: (l, 0, j))],
        out_specs=pl.BlockSpec((R, tn), lambda j: (0, j)),
        out_shape=jax.ShapeDtypeStruct((R, N), F32),
        compiler_params=_cp(("arbitrary",)), name="ada")(c_all, w_ada, b_ada)


def _lnmod_kernel(x_ref, scp_ref, shp_ref, scs_ref, shs_ref, o_ref, *, n_p):
    is_s = pl.program_id(0) >= n_p
    sc = jnp.where(is_s, scs_ref[...], scp_ref[0:1, :])
    sh = jnp.where(is_s, shs_ref[...], shp_ref[0:1, :])
    o_ref[...] = (_ln(x_ref[...], LN_EPS) * (1.0 + sc) + sh).astype(BF16)


def _mod_specs(n_p, cols):
    specs = []
    for cb in cols:
        specs.append(pl.BlockSpec((8, D_MODEL), lambda i, cb=cb: (0, cb)))
    for cb in cols:
        specs.append(pl.BlockSpec((TM, D_MODEL), lambda i, cb=cb: (jnp.maximum(i - n_p, 0), cb)))
    return specs


def _lnmod_call(x, ada_p, mod_s, n_p, sc_col, sh_col):
    M, D = x.shape
    return pl.pallas_call(
        functools.partial(_lnmod_kernel, n_p=n_p), grid=(M // TM,),
        in_specs=[pl.BlockSpec((TM, D), lambda i: (i, 0))] + _mod_specs(n_p, (sc_col, sh_col)),
        out_specs=pl.BlockSpec((TM, D), lambda i: (i, 0)),
        out_shape=jax.ShapeDtypeStruct((M, D), BF16),
        compiler_params=_cp(("arbitrary",)), name="lnmod")(x, ada_p, ada_p, mod_s, mod_s)


def _mm_kernel(a_ref, w_ref, o_ref, wb_ref):
    @pl.when(pl.program_id(1) == 0)
    def _():
        wb_ref[...] = w_ref[...].astype(BF16)

    o_ref[...] = _dot(a_ref[...].astype(BF16), wb_ref[...]).astype(o_ref.dtype)


def _win_kernel(st_ref, a_ref, w_ref, o_ref, wb_ref, *, l):
    @pl.when(pl.program_id(1) == 0)
    def _():
        wb_ref[...] = w_ref[:, l, :].T.astype(BF16)

    o_ref[...] = _dot(a_ref[...], wb_ref[...])


W_IN_TN = 512
_O_B, _O_C, _O_D = A_COLS, A_COLS + B_COLS, A_COLS + B_COLS + C_COLS
_O_G = _O_D + D_COLS
_W_IN_TILE_SRC = np.concatenate([src + W_IN_TN * np.arange(width // W_IN_TN) for src, width in (
    (_O_G, 4 * D_MODEL), (0, Z_D - Z_A), (_O_D, Z_C - Z_D), (_O_C, Z_B - Z_C), (_O_B, Z_COLS - Z_B))]).astype(np.int32)


def _win_call(a, w_in, l, tm):
    M, K = a.shape
    depth = w_in.shape[0]
    w_t = jnp.transpose(w_in, (2, 0, 1))
    tn = W_IN_TN
    nt = Z_COLS // tn
    assert int(_W_IN_TILE_SRC.max()) + tn <= w_in.shape[2] and len(_W_IN_TILE_SRC) == nt
    return pl.pallas_call(
        functools.partial(_win_kernel, l=l),
        grid_spec=pltpu.PrefetchScalarGridSpec(
            num_scalar_prefetch=1, grid=(nt, M // tm),
            in_specs=[pl.BlockSpec((tm, K), lambda j, i, st: (i, 0)),
                      pl.BlockSpec((pl.Element(tn), pl.Element(depth), pl.Element(K)),
                                   lambda j, i, st: (st[j], 0, 0))],
            out_specs=pl.BlockSpec((tm, tn), lambda j, i, st: (i, j)),
            scratch_shapes=[pltpu.VMEM((K, tn), BF16)]),
        out_shape=jax.ShapeDtypeStruct((M, Z_COLS), F32),
        compiler_params=_cp(("arbitrary", "arbitrary")), name="w_in")(jnp.asarray(_W_IN_TILE_SRC), a, w_t)


def _mm_call(a, w, w_index, tm, tn, out_dtype, name):
    M, K = a.shape
    N = w.shape[-1]
    if w.ndim == 3:
        w_spec = pl.BlockSpec((None, K, tn), lambda j, i: (w_index, 0, j))
    else:
        w_spec = pl.BlockSpec((K, tn), lambda j, i: (0, j))
    return pl.pallas_call(
        _mm_kernel, grid=(N // tn, M // tm),
        in_specs=[pl.BlockSpec((tm, K), lambda j, i: (i, 0)), w_spec],
        out_specs=pl.BlockSpec((tm, tn), lambda j, i: (i, j)),
        out_shape=jax.ShapeDtypeStruct((M, N), out_dtype),
        scratch_shapes=[pltpu.VMEM((K, tn), BF16)],
        compiler_params=_cp(("arbitrary", "arbitrary")), name=name)(a, w)


def _resid_kernel(*refs, n_p, with_h):
    if with_h:
        (x_ref, mix_ref, gp_ref, scp_ref, shp_ref, gs_ref, scs_ref, shs_ref, lg_ref, lb_ref,
         x1_ref, h_ref) = refs
    else:
        x_ref, mix_ref, gp_ref, gs_ref, lg_ref, lb_ref, x1_ref = refs
    is_s = pl.program_id(0) >= n_p
    g = jnp.where(is_s, gs_ref[...], gp_ref[0:1, :])
    y = ALPHA * x_ref[...] + (1.0 + g) * mix_ref[...]
    x1 = _ln(y, LN_EPS) * lg_ref[...] + lb_ref[...]
    x1_ref[...] = x1
    if with_h:
        sc = jnp.where(is_s, scs_ref[...], scp_ref[0:1, :])
        sh = jnp.where(is_s, shs_ref[...], shp_ref[0:1, :])
        h_ref[...] = (_ln(x1, LN_EPS) * (1.0 + sc) + sh).astype(BF16)


def _resid_call(x, mix, ada_p, mod_s, ln_g, ln_b, l, n_p, g_col, sc_col=None, sh_col=None):
    M, D = x.shape
    with_h = sc_col is not None
    cols = (g_col, sc_col, sh_col) if with_h else (g_col,)
    row = pl.BlockSpec((TM, D), lambda i: (i, 0))
    vec = pl.BlockSpec((None, 1, D), lambda i: (l, 0, 0))
    in_specs = [row, row] + _mod_specs(n_p, cols) + [vec, vec]
    args = [x, mix] + [ada_p] * len(cols) + [mod_s] * len(cols) + [ln_g, ln_b]
    out_shape = [jax.ShapeDtypeStruct((M, D), F32)]
    out_specs = [row]
    if with_h:
        out_shape.append(jax.ShapeDtypeStruct((M, D), BF16))
        out_specs.append(row)
    res = pl.pallas_call(
        functools.partial(_resid_kernel, n_p=n_p, with_h=with_h), grid=(M // TM,),
        in_specs=in_specs, out_specs=out_specs, out_shape=out_shape,
        compiler_params=_cp(("arbitrary",)), name="resid_h" if with_h else "resid")(*args)
    return res if with_h else res[0]


def _ffn_kernel(h_ref, wg_ref, wi_ref, o_ref, wgb_ref, wib_ref):
    @pl.when(pl.program_id(1) == 0)
    def _():
        wgb_ref[...] = wg_ref[...].astype(BF16)
        wib_ref[...] = wi_ref[...].astype(BF16)

    h = h_ref[...]
    a = _dot(h, wgb_ref[...])
    b = _dot(h, wib_ref[...])
    o_ref[...] = (a * _sigmoid(a) * b).astype(BF16)


def _ffn_call(h, wg, wi, l, tm, tn):
    M, K = h.shape
    N = wg.shape[-1]
    w_spec = pl.BlockSpec((None, K, tn), lambda j, i: (l, 0, j))
    return pl.pallas_call(
        _ffn_kernel, grid=(N // tn, M // tm),
        in_specs=[pl.BlockSpec((tm, K), lambda j, i: (i, 0)), w_spec, w_spec],
        out_specs=pl.BlockSpec((tm, tn), lambda j, i: (i, j)),
        out_shape=jax.ShapeDtypeStruct((M, N), BF16),
        scratch_shapes=[pltpu.VMEM((K, tn), BF16), pltpu.VMEM((K, tn), BF16)],
        compiler_params=_cp(("arbitrary", "arbitrary")), name="ffn_act")(h, wg, wi)


def _merge_kernel(ya_ref, yb_ref, yc_ref, yd_ref, wb_ref, g0_ref, g1_ref, g2_ref, g3_ref, o_ref, wbb_ref):
    @pl.when(pl.program_id(1) == 0)
    def _():
        wbb_ref[...] = wb_ref[...].astype(BF16)

    acc = None
    for n, (y_ref, g_ref) in enumerate(((ya_ref, g0_ref), (yb_ref, g1_ref), (yc_ref, g2_ref), (yd_ref, g3_ref))):
        up = _dot(y_ref[...].astype(BF16), wbb_ref[n])
        t = _sigmoid(g_ref[...]) * up
        acc = t if acc is None else acc + t
    o_ref[...] = acc.astype(BF16)


def _merge_call(ys, w_branch, z, l, tm, tn):
    M = z.shape[0]
    W = BRANCH_W
    D = D_MODEL
    nj = D // tn
    y_spec = pl.BlockSpec((tm, W), lambda j, i: (i, 0))
    g_specs = [pl.BlockSpec((tm, tn), lambda j, i, n=n: (i, (Z_G + n * D) // tn + j)) for n in range(4)]
    return pl.pallas_call(
        _merge_kernel, grid=(nj, M // tm),
        in_specs=[y_spec] * 4 + [pl.BlockSpec((None, 4, W, tn), lambda j, i: (l, 0, 0, j))] + g_specs,
        out_specs=pl.BlockSpec((tm, tn), lambda j, i: (i, j)),
        out_shape=jax.ShapeDtypeStruct((M, D), BF16),
        scratch_shapes=[pltpu.VMEM((4, W, tn), BF16)],
        compiler_params=_cp(("arbitrary", "arbitrary")), name="merge")(*ys, w_branch, z, z, z, z)


def _rwkv_pre_kernel(za_ref, prev_ref, fill_ref, mu_ref, w0_ref, a0_ref, kk_ref, ka_ref, w2_ref, a2_ref,
                     g2_ref, e_ref, r_o, lw_o, k_o, v_o, kn_o, b_o, g_o, *, n_p, seq_s):
    i = pl.program_id(0)
    is_s = i >= n_p
    za = za_ref[...]
    rows = _iota(za.shape, 0)
    prev = jnp.where(rows == 0, prev_ref[7:8, :], pltpu.roll(za, 1, axis=0))
    base = jnp.where(is_s, n_p * TM, 0)
    pmask = jnp.where(is_s, seq_s - 1, 0x3FFFFFFF)
    start = ((rows + (i * TM - base)) & pmask) == 0
    fill = jnp.where(is_s, fill_ref[...], 0.0)
    prev = jnp.where(start, fill, prev)
    zm = za + mu_ref[...] * (prev - za)
    W = BRANCH_W
    r = zm[:, 0:W]
    k = zm[:, W:2 * W]
    v = zm[:, 2 * W:3 * W]
    x = zm[:, 3 * W:4 * W]
    wl = _dot(jnp.tanh(x).astype(BF16), w2_ref[...].astype(BF16))
    y = -(w0_ref[...] + wl)
    softplus = jnp.maximum(y, 0.0) + jnp.log(1.0 + jnp.exp(-jnp.abs(y)))
    w = -softplus - 0.5
    logw = -jnp.exp(w)
    a = _sigmoid(a0_ref[...] + _dot(x.astype(BF16), a2_ref[...].astype(BF16)))
    g = _dot(_sigmoid(x).astype(BF16), g2_ref[...].astype(BF16))
    kk0 = k * kk_ref[...]
    ss = _dot_exact_r(kk0 * kk0, e_ref[...], n=3)
    kn = kk0 / jnp.maximum(jnp.sqrt(ss), 1e-12)
    k2 = k * (1.0 + (a - 1.0) * ka_ref[...])
    b = kn * a
    for h in range(A_HEADS):
        sl = slice(h * A_HD, (h + 1) * A_HD)
        r_o[h] = r[:, sl]
        lw_o[h] = logw[:, sl]
        k_o[h] = k2[:, sl]
        v_o[h] = v[:, sl]
        kn_o[h] = kn[:, sl]
        b_o[h] = b[:, sl]
        g_o[h] = g[:, sl]


def _rwkv_pre_call(z, fill_s, mu, w0, a0, k_k, k_a, w2p, a2p, g2p, e64, n_p, seq_s):
    M = z.shape[0]
    W = BRANCH_W
    cb = Z_A // 2048
    vec = lambda n: pl.BlockSpec((1, n), lambda i: (0, 0))
    mat = pl.BlockSpec((W, W), lambda i: (0, 0))
    out = jax.ShapeDtypeStruct((A_HEADS, M, A_HD), F32)
    ospec = pl.BlockSpec((A_HEADS, TM, A_HD), lambda i: (0, i, 0))
    return pl.pallas_call(
        functools.partial(_rwkv_pre_kernel, n_p=n_p, seq_s=seq_s), grid=(M // TM,),
        in_specs=[pl.BlockSpec((TM, 2048), lambda i: (i, cb)),
                  pl.BlockSpec((8, 2048), lambda i: (jnp.maximum(i * (TM // 8) - 1, 0), cb)),
                  pl.BlockSpec((TM, 2048), lambda i: (jnp.maximum(i - n_p, 0), 0)),
                  vec(2048), vec(W), vec(W), vec(W), vec(W), mat, mat, mat, mat],
        out_specs=[ospec] * 7, out_shape=[out] * 7,
        compiler_params=_cp(("arbitrary",)), name="rwkv_pre")(
            z, z, fill_s, mu, w0, a0, k_k, k_a, w2p, a2p, g2p, e64)


def _rwkv_chunk_kernel(r_ref, lw_ref, k_ref, v_ref, kn_ref, b_ref, g_ref, s0_ref, rk_ref, lng_ref, lnb_ref,
                       *rest, C, nc):
    y_ref, sfin_ref, S_ref = rest[-3:]
    c = pl.program_id(1)

    @pl.when(c == 0)
    def _():
        S_ref[...] = s0_ref[...]

    H = range(A_HEADS)
    row = _iota((C, C), 0)
    col = _iota((C, C), 1)
    tri = (row >= col).astype(BF16)
    eye = (row == col).astype(F32)
    row2 = _iota((2 * C, C), 0)
    col2 = _iota((2 * C, C), 1)
    low2 = jnp.where(row2 < C, row2 - 1, row2 - C) >= col2
    eye_k = _iota((A_HD, A_HD), 0) == _iota((A_HD, A_HD), 1)
    r = [r_ref[h] for h in H]
    lw = [lw_ref[h] for h in H]
    k = [k_ref[h] for h in H]
    v = [v_ref[h] for h in H]
    kn = [kn_ref[h] for h in H]
    b = [b_ref[h] for h in H]
    S0 = [S_ref[h] for h in H]
    L = [_dot_exact_l(tri, lw[h]) for h in H]
    eL = [jnp.exp(L[h]) for h in H]
    eN = [jnp.exp(-L[h]) for h in H]
    eE = [jnp.exp(L[h][C - 1:C, :] - L[h]) for h in H]
    lhs = [jnp.concatenate([kn[h] * jnp.exp(L[h] - lw[h]), r[h] * eL[h]], axis=0) for h in H]
    Ab = [jnp.where(low2, _dot1(lhs[h], b[h] * eN[h], NT), 0.0) for h in H]
    Ak = [jnp.where(low2, _dot1(lhs[h], k[h] * eN[h], NT), 0.0) for h in H]
    N = [Ab[h][0:C] for h in H]
    X = [eye - N[h] for h in H]
    P = [_dot1(N[h], N[h]) for h in H]
    n = 2
    while n < C:
        X = [X[h] + _dot1(X[h], P[h]) for h in H]
        n *= 2
        if n < C:
            P = [_dot1(P[h], P[h]) for h in H]
    AV = [_dot1(Ak[h], v[h]) for h in H]
    KS = [_dot1(lhs[h], S0[h]) for h in H]
    U = [_dot1(X[h], KS[h][0:C] + AV[h][0:C]) for h in H]
    Y = [KS[h][C:] + AV[h][C:] - _dot1(Ab[h][C:], U[h]) for h in H]
    gcol = [jnp.sum(jnp.where(eye_k, jnp.broadcast_to(eL[h][C - 1:C, :], (A_HD, A_HD)), 0.0), axis=1, keepdims=True)
            for h in H]
    S1 = [S0[h] * gcol[h] + _dot1(jnp.concatenate([k[h] * eE[h], -(b[h] * eE[h])], axis=0),
                                  jnp.concatenate([v[h], U[h]], axis=0), TN) for h in H]
    out = []
    for h in H:
        yn = _ln(Y[h], RWKV_GN_EPS) * lng_ref[h] + lnb_ref[h]
        bonus = jnp.sum(r[h] * k[h] * rk_ref[h], axis=1, keepdims=True) * v[h]
        out.append((yn + bonus) * g_ref[h])
    y_ref[...] = jnp.concatenate(out, axis=1)
    for h in H:
        S_ref[h] = S1[h]

    @pl.when(c == nc - 1)
    def _():
        for h in H:
            sfin_ref[h] = S1[h]


def _rwkv_chunk_call(pre, s0t, r_k, ln_g, ln_b, row0, B, T, C, into):
    nc = T // C
    blk0 = row0 // C
    in_spec = pl.BlockSpec((A_HEADS, C, A_HD), lambda b, c: (0, blk0 + b * nc + c, 0))
    hvec = pl.BlockSpec((A_HEADS, 1, A_HD), lambda b, c: (0, 0, 0))
    st = pl.BlockSpec((None, A_HEADS, A_HD, A_HD), lambda b, c: (b, 0, 0, 0))
    return pl.pallas_call(
        functools.partial(_rwkv_chunk_kernel, C=C, nc=nc), grid=(B, nc),
        in_specs=[in_spec] * 7 + [st, hvec, hvec, hvec, pl.BlockSpec(memory_space=pl.ANY)],
        out_specs=[pl.BlockSpec((C, BRANCH_W), lambda b, c: (blk0 + b * nc + c, 0)), st],
        out_shape=[jax.ShapeDtypeStruct(into.shape, F32),
                   jax.ShapeDtypeStruct((B, A_HEADS, A_HD, A_HD), F32)],
        scratch_shapes=[pltpu.VMEM((A_HEADS, A_HD, A_HD), F32)],
        input_output_aliases={11: 0},
        compiler_params=_cp(("arbitrary", "arbitrary")), name="rwkv_chunk_c%d" % C)(
            *pre, s0t, r_k, ln_g, ln_b, into)


def _foxcum_kernel(fl_ref, bf_ref, lf_ref, cum_ref, carry_ref):
    @pl.when(pl.program_id(0) == 0)
    def _():
        carry_ref[...] = jnp.zeros_like(carry_ref)

    lane = _iota((TM, LANES), 1)
    lf = jnp.where(lane < NH, _log_sigmoid(fl_ref[...] + bf_ref[...]), 0.0)
    tri = (_iota((TM, TM), 0) >= _iota((TM, TM), 1)).astype(BF16)
    cum = _dot_exact_l(tri, lf) + carry_ref[...]
    lf_ref[...] = lf
    cum_ref[...] = cum
    carry_ref[...] = cum[TM - 1:TM, :]


def _foxcum_call(z, bf):
    M = z.shape[0]
    cb = (Z_B + 3 * BRANCH_W) // LANES
    spec = pl.BlockSpec((TM, LANES), lambda i: (i, 0))
    return pl.pallas_call(
        _foxcum_kernel, grid=(M // TM,),
        in_specs=[pl.BlockSpec((TM, LANES), lambda i: (i, cb)), pl.BlockSpec((1, LANES), lambda i: (0, 0))],
        out_specs=[spec, spec], out_shape=[jax.ShapeDtypeStruct((M, LANES), F32)] * 2,
        scratch_shapes=[pltpu.VMEM((1, LANES), F32)],
        compiler_params=_cp(("arbitrary",)), name="fox_cum")(z, bf)


def _attn_prep_kernel(fq_ref, fk_ref, fv_ref, mq_ref, mk_ref, mv_ref, cum_ref, cprev_ref, cqprev_ref, km_ref,
                      fqT_o, fkA_o, fvT_o, mqT_o, mk_o, mvT_o, sel_o, *, scale):
    i = pl.program_id(0)
    lane = _iota((TM, LANES), 1)
    cum = cum_ref[...]
    qscale = scale * LOG2E
    crel_k = (cum - jnp.where(i == 0, 0.0, cprev_ref[7:8, :])) * LOG2E
    crel_q = (cum - jnp.where(i < FLASH_TQ // TM, 0.0, cqprev_ref[7:8, :])) * LOG2E
    fq = fq_ref[...]
    fk = fk_ref[...]
    fv = fv_ref[...]
    mq = mq_ref[...]
    mk = mk_ref[...]
    mv = mv_ref[...]
    km = km_ref[...]
    blk = _iota((km.shape[0], TM), 0)
    blk_f = blk.astype(F32)
    ones_rows = jnp.where(_iota((VROWS - HD, TM), 0) == 0, 1.0, 0.0).astype(BF16)
    for h in range(NH):
        sl = slice(h * HD, (h + 1) * HD)
        q3 = [p.astype(F32) for p in _split3(jnp.broadcast_to(crel_q[:, h:h + 1], (TM, LANES)))]
        k3 = [p.astype(F32) for p in _split3(jnp.broadcast_to(crel_k[:, h:h + 1], (TM, LANES)))]
        qb = jnp.where(lane == 0, q3[0], jnp.where(lane == 1, q3[1], jnp.where(lane == 2, q3[2],
                                                                             jnp.where(lane < 6, 1.0, 0.0))))
        kb = jnp.where(lane < 3, 1.0, jnp.where(lane == 3, -k3[0], jnp.where(lane == 4, -k3[1],
                                                                              jnp.where(lane == 5, -k3[2], 0.0))))
        fqT_o[h, 0:HD, :] = (fq[:, sl] * qscale).T.astype(BF16)
        fqT_o[h, HD:2 * HD, :] = qb.T.astype(BF16)
        fkA_o[h, :, 0:HD] = fk[:, sl].astype(BF16)
        fkA_o[h, :, HD:2 * HD] = kb.astype(BF16)
        fvT_o[h, 0:HD, :] = fv[:, sl].T.astype(BF16)
        fvT_o[h, HD:VROWS, :] = ones_rows
        mqT = mq[:, sl].T
        mqT_o[h] = (mqT * qscale).astype(BF16)
        mk_o[h] = mk[:, sl].astype(BF16)
        mvT_o[h, 0:HD, :] = mv[:, sl].T.astype(BF16)
        mvT_o[h, HD:VROWS, :] = ones_rows
        sc = _dot3(km[:, sl], mqT)
        sel_o[h] = _top3_select(sc, blk < i, blk_f, axis=0)


def _attn_prep_call(z, cum, kmean, Tp):
    W = BRANCH_W
    nbp = kmean.shape[0]
    cbB = Z_B // W
    cbC = Z_C // W
    rq = FLASH_TQ // TM
    zs = lambda cb: pl.BlockSpec((TM, W), lambda i: (i, cb))
    colT = lambda rows: pl.BlockSpec((NH, rows, TM), lambda i: (0, 0, i))
    rowm = lambda cols: pl.BlockSpec((NH, TM, cols), lambda i: (0, i, 0))
    sd = jax.ShapeDtypeStruct
    return pl.pallas_call(
        functools.partial(_attn_prep_kernel, scale=HD ** -0.5), grid=(Tp // TM,),
        in_specs=[zs(cbB), zs(cbB + 1), zs(cbB + 2), zs(cbC), zs(cbC + 1), zs(cbC + 2),
                  pl.BlockSpec((TM, LANES), lambda i: (i, 0)),
                  pl.BlockSpec((8, LANES), lambda i: (jnp.maximum(i * (TM // 8) - 1, 0), 0)),
                  pl.BlockSpec((8, LANES), lambda i: (jnp.maximum((i // rq) * (FLASH_TQ // 8) - 1, 0), 0)),
                  pl.BlockSpec((nbp, W), lambda i: (0, 0))],
        out_specs=[colT(2 * HD), rowm(2 * HD), colT(VROWS), colT(HD), rowm(HD), colT(VROWS), colT(nbp)],
        out_shape=[sd((NH, 2 * HD, Tp), BF16), sd((NH, Tp, 2 * HD), BF16), sd((NH, VROWS, Tp), BF16),
                   sd((NH, HD, Tp), BF16), sd((NH, Tp, HD), BF16), sd((NH, VROWS, Tp), BF16),
                   sd((NH, nbp, Tp), F32)],
        compiler_params=_cp(("arbitrary",)), name="attn_prep")(z, z, z, z, z, z, cum, cum, cum, kmean)


def _flash_kernel(ti_ref, tj_ref, bq_ref, bk_ref, qT_ref, kA_ref, vT_ref, *rest, moba):
    sel_ref = rest[0] if moba else None
    o_ref, m_ref, acc_ref = rest[-3:]
    step = pl.program_id(0)
    i = ti_ref[step]
    j = tj_ref[step]
    T = kA_ref.shape[1]
    Tq = qT_ref.shape[2]
    ratio = Tq // T
    nqt = bq_ref.shape[0] // NH
    nkt = bk_ref.shape[0] // NH

    @pl.when(j == 0)
    def _():
        m_ref[...] = jnp.full(m_ref.shape, NEG, F32)
        acc_ref[...] = jnp.zeros(acc_ref.shape, F32)

    def step_body(on_diagonal):
        H = range(NH)
        sT = {h: _dot(kA_ref[h], qT_ref[h]) for h in H}
        if on_diagonal:
            causal = (_iota((T, Tq), 0) - _iota((T, Tq), 1)) <= (i * Tq - j * T)
            sT = {h: jnp.where(causal, sT[h], NEG) for h in H}
        if moba:
            own = (_iota((1, Tq), 1) // T + i * ratio) == j
            sT = {h: sT[h] + (jnp.where(own, 1.0, sel_ref[h, pl.ds(j, 1), :]) - 1.0) * (-NEG) for h in H}
            off = {h: 0.0 for h in H}
        else:
            off = {h: (bq_ref[h * nqt + i] - bk_ref[h * nkt + j]) * LOG2E for h in H}
        m_prev = {h: m_ref[h] for h in H}
        m_new = {h: jnp.maximum(m_prev[h], jnp.max(sT[h], axis=0, keepdims=True) + off[h]) for h in H}
        p = {h: jnp.exp2((sT[h] - (m_new[h] - off[h])).astype(BF16)) for h in H}
        alpha = {h: jnp.exp2(m_prev[h] - m_new[h]) for h in H}
        pv = {h: _dot(vT_ref[h], p[h]) for h in H}
        for h in H:
            acc_ref[h] = alpha[h] * acc_ref[h] + pv[h]
            m_ref[h] = m_new[h]

    crosses = (j + 1) * T - 1 > i * Tq
    pl.when(crosses)(functools.partial(step_body, True))
    pl.when(jnp.logical_not(crosses))(functools.partial(step_body, False))

    @pl.when(j == (i + 1) * ratio - 1)
    def _():
        for h in range(NH):
            acc = acc_ref[h]
            o_ref[:, h * HD:(h + 1) * HD] = (acc[0:HD] / acc[HD:HD + 1]).T


def _flash_call(qT, kA, vT, base_q, base_k, sel, Tp, into, name):
    T = MOBA_BLOCK
    Tq = FLASH_TQ
    ratio = Tq // T
    Kc = kA.shape[2]
    moba = sel is not None
    ti = np.concatenate([np.full((i + 1) * ratio, i, np.int32) for i in range(Tp // Tq)])
    tj = np.concatenate([np.arange((i + 1) * ratio, dtype=np.int32) for i in range(Tp // Tq)])
    in_specs = [pl.BlockSpec((NH, Kc, Tq), lambda s, ti, tj, bq, bk: (0, 0, ti[s])),
                pl.BlockSpec((NH, T, Kc), lambda s, ti, tj, bq, bk: (0, tj[s], 0)),
                pl.BlockSpec((NH, VROWS, T), lambda s, ti, tj, bq, bk: (0, 0, tj[s]))]
    args = [qT, kA, vT]
    if moba:
        in_specs.append(pl.BlockSpec((NH, sel.shape[1], Tq), lambda s, ti, tj, bq, bk: (0, 0, ti[s])))
        args.append(sel)
    in_specs.append(pl.BlockSpec(memory_space=pl.ANY))
    return pl.pallas_call(
        functools.partial(_flash_kernel, moba=moba),
        grid_spec=pltpu.PrefetchScalarGridSpec(
            num_scalar_prefetch=4, grid=(len(ti),), in_specs=in_specs,
            out_specs=pl.BlockSpec((Tq, NH * HD), lambda s, ti, tj, bq, bk: (ti[s], 0)),
            scratch_shapes=[pltpu.VMEM((NH, 1, Tq), F32), pltpu.VMEM((NH, VROWS, Tq), F32)]),
        out_shape=jax.ShapeDtypeStruct(into.shape, F32),
        input_output_aliases={4 + len(args): 0},
        compiler_params=_cp(("arbitrary",)), name=name)(
            jnp.asarray(ti), jnp.asarray(tj), base_q, base_k, *args, into)


def _top3_select(sc, valid, idx_f, axis=1):
    sc = jnp.where(valid, sc, -jnp.inf)
    sel = jnp.zeros(sc.shape, F32)
    for _ in range(MOBA_TOPK):
        mx = jnp.max(sc, axis=axis, keepdims=True)
        idx = jnp.min(jnp.where(sc == mx, idx_f, 1e9), axis=axis, keepdims=True)
        hit = idx_f == idx
        sel = jnp.where(hit & valid, 1.0, sel)
        sc = jnp.where(hit, -jnp.inf, sc)
    return sel


def _blockmean_kernel(k_ref, o_ref):
    o_ref[...] = jnp.sum(k_ref[...], axis=0, keepdims=True) * (1.0 / MOBA_BLOCK)


def _blockmean_call(z, Tp):
    nb = Tp // MOBA_BLOCK
    W = BRANCH_W
    cb = Z_C // W + 1
    return pl.pallas_call(
        _blockmean_kernel, grid=(nb,),
        in_specs=[pl.BlockSpec((MOBA_BLOCK, W), lambda i: (i, cb))],
        out_specs=pl.BlockSpec((None, 1, W), lambda i: (i, 0, 0)),
        out_shape=jax.ShapeDtypeStruct((nb, 1, W), F32),
        compiler_params=_cp(("arbitrary",)), name="moba_kmean")(z)


def _ret_kernel(q_ref, k_ref, v_ref, gd_ref, cos_ref, sin_ref, din_ref, qd_ref, kd_ref, cd_ref, s0_ref, lng_ref,
                *rest, nc):
    y_ref, sfin_ref, S_ref = rest[-3:]
    c = pl.program_id(1)

    @pl.when(c == 0)
    def _():
        S_ref[...] = s0_ref[...]

    q = q_ref[...]
    k = k_ref[...]
    v = v_ref[...]
    gd = gd_ref[...]
    cos = cos_ref[...]
    sin = sin_ref[...]
    lng = lng_ref[...]
    for h in range(NH):
        sl = slice(h * HD, (h + 1) * HD)
        qh = q[:, sl]
        kh = k[:, sl]
        qr = qh * cos + pltpu.roll(qh, HD // 2, axis=1) * sin
        kr = (kh * cos + pltpu.roll(kh, HD // 2, axis=1) * sin) * (HD ** -0.5)
        vb = v[:, sl].astype(BF16)
        qb = qr.astype(BF16)
        att = _dot(qb, kr.astype(BF16), NT) * din_ref[h]
        S = S_ref[h]
        o = _dot(att.astype(BF16), vb) + _dot(qb, S.astype(BF16)) * qd_ref[h]
        S_ref[h] = S * cd_ref[h] + _dot((kr * kd_ref[h]).astype(BF16), vb, TN)
        y_ref[:, sl] = _ln(o, RET_GN_EPS) * lng[:, sl] * (gd[:, sl] * _sigmoid(gd[:, sl]))

    @pl.when(c == nc - 1)
    def _():
        sfin_ref[...] = S_ref[...]


def _ret_tables(C):
    lg = np.log(1.0 - 2.0 ** (-5.0 - np.arange(NH, dtype=np.float32))).astype(np.float32)
    i = np.arange(C, dtype=np.float32)
    diff = i[:, None] - i[None, :]
    din = np.where(diff[None] >= 0, np.exp(np.maximum(diff, 0.0)[None] * lg[:, None, None]), 0.0)
    qd = np.exp((i[None, :] + 1.0) * lg[:, None])
    kd = np.exp((C - 1.0 - i)[None, :] * lg[:, None])
    cd = np.exp(C * lg)
    bc = lambda t: np.ascontiguousarray(np.broadcast_to(t[:, :, None], (NH, C, HD))).astype(np.float32)
    cdb = np.ascontiguousarray(np.broadcast_to(cd[:, None, None], (NH, 1, HD))).astype(np.float32)
    return din.astype(np.float32), bc(qd), bc(kd), cdb


def _rope_tables(pos0, T):
    half = HD // 2
    inv = 1.0 / (ROPE_BASE ** (jnp.arange(half, dtype=F32) / half))
    ang = (pos0 + jnp.arange(T)).astype(F32)[:, None] * inv[None, :]
    cos = jnp.cos(ang)
    sin = jnp.sin(ang)
    return jnp.concatenate([cos, cos], axis=1), jnp.concatenate([-sin, sin], axis=1)


def _ret_call(z, s0, ln_g, l, row0, B, T, C, pos0, into):
    nc = T // C
    W = BRANCH_W
    cb = Z_D // W
    blk0 = row0 // C
    cos, sin = _rope_tables(pos0, T)
    din, qd, kd, cd = _ret_tables(C)
    zs = lambda off: pl.BlockSpec((C, W), lambda b, c: (blk0 + b * nc + c, cb + off))
    tab = pl.BlockSpec((C, HD), lambda b, c: (c, 0))
    full = lambda shape: pl.BlockSpec(shape, lambda b, c: (0,) * len(shape))
    st = pl.BlockSpec((None, NH, HD, HD), lambda b, c: (b, 0, 0, 0))
    return pl.pallas_call(
        functools.partial(_ret_kernel, nc=nc), grid=(B, nc),
        in_specs=[zs(0), zs(1), zs(2), zs(3), tab, tab, full((NH, C, C)), full((NH, C, HD)), full((NH, C, HD)),
                  full((NH, 1, HD)), st, pl.BlockSpec((None, 1, W), lambda b, c: (l, 0, 0)),
                  pl.BlockSpec(memory_space=pl.ANY)],
        out_specs=[pl.BlockSpec((C, W), lambda b, c: (blk0 + b * nc + c, 0)), st],
        out_shape=[jax.ShapeDtypeStruct(into.shape, F32), jax.ShapeDtypeStruct((B, NH, HD, HD), F32)],
        scratch_shapes=[pltpu.VMEM((NH, HD, HD), F32)],
        input_output_aliases={12: 0},
        compiler_params=_cp(("arbitrary", "arbitrary")), name="ret_c%d" % C)(
            z, z, z, z, cos, sin, din, qd, kd, cd, s0, ln_g, into)


def _lfsuf_kernel(pt_ref, lf_hbm, ts_ref, tot_ref, o_ref, buf, sem, *, npg, base):
    b = pl.program_id(0)

    def page_copy(p):
        return pltpu.make_async_copy(lf_hbm.at[pl.ds(NH * (base + pt_ref[b * npg + p]), NH), :],
                                     buf.at[pl.ds(NH * p, NH), :], sem.at[p])

    for p in range(npg):
        page_copy(p).start()
    for p in range(npg):
        page_copy(p).wait()
    loc = None
    tot = None
    for h in range(NH):
        x = buf[pl.ds(h, npg, stride=NH), :]
        a = _dot_exact_r(x, ts_ref[h])
        t = _dot_exact_r(x, tot_ref[h])
        loc = a if loc is None else loc + a
        tot = t if tot is None else tot + t
    later = (_iota((npg, npg), 1) > _iota((npg, npg), 0)).astype(BF16)
    o_ref[...] = loc + _dot_exact_l(later, tot)


def _lfsuf_call(pt_flat, lf_rows, ts, tot, B, npg, base):
    PW = PAGE_SIZE * NH
    sel = pl.BlockSpec((NH, PAGE_SIZE, PW), lambda b, pt: (0, 0, 0))
    return pl.pallas_call(
        functools.partial(_lfsuf_kernel, npg=npg, base=base),
        grid_spec=pltpu.PrefetchScalarGridSpec(
            num_scalar_prefetch=1, grid=(B,),
            in_specs=[pl.BlockSpec(memory_space=pl.ANY), sel, sel],
            out_specs=pl.BlockSpec((None, npg, PW), lambda b, pt: (b, 0, 0)),
            scratch_shapes=[pltpu.VMEM((NH * npg, PAGE_SIZE), F32), pltpu.SemaphoreType.DMA((npg,))]),
        out_shape=jax.ShapeDtypeStruct((B, npg, PW), F32),
        compiler_params=_cp(("arbitrary",)), name="fox_logf_suffix")(pt_flat, lf_rows, ts, tot)


def _suffix_matrices():
    r = np.arange(PAGE_SIZE)
    dst_r = np.repeat(r, NH)
    dst_h = np.tile(np.arange(NH), PAGE_SIZE)
    same = np.arange(NH)[:, None, None] == dst_h[None, None, :]
    ts = same & (r[None, :, None] > dst_r[None, None, :])
    return jnp.asarray(ts, BF16), jnp.asarray(np.broadcast_to(same, ts.shape), BF16)


def _sattn_kernel(pt_ref, fq_ref, fkn_ref, fvn_ref, lfn_ref, mq_ref, mkn_ref, mvn_ref, R_ref, *rest, nb, S, scale):
    P = SAMPLE_PAGES
    fk, fv, mk, mv = (rest[t * P:(t + 1) * P] for t in range(4))
    yb_ref, yc_ref, fqb, mqb, mqf, cnb, fm, fl, facc, sc_all, m_all, l_all, o_all = rest[-13:]
    n = pl.program_id(1)
    R4 = NH * S
    PW = PAGE_SIZE * NH
    own_head = (_iota((R4, PW), 1) % NH) == (_iota((R4, PW), 0) // S)
    lane = _iota((R4, LANES), 1)

    def stack_heads(x):
        return jnp.concatenate([x[:, h * HD:(h + 1) * HD] for h in range(NH)], axis=0)

    def head_sums(kpage):
        return jnp.sum(kpage.reshape(PW // 8, 8, HD), axis=0)

    @pl.when(n == 0)
    def _():
        fqb[...] = stack_heads(fq_ref[...]).astype(BF16)
        mq = stack_heads(mq_ref[...])
        mqb[...] = mq.astype(BF16)
        mqf[...] = mq
        tri = (_iota((S, S), 0) >= _iota((S, S), 1)).astype(BF16)
        cn = _dot_exact_l(tri, lfn_ref[...])
        cnb[...] = jnp.concatenate([jnp.broadcast_to(cn[:, h:h + 1], (S, LANES)) for h in range(NH)], axis=0)
        fm[...] = jnp.full(fm.shape, NEG, F32)
        fl[...] = jnp.zeros(fl.shape, F32)
        facc[...] = jnp.zeros(facc.shape, F32)
        sc_all[...] = jnp.full(sc_all.shape, -jnp.inf, F32)
        m_all[...] = jnp.full(m_all.shape, NEG, F32)
        l_all[...] = jnp.zeros(l_all.shape, F32)

    def fox_update(s_list, v_list):
        m_prev = fm[...]
        m_new = m_prev
        for s in s_list:
            m_new = jnp.maximum(m_new, jnp.max(s, axis=1, keepdims=True))
        alpha = jnp.exp(m_prev - m_new)
        p_list = [jnp.exp(s - m_new) for s in s_list]
        l_new = alpha * fl[...]
        acc = alpha * facc[...]
        for p, v_bf in zip(p_list, v_list):
            l_new = l_new + jnp.sum(p, axis=1, keepdims=True)
            acc = acc + _dot(p.astype(BF16), v_bf)
        fl[...] = l_new
        facc[...] = acc
        fm[...] = m_new

    cn_b = cnb[...]
    cn_w = jnp.concatenate([cn_b] * NH, axis=1)
    fq_b = fqb[...]
    s_fox = [_dot(fq_b, fk[g][...].astype(BF16), NT) * scale for g in range(P)]
    s_fox = [jnp.where(own_head, s_fox[g] + cn_w + R_ref[pl.ds(P * n + g, 1), :], NEG) for g in range(P)]
    fox_update(s_fox, [fv[g][...].astype(BF16) for g in range(P)])

    mq_b = mqb[...]
    mq_f = mqf[...]
    kpg = [mk[g][...] for g in range(P)]
    s_mo = [jnp.where(own_head, _dot(mq_b, kpg[g].astype(BF16), NT) * scale, NEG) for g in range(P)]
    for blk in range(P // 2):
        g0, g1 = 2 * blk, 2 * blk + 1
        ks = head_sums(kpg[g0]) + head_sums(kpg[g1])
        kmean = (ks[0:NH] + ks[NH:2 * NH]) * (1.0 / MOBA_BLOCK)
        kmean = jnp.concatenate([jnp.broadcast_to(kmean[h:h + 1], (S, HD)) for h in range(NH)], axis=0)
        sc_col = jnp.sum(mq_f * kmean, axis=1, keepdims=True)
        m_b = jnp.maximum(jnp.max(s_mo[g0], axis=1, keepdims=True), jnp.max(s_mo[g1], axis=1, keepdims=True))
        p0 = jnp.exp(s_mo[g0] - m_b)
        p1 = jnp.exp(s_mo[g1] - m_b)
        here = lane == (P // 2) * n + blk
        sc_all[...] = jnp.where(here, sc_col, sc_all[...])
        m_all[...] = jnp.where(here, m_b, m_all[...])
        l_all[...] = jnp.where(here, jnp.sum(p0, axis=1, keepdims=True) + jnp.sum(p1, axis=1, keepdims=True),
                               l_all[...])
        o_all[(P // 2) * n + blk] = (_dot(p0.astype(BF16), mv[g0][...].astype(BF16))
                                     + _dot(p1.astype(BF16), mv[g1][...].astype(BF16)))

    @pl.when(n == pl.num_programs(1) - 1)
    def _():
        ri = _iota((R4, R4), 0)
        ci = _iota((R4, R4), 1)
        causal = ((ri // S) == (ci // S)) & ((ci % S) <= (ri % S))
        cn_col = cn_b[:, 0:1]
        cn_row = jnp.sum(jnp.where(ri == ci, jnp.broadcast_to(cn_col, (R4, R4)), 0.0), axis=0, keepdims=True)
        s = _dot(fqb[...], stack_heads(fkn_ref[...]).astype(BF16), NT) * scale
        s = s + (cn_col - cn_row)
        fox_update([jnp.where(causal, s, NEG)], [stack_heads(fvn_ref[...]).astype(BF16)])
        out = facc[...] / fl[...]
        for h in range(NH):
            yb_ref[:, h * HD:(h + 1) * HD] = out[h * S:(h + 1) * S, :]

        s = _dot(mqb[...], stack_heads(mkn_ref[...]).astype(BF16), NT) * scale
        s = jnp.where(causal, s, NEG)
        m_o = jnp.max(s, axis=1, keepdims=True)
        p = jnp.exp(s - m_o)
        l_o = jnp.sum(p, axis=1, keepdims=True)
        o_o = _dot(p.astype(BF16), stack_heads(mvn_ref[...]).astype(BF16))
        sel = _top3_select(sc_all[...], lane < nb, lane.astype(F32)) > 0.5
        m_past = m_all[...]
        m_tot = jnp.maximum(jnp.max(jnp.where(sel, m_past, NEG), axis=1, keepdims=True), m_o)
        w = jnp.where(sel, jnp.exp(m_past - m_tot), 0.0)
        w_o = jnp.exp(m_o - m_tot)
        den = jnp.sum(w * l_all[...], axis=1, keepdims=True) + w_o * l_o
        num = w_o * o_o
        for blk in range(nb):
            num = num + w[:, blk:blk + 1] * o_all[blk]
        out = num / den
        for h in range(NH):
            yc_ref[:, h * HD:(h + 1) * HD] = out[h * S:(h + 1) * S, :]


def _sattn_call(pt_flat, z, lf, R, caches, l, Mp, B, S, npg, n_pool, yb_into, yc_into):
    fk, fv, mk, mv = caches
    W = BRANCH_W
    PW = PAGE_SIZE * NH
    nb = npg // 2
    R4 = NH * S
    rb = Mp // S
    base = l * n_pool
    zrow = lambda cb: pl.BlockSpec((S, W), lambda b, n, pt: (rb + b, cb))
    P = SAMPLE_PAGES
    page = lambda g: pl.BlockSpec((PW, HD), lambda b, n, pt: (base + pt[b * npg + P * n + g], 0))
    pages = [page(g) for g in range(P)]
    cbB = Z_B // W
    cbC = Z_C // W
    in_specs = [zrow(cbB), zrow(cbB + 1), zrow(cbB + 2),
                pl.BlockSpec((S, LANES), lambda b, n, pt: (rb + b, 0)),
                zrow(cbC), zrow(cbC + 1), zrow(cbC + 2),
                pl.BlockSpec((None, npg, PW), lambda b, n, pt: (b, 0, 0))] + pages * 4
    in_specs += [pl.BlockSpec(memory_space=pl.ANY)] * 2
    n_in = 1 + len(in_specs)
    out_spec = pl.BlockSpec((S, W), lambda b, n, pt: (rb + b, 0))
    scratch = [pltpu.VMEM((R4, HD), BF16), pltpu.VMEM((R4, HD), BF16), pltpu.VMEM((R4, HD), F32),
               pltpu.VMEM((R4, LANES), F32),
               pltpu.VMEM((R4, 1), F32), pltpu.VMEM((R4, 1), F32), pltpu.VMEM((R4, HD), F32),
               pltpu.VMEM((R4, LANES), F32), pltpu.VMEM((R4, LANES), F32), pltpu.VMEM((R4, LANES), F32),
               pltpu.VMEM((nb, R4, HD), F32)]
    return pl.pallas_call(
        functools.partial(_sattn_kernel, nb=nb, S=S, scale=HD ** -0.5),
        grid_spec=pltpu.PrefetchScalarGridSpec(
            num_scalar_prefetch=1, grid=(B, npg // P), in_specs=in_specs,
            out_specs=[out_spec, out_spec], scratch_shapes=scratch),
        out_shape=[jax.ShapeDtypeStruct(yb_into.shape, F32), jax.ShapeDtypeStruct(yc_into.shape, F32)],
        input_output_aliases={n_in - 2: 0, n_in - 1: 1},
        compiler_params=_cp(("arbitrary", "arbitrary")), name="sample_attn")(
            pt_flat, z, z, z, lf, z, z, z, R, *([fk] * P + [fv] * P + [mk] * P + [mv] * P), yb_into, yc_into)


def _pick_tile(m, cands):
    for t in cands:
        if m % t == 0:
            return t
    raise ValueError("no row tile divides %d" % m)


def _pad_cols(x, n):
    return jnp.pad(x, ((0, 0), (0, n - x.shape[1])))


def kernel(x_prompt, x_sample, c_prompt, c_sample, page_table, cache_fox_k, cache_fox_v, cache_fox_logf, cache_moba_k, cache_moba_v, state_rwkv, state_rwkv_shift, state_ret, w_ada, b_ada, w_in, rwkv_mu, rwkv_w0, rwkv_w2, rwkv_a0, rwkv_a2, rwkv_g2, rwkv_k_k, rwkv_k_a, rwkv_r_k, rwkv_ln_g, rwkv_ln_b, fox_bf, ret_ln_g, w_branch, w_o, ln1_g, ln1_b, ln2_g, ln2_b, w_ffn_gate, w_ffn_in, w_ffn_out):
    Bp, Tp, D = x_prompt.shape
    Bs, S, _ = x_sample.shape
    depth = w_in.shape[0]
    n_pool = cache_fox_k.shape[1]
    npg = page_table.shape[1]
    past = npg * PAGE_SIZE
    Mp, Ms = Bp * Tp, Bs * S
    M = Mp + Ms
    assert Bp == 1 and D == D_MODEL and depth == DEPTH
    assert Mp % TM == 0 and Ms % TM == 0 and Tp % MOBA_BLOCK == 0 and Tp // MOBA_BLOCK <= LANES
    assert Tp % RWKV_CHUNK == 0 and Tp % RET_CHUNK == 0 and S == 8 and past % MOBA_BLOCK == 0
    assert TM == MOBA_BLOCK and 2 * NH == 8 and MOBA_BLOCK == 2 * PAGE_SIZE
    assert Tp % FLASH_TQ == 0 and FLASH_TQ % MOBA_BLOCK == 0
    assert npg % SAMPLE_PAGES == 0 and SAMPLE_PAGES % 2 == 0 and npg // 2 <= LANES
    n_p = Mp // TM
    W = BRANCH_W
    tm_big = _pick_tile(M, (768, 512, 256))
    tm_small = _pick_tile(M, (384, 256))

    x = jnp.concatenate([x_prompt.reshape(Mp, D), x_sample.reshape(Ms, D)], axis=0)
    c_rows = 8 * (-(-(Bp + Bs) // 8))
    c_all = jnp.pad(jnp.concatenate([c_prompt, c_sample], axis=0), ((0, c_rows - Bp - Bs), (0, 0)))
    pt_flat = page_table.reshape(-1).astype(jnp.int32)
    cache2 = lambda t: t.reshape(depth * n_pool * PAGE_SIZE * NH, HD)
    caches = (cache2(cache_fox_k), cache2(cache_fox_v), cache2(cache_moba_k), cache2(cache_moba_v))
    lf_rows = jnp.swapaxes(cache_fox_logf, 2, 3).reshape(depth * n_pool * NH, PAGE_SIZE)
    ts_m, tot_m = _suffix_matrices()
    e64 = jnp.asarray(np.kron(np.eye(A_HEADS), np.ones((A_HD, A_HD))), BF16)
    b_ada3 = b_ada.reshape(depth, 1, -1)
    vec3 = lambda t: t.reshape(depth, 1, -1)
    ln1_g3, ln1_b3, ln2_g3, ln2_b3, ret_g3 = map(vec3, (ln1_g, ln1_b, ln2_g, ln2_b, ret_ln_g))

    assert w_in.shape[2] == _O_G + 4 * D_MODEL
    tm_in = _pick_tile(M, (1408, 768, 512, 256))

    per_layer = []
    for l in range(depth):
        ada = _ada_call(c_all, w_ada, b_ada3, l)
        ada_p = ada[0:8]
        mod_s = jnp.repeat(ada[Bp:Bp + Bs], S, axis=0)
        h = _lnmod_call(x, ada_p, mod_s, n_p, 1, 0)
        z = _win_call(h, w_in, l, tm_in)

        fill_s = jnp.repeat(_pad_cols(state_rwkv_shift[l], 2048), S, axis=0)
        pad_rows = lambda w2, r0: jnp.zeros((W, W), F32).at[r0:r0 + w2.shape[0]].set(w2)
        pre = _rwkv_pre_call(
            z, fill_s, _pad_cols(rwkv_mu[l][None], 2048), rwkv_w0[l][None], rwkv_a0[l][None],
            rwkv_k_k[l][None], rwkv_k_a[l][None], pad_rows(rwkv_w2[l], 0), pad_rows(rwkv_a2[l], A_LORA_W),
            pad_rows(rwkv_g2[l], A_LORA_W + A_LORA_A), e64, n_p, S)
        hv = lambda t: t.reshape(A_HEADS, 1, A_HD)
        rk, lg, lb = hv(rwkv_r_k[l]), hv(rwkv_ln_g[l]), hv(rwkv_ln_b[l])
        blank = jnp.zeros((M, W), F32)
        ya, sT_p = _rwkv_chunk_call(pre, jnp.zeros((Bp, A_HEADS, A_HD, A_HD), F32), rk, lg, lb,
                                    0, Bp, Tp, RWKV_CHUNK, blank)
        ya, sT_s = _rwkv_chunk_call(pre, jnp.swapaxes(state_rwkv[l], -1, -2), rk, lg, lb, Mp, Bs, S, S, ya)

        lf, cum = _foxcum_call(z, _pad_cols(fox_bf[l][None], LANES))
        nbk = Tp // MOBA_BLOCK
        kmean = _blockmean_call(z, Tp).reshape(nbk, W)
        kmean = jnp.pad(kmean, ((0, 8 * (-(-nbk // 8)) - nbk), (0, 0)))
        fqT, fkA, fvT, mqT, mkb, mvT, sel = _attn_prep_call(z, cum, kmean, Tp)
        tile_base = lambda t: jnp.transpose(jnp.concatenate(
            [jnp.zeros((1, NH), F32), cum[t - 1:Mp - 1:t, :NH]], axis=0)).reshape(-1)
        base_q, base_k = tile_base(FLASH_TQ), tile_base(MOBA_BLOCK)
        yb = _flash_call(fqT, fkA, fvT, base_q, base_k, None, Tp, blank, "fox_prompt")
        yc = _flash_call(mqT, mkb, mvT, base_q, base_k, sel, Tp, blank, "moba_prompt")
        R = _lfsuf_call(pt_flat, lf_rows, ts_m, tot_m, Bs, npg, l * n_pool)
        yb, yc = _sattn_call(pt_flat, z, lf, R, caches, l, Mp, Bs, S, npg, n_pool, yb, yc)

        yd, rS_p = _ret_call(z, jnp.zeros((Bp, NH, HD, HD), F32), ret_g3, l, 0, Bp, Tp, RET_CHUNK, 0, blank)
        yd, rS_s = _ret_call(z, state_ret[l], ret_g3, l, Mp, Bs, S, S, past, yd)

        mixin = _merge_call((ya, yb, yc, yd), w_branch, z, l, tm_big, 512)
        mix = _mm_call(mixin, w_o, l, tm_big, 512, F32, "w_o")
        x1, h2 = _resid_call(x, mix, ada_p, mod_s, ln1_g3, ln1_b3, l, n_p, 2, 4, 3)
        act = _ffn_call(h2, w_ffn_gate, w_ffn_in, l, tm_big, 512)
        f = _mm_call(act, w_ffn_out, l, tm_small, 512, F32, "ffn_out")
        x = _resid_call(x1, f, ada_p, mod_s, ln2_g3, ln2_b3, l, n_p, 5)

        zb = Z_B
        zc = Z_C
        hp = lambda t: t.reshape(Bp, Tp, NH, HD)
        hs = lambda t: t.reshape(Bs, S, NH, HD)
        per_layer.append(dict(
            fox_k_p=hp(z[:Mp, zb + W:zb + 2 * W]), fox_v_p=hp(z[:Mp, zb + 2 * W:zb + 3 * W]),
            fox_lf_p=lf[:Mp, :NH].reshape(Bp, Tp, NH),
            moba_k_p=hp(z[:Mp, zc + W:zc + 2 * W]), moba_v_p=hp(z[:Mp, zc + 2 * W:zc + 3 * W]),
            rwkv_S_p=jnp.swapaxes(sT_p, -1, -2), shift_p=z[Mp - 1:Mp, Z_A:Z_A + A_COLS], ret_S_p=rS_p,
            fox_k_s=hs(z[Mp:, zb + W:zb + 2 * W]), fox_v_s=hs(z[Mp:, zb + 2 * W:zb + 3 * W]),
            fox_lf_s=lf[Mp:, :NH].reshape(Bs, S, NH),
            moba_k_s=hs(z[Mp:, zc + W:zc + 2 * W]), moba_v_s=hs(z[Mp:, zc + 2 * W:zc + 3 * W]),
            rwkv_S_s=jnp.swapaxes(sT_s, -1, -2),
            shift_s=z[Mp:, Z_A:Z_A + A_COLS].reshape(Bs, S, A_COLS)[:, S - 1], ret_S_s=rS_s))

    st = lambda name: jnp.stack([p[name] for p in per_layer])
    return (x[:Mp].reshape(Bp, Tp, D), x[Mp:].reshape(Bs, S, D),
            st("fox_k_p"), st("fox_v_p"), st("fox_lf_p"), st("moba_k_p"), st("moba_v_p"),
            st("rwkv_S_p"), st("shift_p"), st("ret_S_p"),
            st("fox_k_s"), st("fox_v_s"), st("fox_lf_s"), st("moba_k_s"), st("moba_v_s"),
            st("rwkv_S_s"), st("shift_s"), st("ret_S_s"))
```

```python
import functools

import numpy as np
import jax
import jax.numpy as jnp
from jax import lax
from jax.experimental import pallas as pl
from jax.experimental.pallas import tpu as pltpu

F32 = jnp.float32
BF16 = jnp.bfloat16

D_MODEL = 2048
DEPTH = 2
PAGE_SIZE = 128
BRANCH_W = 512
A_HD = 64
A_HEADS = 8
A_LORA_W = 96
A_LORA_A = 96
A_LORA_G = 256
A_COLS = 3 * BRANCH_W + A_LORA_W + A_LORA_A + A_LORA_G
B_COLS = 3 * BRANCH_W + 4
C_COLS = 3 * BRANCH_W
D_COLS = 4 * BRANCH_W
HD = 128
NH = 4
RWKV_GN_EPS = 64e-5
RET_GN_EPS = 1e-6
LN_EPS = 1e-5
MOBA_BLOCK = 256
MOBA_TOPK = 3
RET_CHUNK = 128
RWKV_CHUNK = 64
RWKV_CHUNKS_PER_STEP = 4
ROPE_BASE = 10000.0
FFN_HIDDEN = 5632
ALPHA = (2 * DEPTH) ** 0.25

Z_G = 0
Z_A = 4 * D_MODEL
Z_D = Z_A + 2048
Z_C = Z_D + 2048
Z_B = Z_C + 2048
Z_COLS = Z_B + 2048

TM = 256
FLASH_TQ = 512
SAMPLE_PAGES = 8
VROWS = HD + 16
LANES = 128
NEG = -1e30
LOG2E = 1.4426950408889634
VMEM_LIMIT = 56 << 20

NN = (((1,), (0,)), ((), ()))
NT = (((1,), (1,)), ((), ()))
TN = (((0,), (0,)), ((), ()))


def _cp(sem):
    return pltpu.CompilerParams(dimension_semantics=sem, vmem_limit_bytes=VMEM_LIMIT)


def _dot(a, b, dims=NN):
    return lax.dot_general(a, b, dims, preferred_element_type=F32)


def _split2(x):
    hi = x.astype(BF16)
    return hi, (x - hi.astype(F32)).astype(BF16)


def _split3(x):
    hi = x.astype(BF16)
    r1 = x - hi.astype(F32)
    mid = r1.astype(BF16)
    return hi, mid, (r1 - mid.astype(F32)).astype(BF16)


def _dot3(a, b, dims=NN):
    ah, al = _split2(a)
    bh, bl = _split2(b)
    return _dot(ah, bh, dims) + (_dot(ah, bl, dims) + _dot(al, bh, dims))


def _dot1(a, b, dims=NN):
    return _dot(a.astype(BF16), b.astype(BF16), dims)


def _dot_exact_l(m_bf16, x, n=3):
    parts = _split3(x) if n == 3 else _split2(x)
    acc = _dot(m_bf16, parts[0])
    for p in parts[1:]:
        acc = acc + _dot(m_bf16, p)
    return acc


def _dot_exact_r(x, m_bf16, n=3):
    parts = _split3(x) if n == 3 else _split2(x)
    acc = _dot(parts[0], m_bf16)
    for p in parts[1:]:
        acc = acc + _dot(p, m_bf16)
    return acc


def _ln(x, eps):
    mu = jnp.mean(x, axis=-1, keepdims=True)
    xc = x - mu
    var = jnp.mean(xc * xc, axis=-1, keepdims=True)
    return xc * lax.rsqrt(var + eps)


def _sigmoid(x):
    return 1.0 / (1.0 + jnp.exp(-x))


def _log_sigmoid(x):
    return jnp.minimum(x, 0.0) - jnp.log(1.0 + jnp.exp(-jnp.abs(x)))


def _iota(shape, dim):
    return lax.broadcasted_iota(jnp.int32, shape, dim)


def _ada_kernel(c_ref, w_ref, b_ref, o_ref):
    c = c_ref[...]
    s = (c * _sigmoid(c)).astype(BF16)
    o_ref[...] = _dot(s, w_ref[...].astype(BF16)) + b_ref[...]


def _ada_call(c_all, w_ada, b_ada, l):
    R, D = c_all.shape
    N = w_ada.shape[2]
    tn = 2048
    return pl.pallas_call(
        _ada_kernel, grid=(N // tn,),
        in_specs=[pl.BlockSpec((R, D), lambda j: (0, 0)),
                  pl.BlockSpec((None, D, tn), lambda j: (l, 0, j)),
                  pl.BlockSpec((None, 1, tn), lambda j: (l, 0, j))],
        out_specs=pl.BlockSpec((R, tn), lambda j: (0, j)),
        out_shape=jax.ShapeDtypeStruct((R, N), F32),
        compiler_params=_cp(("arbitrary",)), name="ada")(c_all, w_ada, b_ada)


def _lnmod_kernel(x_ref, scp_ref, shp_ref, scs_ref, shs_ref, o_ref, *, n_p):
    is_s = pl.program_id(0) >= n_p
    sc = jnp.where(is_s, scs_ref[...], scp_ref[0:1, :])
    sh = jnp.where(is_s, shs_ref[...], shp_ref[0:1, :])
    o_ref[...] = (_ln(x_ref[...], LN_EPS) * (1.0 + sc) + sh).astype(BF16)


def _mod_specs(n_p, cols):
    specs = []
    for cb in cols:
        specs.append(pl.BlockSpec((8, D_MODEL), lambda i, cb=cb: (0, cb)))
    for cb in cols:
        specs.append(pl.BlockSpec((TM, D_MODEL), lambda i, cb=cb: (jnp.maximum(i - n_p, 0), cb)))
    return specs


def _lnmod_call(x, ada_p, mod_s, n_p, sc_col, sh_col):
    M, D = x.shape
    return pl.pallas_call(
        functools.partial(_lnmod_kernel, n_p=n_p), grid=(M // TM,),
        in_specs=[pl.BlockSpec((TM, D), lambda i: (i, 0))] + _mod_specs(n_p, (sc_col, sh_col)),
        out_specs=pl.BlockSpec((TM, D), lambda i: (i, 0)),
        out_shape=jax.ShapeDtypeStruct((M, D), BF16),
        compiler_params=_cp(("arbitrary",)), name="lnmod")(x, ada_p, ada_p, mod_s, mod_s)


def _mm_kernel(a_ref, w_ref, o_ref, wb_ref):
    @pl.when(pl.program_id(1) == 0)
    def _():
        wb_ref[...] = w_ref[...].astype(BF16)

    o_ref[...] = _dot(a_ref[...].astype(BF16), wb_ref[...]).astype(o_ref.dtype)


def _win_kernel(st_ref, a_ref, w_ref, o_ref, wb_ref, *, l):
    @pl.when(pl.program_id(1) == 0)
    def _():
        wb_ref[...] = w_ref[:, l, :].T.astype(BF16)

    o_ref[...] = _dot(a_ref[...], wb_ref[...])


W_IN_TN = 1024
_O_B, _O_C, _O_D = A_COLS, A_COLS + B_COLS, A_COLS + B_COLS + C_COLS
_O_G = _O_D + D_COLS
_W_IN_TILE_SRC = np.concatenate([src + W_IN_TN * np.arange(width // W_IN_TN) for src, width in (
    (_O_G, 4 * D_MODEL), (0, Z_D - Z_A), (_O_D, Z_C - Z_D), (_O_C, Z_B - Z_C), (_O_B, Z_COLS - Z_B))]).astype(np.int32)


def _win_call(a, w_in, l, tm):
    M, K = a.shape
    depth = w_in.shape[0]
    w_t = jnp.transpose(w_in, (2, 0, 1))
    tn = W_IN_TN
    nt = Z_COLS // tn
    assert int(_W_IN_TILE_SRC.max()) + tn <= w_in.shape[2] and len(_W_IN_TILE_SRC) == nt
    return pl.pallas_call(
        functools.partial(_win_kernel, l=l),
        grid_spec=pltpu.PrefetchScalarGridSpec(
            num_scalar_prefetch=1, grid=(nt, M // tm),
            in_specs=[pl.BlockSpec((tm, K), lambda j, i, st: (i, 0)),
                      pl.BlockSpec((pl.Element(tn), pl.Element(depth), pl.Element(K)),
                                   lambda j, i, st: (st[j], 0, 0))],
            out_specs=pl.BlockSpec((tm, tn), lambda j, i, st: (i, j)),
            scratch_shapes=[pltpu.VMEM((K, tn), BF16)]),
        out_shape=jax.ShapeDtypeStruct((M, Z_COLS), F32),
        compiler_params=_cp(("arbitrary", "arbitrary")), name="w_in")(jnp.asarray(_W_IN_TILE_SRC), a, w_t)


def _mm_call(a, w, w_index, tm, tn, out_dtype, name):
    M, K = a.shape
    N = w.shape[-1]
    if w.ndim == 3:
        w_spec = pl.BlockSpec((None, K, tn), lambda j, i: (w_index, 0, j))
    else:
        w_spec = pl.BlockSpec((K, tn), lambda j, i: (0, j))
    return pl.pallas_call(
        _mm_kernel, grid=(N // tn, M // tm),
        in_specs=[pl.BlockSpec((tm, K), lambda j, i: (i, 0)), w_spec],
        out_specs=pl.BlockSpec((tm, tn), lambda j, i: (i, j)),
        out_shape=jax.ShapeDtypeStruct((M, N), out_dtype),
        scratch_shapes=[pltpu.VMEM((K, tn), BF16)],
        compiler_params=_cp(("arbitrary", "arbitrary")), name=name)(a, w)


def _resid_kernel(*refs, n_p, with_h):
    if with_h:
        (x_ref, mix_ref, gp_ref, scp_ref, shp_ref, gs_ref, scs_ref, shs_ref, lg_ref, lb_ref,
         x1_ref, h_ref) = refs
    else:
        x_ref, mix_ref, gp_ref, gs_ref, lg_ref, lb_ref, x1_ref = refs
    is_s = pl.program_id(0) >= n_p
    g = jnp.where(is_s, gs_ref[...], gp_ref[0:1, :])
    y = ALPHA * x_ref[...] + (1.0 + g) * mix_ref[...]
    x1 = _ln(y, LN_EPS) * lg_ref[...] + lb_ref[...]
    x1_ref[...] = x1
    if with_h:
        sc = jnp.where(is_s, scs_ref[...], scp_ref[0:1, :])
        sh = jnp.where(is_s, shs_ref[...], shp_ref[0:1, :])
        h_ref[...] = (_ln(x1, LN_EPS) * (1.0 + sc) + sh).astype(BF16)


def _resid_call(x, mix, ada_p, mod_s, ln_g, ln_b, l, n_p, g_col, sc_col=None, sh_col=None):
    M, D = x.shape
    with_h = sc_col is not None
    cols = (g_col, sc_col, sh_col) if with_h else (g_col,)
    row = pl.BlockSpec((TM, D), lambda i: (i, 0))
    vec = pl.BlockSpec((None, 1, D), lambda i: (l, 0, 0))
    in_specs = [row, row] + _mod_specs(n_p, cols) + [vec, vec]
    args = [x, mix] + [ada_p] * len(cols) + [mod_s] * len(cols) + [ln_g, ln_b]
    out_shape = [jax.ShapeDtypeStruct((M, D), F32)]
    out_specs = [row]
    if with_h:
        out_shape.append(jax.ShapeDtypeStruct((M, D), BF16))
        out_specs.append(row)
    res = pl.pallas_call(
        functools.partial(_resid_kernel, n_p=n_p, with_h=with_h), grid=(M // TM,),
        in_specs=in_specs, out_specs=out_specs, out_shape=out_shape,
        compiler_params=_cp(("arbitrary",)), name="resid_h" if with_h else "resid")(*args)
    return res if with_h else res[0]


def _ffn_kernel(h_ref, wg_ref, wi_ref, o_ref, wgb_ref, wib_ref):
    @pl.when(pl.program_id(1) == 0)
    def _():
        wgb_ref[...] = wg_ref[...].astype(BF16)
        wib_ref[...] = wi_ref[...].astype(BF16)

    h = h_ref[...]
    a = _dot(h, wgb_ref[...])
    b = _dot(h, wib_ref[...])
    o_ref[...] = (a * _sigmoid(a) * b).astype(BF16)


def _ffn_call(h, wg, wi, l, tm, tn):
    M, K = h.shape
    N = wg.shape[-1]
    w_spec = pl.BlockSpec((None, K, tn), lambda j, i: (l, 0, j))
    return pl.pallas_call(
        _ffn_kernel, grid=(N // tn, M // tm),
        in_specs=[pl.BlockSpec((tm, K), lambda j, i: (i, 0)), w_spec, w_spec],
        out_specs=pl.BlockSpec((tm, tn), lambda j, i: (i, j)),
        out_shape=jax.ShapeDtypeStruct((M, N), BF16),
        scratch_shapes=[pltpu.VMEM((K, tn), BF16), pltpu.VMEM((K, tn), BF16)],
        compiler_params=_cp(("arbitrary", "arbitrary")), name="ffn_act")(h, wg, wi)


def _merge_kernel(ya_ref, yb_ref, yc_ref, yd_ref, wb_ref, g0_ref, g1_ref, g2_ref, g3_ref, o_ref, wbb_ref):
    @pl.when(pl.program_id(1) == 0)
    def _():
        wbb_ref[...] = wb_ref[...].astype(BF16)

    acc = None
    for n, (y_ref, g_ref) in enumerate(((ya_ref, g0_ref), (yb_ref, g1_ref), (yc_ref, g2_ref), (yd_ref, g3_ref))):
        up = _dot(y_ref[...].astype(BF16), wbb_ref[n])
        t = _sigmoid(g_ref[...]) * up
        acc = t if acc is None else acc + t
    o_ref[...] = acc.astype(BF16)


def _merge_call(ys, w_branch, z, l, tm, tn):
    M = z.shape[0]
    W = BRANCH_W
    D = D_MODEL
    nj = D // tn
    y_spec = pl.BlockSpec((tm, W), lambda j, i: (i, 0))
    g_specs = [pl.BlockSpec((tm, tn), lambda j, i, n=n: (i, (Z_G + n * D) // tn + j)) for n in range(4)]
    return pl.pallas_call(
        _merge_kernel, grid=(nj, M // tm),
        in_specs=[y_spec] * 4 + [pl.BlockSpec((None, 4, W, tn), lambda j, i: (l, 0, 0, j))] + g_specs,
        out_specs=pl.BlockSpec((tm, tn), lambda j, i: (i, j)),
        out_shape=jax.ShapeDtypeStruct((M, D), BF16),
        scratch_shapes=[pltpu.VMEM((4, W, tn), BF16)],
        compiler_params=_cp(("arbitrary", "arbitrary")), name="merge")(*ys, w_branch, z, z, z, z)


def _rwkv_pre_kernel(za_ref, prev_ref, fill_ref, mu_ref, w0_ref, a0_ref, kk_ref, ka_ref, w2_ref, a2_ref,
                     g2_ref, e_ref, r_o, lw_o, k_o, v_o, kn_o, b_o, g_o, *, n_p, seq_s):
    i = pl.program_id(0)
    is_s = i >= n_p
    za = za_ref[...]
    rows = _iota(za.shape, 0)
    prev = jnp.where(rows == 0, prev_ref[7:8, :], pltpu.roll(za, 1, axis=0))
    base = jnp.where(is_s, n_p * TM, 0)
    pmask = jnp.where(is_s, seq_s - 1, 0x3FFFFFFF)
    start = ((rows + (i * TM - base)) & pmask) == 0
    fill = jnp.where(is_s, fill_ref[...], 0.0)
    prev = jnp.where(start, fill, prev)
    zm = za + mu_ref[...] * (prev - za)
    W = BRANCH_W
    r = zm[:, 0:W]
    k = zm[:, W:2 * W]
    v = zm[:, 2 * W:3 * W]
    x = zm[:, 3 * W:4 * W]
    wl = _dot(jnp.tanh(x).astype(BF16), w2_ref[...].astype(BF16))
    y = -(w0_ref[...] + wl)
    softplus = jnp.maximum(y, 0.0) + jnp.log(1.0 + jnp.exp(-jnp.abs(y)))
    w = -softplus - 0.5
    logw = -jnp.exp(w)
    a = _sigmoid(a0_ref[...] + _dot(x.astype(BF16), a2_ref[...].astype(BF16)))
    g = _dot(_sigmoid(x).astype(BF16), g2_ref[...].astype(BF16))
    kk0 = k * kk_ref[...]
    ss = _dot_exact_r(kk0 * kk0, e_ref[...], n=3)
    kn = kk0 / jnp.maximum(jnp.sqrt(ss), 1e-12)
    k2 = k * (1.0 + (a - 1.0) * ka_ref[...])
    b = kn * a
    for h in range(A_HEADS):
        sl = slice(h * A_HD, (h + 1) * A_HD)
        r_o[h] = r[:, sl]
        lw_o[h] = logw[:, sl]
        k_o[h] = k2[:, sl]
        v_o[h] = v[:, sl]
        kn_o[h] = kn[:, sl]
        b_o[h] = b[:, sl]
        g_o[h] = g[:, sl]


def _rwkv_pre_call(z, fill_s, mu, w0, a0, k_k, k_a, w2p, a2p, g2p, e64, n_p, seq_s):
    M = z.shape[0]
    W = BRANCH_W
    cb = Z_A // 2048
    vec = lambda n: pl.BlockSpec((1, n), lambda i: (0, 0))
    mat = pl.BlockSpec((W, W), lambda i: (0, 0))
    out = jax.ShapeDtypeStruct((A_HEADS, M, A_HD), F32)
    ospec = pl.BlockSpec((A_HEADS, TM, A_HD), lambda i: (0, i, 0))
    return pl.pallas_call(
        functools.partial(_rwkv_pre_kernel, n_p=n_p, seq_s=seq_s), grid=(M // TM,),
        in_specs=[pl.BlockSpec((TM, 2048), lambda i: (i, cb)),
                  pl.BlockSpec((8, 2048), lambda i: (jnp.maximum(i * (TM // 8) - 1, 0), cb)),
                  pl.BlockSpec((TM, 2048), lambda i: (jnp.maximum(i - n_p, 0), 0)),
                  vec(2048), vec(W), vec(W), vec(W), vec(W), mat, mat, mat, mat],
        out_specs=[ospec] * 7, out_shape=[out] * 7,
        compiler_params=_cp(("arbitrary",)), name="rwkv_pre")(
            z, z, fill_s, mu, w0, a0, k_k, k_a, w2p, a2p, g2p, e64)


def _rwkv_chunk_kernel(r_ref, lw_ref, k_ref, v_ref, kn_ref, b_ref, g_ref, s0_ref, rk_ref, lng_ref, lnb_ref,
                       *rest, C, Q, nc):
    y_ref, sfin_ref, S_ref = rest[-3:]
    c = pl.program_id(1)

    @pl.when(c == 0)
    def _():
        S_ref[...] = s0_ref[...]

    H = range(A_HEADS)
    row = _iota((C, C), 0)
    col = _iota((C, C), 1)
    tri = (row >= col).astype(BF16)
    eye = (row == col).astype(F32)
    row2 = _iota((2 * C, C), 0)
    col2 = _iota((2 * C, C), 1)
    low2 = jnp.where(row2 < C, row2 - 1, row2 - C) >= col2
    eye_k = _iota((A_HD, A_HD), 0) == _iota((A_HD, A_HD), 1)
    HQ = [(h, q) for q in range(Q) for h in H]
    take = lambda ref: {(h, q): ref[h, q * C:(q + 1) * C, :] for h, q in HQ}
    r, lw, k, v, kn, b, g = (take(ref) for ref in (r_ref, lw_ref, k_ref, v_ref, kn_ref, b_ref, g_ref))
    L = {p: _dot_exact_l(tri, lw[p]) for p in HQ}
    eL = {p: jnp.exp(L[p]) for p in HQ}
    eN = {p: jnp.exp(-L[p]) for p in HQ}
    eE = {p: jnp.exp(L[p][C - 1:C, :] - L[p]) for p in HQ}
    lhs = {p: jnp.concatenate([kn[p] * jnp.exp(L[p] - lw[p]), r[p] * eL[p]], axis=0) for p in HQ}
    Ab = {p: jnp.where(low2, _dot1(lhs[p], b[p] * eN[p], NT), 0.0) for p in HQ}
    Ak = {p: jnp.where(low2, _dot1(lhs[p], k[p] * eN[p], NT), 0.0) for p in HQ}
    N = {p: Ab[p][0:C] for p in HQ}
    X = {p: eye - N[p] for p in HQ}
    P = {p: _dot1(N[p], N[p]) for p in HQ}
    n = 2
    while n < C:
        X = {p: X[p] + _dot1(X[p], P[p]) for p in HQ}
        n *= 2
        if n < C:
            P = {p: _dot1(P[p], P[p]) for p in HQ}
    AV = {p: _dot1(Ak[p], v[p]) for p in HQ}
    gcol = {p: jnp.sum(jnp.where(eye_k, jnp.broadcast_to(eL[p][C - 1:C, :], (A_HD, A_HD)), 0.0), axis=1,
                       keepdims=True) for p in HQ}
    kb = {p: jnp.concatenate([k[p] * eE[p], -(b[p] * eE[p])], axis=0) for p in HQ}
    bonus = {p: jnp.sum(r[p] * k[p] * rk_ref[p[0]], axis=1, keepdims=True) * v[p] for p in HQ}
    S = [S_ref[h] for h in H]
    rows = []
    for q in range(Q):
        KS = [_dot1(lhs[h, q], S[h]) for h in H]
        U = [_dot1(X[h, q], KS[h][0:C] + AV[h, q][0:C]) for h in H]
        Y = [KS[h][C:] + AV[h, q][C:] - _dot1(Ab[h, q][C:], U[h]) for h in H]
        S = [S[h] * gcol[h, q] + _dot1(kb[h, q], jnp.concatenate([v[h, q], U[h]], axis=0), TN) for h in H]
        rows.append(jnp.concatenate(
            [(_ln(Y[h], RWKV_GN_EPS) * lng_ref[h] + lnb_ref[h] + bonus[h, q]) * g[h, q] for h in H], axis=1))
    y_ref[...] = jnp.concatenate(rows, axis=0) if Q > 1 else rows[0]
    for h in H:
        S_ref[h] = S[h]

    @pl.when(c == nc - 1)
    def _():
        for h in H:
            sfin_ref[h] = S[h]


def _rwkv_chunk_call(pre, s0t, r_k, ln_g, ln_b, row0, B, T, C, Q, into):
    R = C * Q
    nc = T // R
    blk0 = row0 // R
    in_spec = pl.BlockSpec((A_HEADS, R, A_HD), lambda b, c: (0, blk0 + b * nc + c, 0))
    hvec = pl.BlockSpec((A_HEADS, 1, A_HD), lambda b, c: (0, 0, 0))
    st = pl.BlockSpec((None, A_HEADS, A_HD, A_HD), lambda b, c: (b, 0, 0, 0))
    return pl.pallas_call(
        functools.partial(_rwkv_chunk_kernel, C=C, Q=Q, nc=nc), grid=(B, nc),
        in_specs=[in_spec] * 7 + [st, hvec, hvec, hvec, pl.BlockSpec(memory_space=pl.ANY)],
        out_specs=[pl.BlockSpec((R, BRANCH_W), lambda b, c: (blk0 + b * nc + c, 0)), st],
        out_shape=[jax.ShapeDtypeStruct(into.shape, F32),
                   jax.ShapeDtypeStruct((B, A_HEADS, A_HD, A_HD), F32)],
        scratch_shapes=[pltpu.VMEM((A_HEADS, A_HD, A_HD), F32)],
        input_output_aliases={11: 0},
        compiler_params=_cp(("arbitrary", "arbitrary")), name="rwkv_chunk_c%d" % C)(
            *pre, s0t, r_k, ln_g, ln_b, into)


def _foxcum_kernel(fl_ref, bf_ref, lf_ref, cum_ref, carry_ref):
    @pl.when(pl.program_id(0) == 0)
    def _():
        carry_ref[...] = jnp.zeros_like(carry_ref)

    lane = _iota((TM, LANES), 1)
    lf = jnp.where(lane < NH, _log_sigmoid(fl_ref[...] + bf_ref[...]), 0.0)
    tri = (_iota((TM, TM), 0) >= _iota((TM, TM), 1)).astype(BF16)
    cum = _dot_exact_l(tri, lf) + carry_ref[...]
    lf_ref[...] = lf
    cum_ref[...] = cum
    carry_ref[...] = cum[TM - 1:TM, :]


def _foxcum_call(z, bf):
    M = z.shape[0]
    cb = (Z_B + 3 * BRANCH_W) // LANES
    spec = pl.BlockSpec((TM, LANES), lambda i: (i, 0))
    return pl.pallas_call(
        _foxcum_kernel, grid=(M // TM,),
        in_specs=[pl.BlockSpec((TM, LANES), lambda i: (i, cb)), pl.BlockSpec((1, LANES), lambda i: (0, 0))],
        out_specs=[spec, spec], out_shape=[jax.ShapeDtypeStruct((M, LANES), F32)] * 2,
        scratch_shapes=[pltpu.VMEM((1, LANES), F32)],
        compiler_params=_cp(("arbitrary",)), name="fox_cum")(z, bf)


def _attn_prep_kernel(fq_ref, fk_ref, fv_ref, mq_ref, mk_ref, mv_ref, cum_ref, cprev_ref, cqprev_ref, km_ref,
                      fqT_o, fkA_o, fvT_o, mqT_o, mk_o, mvT_o, sel_o, *, scale):
    i = pl.program_id(0)
    lane = _iota((TM, LANES), 1)
    cum = cum_ref[...]
    qscale = scale * LOG2E
    crel_k = (cum - jnp.where(i == 0, 0.0, cprev_ref[7:8, :])) * LOG2E
    crel_q = (cum - jnp.where(i < FLASH_TQ // TM, 0.0, cqprev_ref[7:8, :])) * LOG2E
    fq = fq_ref[...]
    fk = fk_ref[...]
    fv = fv_ref[...]
    mq = mq_ref[...]
    mk = mk_ref[...]
    mv = mv_ref[...]
    km = km_ref[...]
    blk = _iota((km.shape[0], TM), 0)
    blk_f = blk.astype(F32)
    ones_rows = jnp.where(_iota((VROWS - HD, TM), 0) == 0, 1.0, 0.0).astype(BF16)
    for h in range(NH):
        sl = slice(h * HD, (h + 1) * HD)
        q3 = [p.astype(F32) for p in _split3(jnp.broadcast_to(crel_q[:, h:h + 1], (TM, LANES)))]
        k3 = [p.astype(F32) for p in _split3(jnp.broadcast_to(crel_k[:, h:h + 1], (TM, LANES)))]
        qb = jnp.where(lane == 0, q3[0], jnp.where(lane == 1, q3[1], jnp.where(lane == 2, q3[2],
                                                                             jnp.where(lane < 6, 1.0, 0.0))))
        kb = jnp.where(lane < 3, 1.0, jnp.where(lane == 3, -k3[0], jnp.where(lane == 4, -k3[1],
                                                                              jnp.where(lane == 5, -k3[2], 0.0))))
        fqT_o[h, 0:HD, :] = (fq[:, sl] * qscale).T.astype(BF16)
        fqT_o[h, HD:2 * HD, :] = qb.T.astype(BF16)
        fkA_o[h, :, 0:HD] = fk[:, sl].astype(BF16)
        fkA_o[h, :, HD:2 * HD] = kb.astype(BF16)
        fvT_o[h, 0:HD, :] = fv[:, sl].T.astype(BF16)
        fvT_o[h, HD:VROWS, :] = ones_rows
        mqT = mq[:, sl].T
        mqT_o[h] = (mqT * qscale).astype(BF16)
        mk_o[h] = mk[:, sl].astype(BF16)
        mvT_o[h, 0:HD, :] = mv[:, sl].T.astype(BF16)
        mvT_o[h, HD:VROWS, :] = ones_rows
        sc = _dot3(km[:, sl], mqT)
        sel_o[h] = _top3_select(sc, blk < i, blk_f, axis=0)


def _attn_prep_call(z, cum, kmean, Tp):
    W = BRANCH_W
    nbp = kmean.shape[0]
    cbB = Z_B // W
    cbC = Z_C // W
    rq = FLASH_TQ // TM
    zs = lambda cb: pl.BlockSpec((TM, W), lambda i: (i, cb))
    colT = lambda rows: pl.BlockSpec((NH, rows, TM), lambda i: (0, 0, i))
    rowm = lambda cols: pl.BlockSpec((NH, TM, cols), lambda i: (0, i, 0))
    sd = jax.ShapeDtypeStruct
    return pl.pallas_call(
        functools.partial(_attn_prep_kernel, scale=HD ** -0.5), grid=(Tp // TM,),
        in_specs=[zs(cbB), zs(cbB + 1), zs(cbB + 2), zs(cbC), zs(cbC + 1), zs(cbC + 2),
                  pl.BlockSpec((TM, LANES), lambda i: (i, 0)),
                  pl.BlockSpec((8, LANES), lambda i: (jnp.maximum(i * (TM // 8) - 1, 0), 0)),
                  pl.BlockSpec((8, LANES), lambda i: (jnp.maximum((i // rq) * (FLASH_TQ // 8) - 1, 0), 0)),
                  pl.BlockSpec((nbp, W), lambda i: (0, 0))],
        out_specs=[colT(2 * HD), rowm(2 * HD), colT(VROWS), colT(HD), rowm(HD), colT(VROWS), colT(nbp)],
        out_shape=[sd((NH, 2 * HD, Tp), BF16), sd((NH, Tp, 2 * HD), BF16), sd((NH, VROWS, Tp), BF16),
                   sd((NH, HD, Tp), BF16), sd((NH, Tp, HD), BF16), sd((NH, VROWS, Tp), BF16),
                   sd((NH, nbp, Tp), F32)],
        compiler_params=_cp(("arbitrary",)), name="attn_prep")(z, z, z, z, z, z, cum, cum, cum, kmean)


def _flash_kernel(ti_ref, tj_ref, bq_ref, bk_ref, qT_ref, kA_ref, vT_ref, *rest, moba):
    sel_ref = rest[0] if moba else None
    o_ref, m_ref, acc_ref = rest[-3:]
    step = pl.program_id(0)
    i = ti_ref[step]
    j = tj_ref[step]
    T = kA_ref.shape[1]
    Tq = qT_ref.shape[2]
    ratio = Tq // T
    nqt = bq_ref.shape[0] // NH
    nkt = bk_ref.shape[0] // NH

    @pl.when(j == 0)
    def _():
        m_ref[...] = jnp.full(m_ref.shape, NEG, F32)
        acc_ref[...] = jnp.zeros(acc_ref.shape, F32)

    def step_body(on_diagonal):
        H = range(NH)
        sT = {h: _dot(kA_ref[h], qT_ref[h]) for h in H}
        if on_diagonal:
            causal = (_iota((T, Tq), 0) - _iota((T, Tq), 1)) <= (i * Tq - j * T)
            sT = {h: jnp.where(causal, sT[h], NEG) for h in H}
        if moba:
            own = (_iota((1, Tq), 1) // T + i * ratio) == j
            sT = {h: sT[h] + (jnp.where(own, 1.0, sel_ref[h, pl.ds(j, 1), :]) - 1.0) * (-NEG) for h in H}
            off = {h: 0.0 for h in H}
        else:
            off = {h: (bq_ref[h * nqt + i] - bk_ref[h * nkt + j]) * LOG2E for h in H}
        m_prev = {h: m_ref[h] for h in H}
        m_new = {h: jnp.maximum(m_prev[h], jnp.max(sT[h], axis=0, keepdims=True) + off[h]) for h in H}
        p = {h: jnp.exp2((sT[h] - (m_new[h] - off[h])).astype(BF16)) for h in H}
        alpha = {h: jnp.exp2(m_prev[h] - m_new[h]) for h in H}
        pv = {h: _dot(vT_ref[h], p[h]) for h in H}
        for h in H:
            acc_ref[h] = alpha[h] * acc_ref[h] + pv[h]
            m_ref[h] = m_new[h]

    crosses = (j + 1) * T - 1 > i * Tq
    pl.when(crosses)(functools.partial(step_body, True))
    pl.when(jnp.logical_not(crosses))(functools.partial(step_body, False))

    @pl.when(j == (i + 1) * ratio - 1)
    def _():
        for h in range(NH):
            acc = acc_ref[h]
            o_ref[:, h * HD:(h + 1) * HD] = (acc[0:HD] / acc[HD:HD + 1]).T


def _flash_call(qT, kA, vT, base_q, base_k, sel, Tp, into, name):
    T = MOBA_BLOCK
    Tq = FLASH_TQ
    ratio = Tq // T
    Kc = kA.shape[2]
    moba = sel is not None
    ti = np.concatenate([np.full((i + 1) * ratio, i, np.int32) for i in range(Tp // Tq)])
    tj = np.concatenate([np.arange((i + 1) * ratio, dtype=np.int32) for i in range(Tp // Tq)])
    in_specs = [pl.BlockSpec((NH, Kc, Tq), lambda s, ti, tj, bq, bk: (0, 0, ti[s])),
                pl.BlockSpec((NH, T, Kc), lambda s, ti, tj, bq, bk: (0, tj[s], 0)),
                pl.BlockSpec((NH, VROWS, T), lambda s, ti, tj, bq, bk: (0, 0, tj[s]))]
    args = [qT, kA, vT]
    if moba:
        in_specs.append(pl.BlockSpec((NH, sel.shape[1], Tq), lambda s, ti, tj, bq, bk: (0, 0, ti[s])))
        args.append(sel)
    in_specs.append(pl.BlockSpec(memory_space=pl.ANY))
    return pl.pallas_call(
        functools.partial(_flash_kernel, moba=moba),
        grid_spec=pltpu.PrefetchScalarGridSpec(
            num_scalar_prefetch=4, grid=(len(ti),), in_specs=in_specs,
            out_specs=pl.BlockSpec((Tq, NH * HD), lambda s, ti, tj, bq, bk: (ti[s], 0)),
            scratch_shapes=[pltpu.VMEM((NH, 1, Tq), F32), pltpu.VMEM((NH, VROWS, Tq), F32)]),
        out_shape=jax.ShapeDtypeStruct(into.shape, F32),
        input_output_aliases={4 + len(args): 0},
        compiler_params=_cp(("arbitrary",)), name=name)(
            jnp.asarray(ti), jnp.asarray(tj), base_q, base_k, *args, into)


def _top3_select(sc, valid, idx_f, axis=1):
    sc = jnp.where(valid, sc, -jnp.inf)
    sel = jnp.zeros(sc.shape, F32)
    for _ in range(MOBA_TOPK):
        mx = jnp.max(sc, axis=axis, keepdims=True)
        idx = jnp.min(jnp.where(sc == mx, idx_f, 1e9), axis=axis, keepdims=True)
        hit = idx_f == idx
        sel = jnp.where(hit & valid, 1.0, sel)
        sc = jnp.where(hit, -jnp.inf, sc)
    return sel


def _blockmean_kernel(k_ref, o_ref):
    o_ref[...] = jnp.sum(k_ref[...], axis=0, keepdims=True) * (1.0 / MOBA_BLOCK)


def _blockmean_call(z, Tp):
    nb = Tp // MOBA_BLOCK
    W = BRANCH_W
    cb = Z_C // W + 1
    return pl.pallas_call(
        _blockmean_kernel, grid=(nb,),
        in_specs=[pl.BlockSpec((MOBA_BLOCK, W), lambda i: (i, cb))],
        out_specs=pl.BlockSpec((None, 1, W), lambda i: (i, 0, 0)),
        out_shape=jax.ShapeDtypeStruct((nb, 1, W), F32),
        compiler_params=_cp(("arbitrary",)), name="moba_kmean")(z)


def _ret_kernel(q_ref, k_ref, v_ref, gd_ref, cos_ref, sin_ref, din_ref, qd_ref, kd_ref, cd_ref, s0_ref, lng_ref,
                *rest, nc):
    y_ref, sfin_ref, S_ref = rest[-3:]
    c = pl.program_id(1)

    @pl.when(c == 0)
    def _():
        S_ref[...] = s0_ref[...]

    q = q_ref[...]
    k = k_ref[...]
    v = v_ref[...]
    gd = gd_ref[...]
    cos = cos_ref[...]
    sin = sin_ref[...]
    lng = lng_ref[...]
    for h in range(NH):
        sl = slice(h * HD, (h + 1) * HD)
        qh = q[:, sl]
        kh = k[:, sl]
        qr = qh * cos + pltpu.roll(qh, HD // 2, axis=1) * sin
        kr = (kh * cos + pltpu.roll(kh, HD // 2, axis=1) * sin) * (HD ** -0.5)
        vb = v[:, sl].astype(BF16)
        qb = qr.astype(BF16)
        att = _dot(qb, kr.astype(BF16), NT) * din_ref[h]
        S = S_ref[h]
        o = _dot(att.astype(BF16), vb) + _dot(qb, S.astype(BF16)) * qd_ref[h]
        S_ref[h] = S * cd_ref[h] + _dot((kr * kd_ref[h]).astype(BF16), vb, TN)
        y_ref[:, sl] = _ln(o, RET_GN_EPS) * lng[:, sl] * (gd[:, sl] * _sigmoid(gd[:, sl]))

    @pl.when(c == nc - 1)
    def _():
        sfin_ref[...] = S_ref[...]


def _ret_tables(C):
    lg = np.log(1.0 - 2.0 ** (-5.0 - np.arange(NH, dtype=np.float32))).astype(np.float32)
    i = np.arange(C, dtype=np.float32)
    diff = i[:, None] - i[None, :]
    din = np.where(diff[None] >= 0, np.exp(np.maximum(diff, 0.0)[None] * lg[:, None, None]), 0.0)
    qd = np.exp((i[None, :] + 1.0) * lg[:, None])
    kd = np.exp((C - 1.0 - i)[None, :] * lg[:, None])
    cd = np.exp(C * lg)
    bc = lambda t: np.ascontiguousarray(np.broadcast_to(t[:, :, None], (NH, C, HD))).astype(np.float32)
    cdb = np.ascontiguousarray(np.broadcast_to(cd[:, None, None], (NH, 1, HD))).astype(np.float32)
    return din.astype(np.float32), bc(qd), bc(kd), cdb


def _rope_tables(pos0, T):
    half = HD // 2
    inv = 1.0 / (ROPE_BASE ** (jnp.arange(half, dtype=F32) / half))
    ang = (pos0 + jnp.arange(T)).astype(F32)[:, None] * inv[None, :]
    cos = jnp.cos(ang)
    sin = jnp.sin(ang)
    return jnp.concatenate([cos, cos], axis=1), jnp.concatenate([-sin, sin], axis=1)


def _ret_call(z, s0, ln_g, l, row0, B, T, C, pos0, into):
    nc = T // C
    W = BRANCH_W
    cb = Z_D // W
    blk0 = row0 // C
    cos, sin = _rope_tables(pos0, T)
    din, qd, kd, cd = _ret_tables(C)
    zs = lambda off: pl.BlockSpec((C, W), lambda b, c: (blk0 + b * nc + c, cb + off))
    tab = pl.BlockSpec((C, HD), lambda b, c: (c, 0))
    full = lambda shape: pl.BlockSpec(shape, lambda b, c: (0,) * len(shape))
    st = pl.BlockSpec((None, NH, HD, HD), lambda b, c: (b, 0, 0, 0))
    return pl.pallas_call(
        functools.partial(_ret_kernel, nc=nc), grid=(B, nc),
        in_specs=[zs(0), zs(1), zs(2), zs(3), tab, tab, full((NH, C, C)), full((NH, C, HD)), full((NH, C, HD)),
                  full((NH, 1, HD)), st, pl.BlockSpec((None, 1, W), lambda b, c: (l, 0, 0)),
                  pl.BlockSpec(memory_space=pl.ANY)],
        out_specs=[pl.BlockSpec((C, W), lambda b, c: (blk0 + b * nc + c, 0)), st],
        out_shape=[jax.ShapeDtypeStruct(into.shape, F32), jax.ShapeDtypeStruct((B, NH, HD, HD), F32)],
        scratch_shapes=[pltpu.VMEM((NH, HD, HD), F32)],
        input_output_aliases={12: 0},
        compiler_params=_cp(("arbitrary", "arbitrary")), name="ret_c%d" % C)(
            z, z, z, z, cos, sin, din, qd, kd, cd, s0, ln_g, into)


def _lfsuf_kernel(pt_ref, lf_hbm, ts_ref, tot_ref, o_ref, buf, sem, *, npg, base):
    b = pl.program_id(0)

    def page_copy(p):
        return pltpu.make_async_copy(lf_hbm.at[pl.ds(NH * (base + pt_ref[b * npg + p]), NH), :],
                                     buf.at[pl.ds(NH * p, NH), :], sem.at[p])

    for p in range(npg):
        page_copy(p).start()
    for p in range(npg):
        page_copy(p).wait()
    loc = None
    tot = None
    for h in range(NH):
        x = buf[pl.ds(h, npg, stride=NH), :]
        a = _dot_exact_r(x, ts_ref[h])
        t = _dot_exact_r(x, tot_ref[h])
        loc = a if loc is None else loc + a
        tot = t if tot is None else tot + t
    later = (_iota((npg, npg), 1) > _iota((npg, npg), 0)).astype(BF16)
    o_ref[...] = loc + _dot_exact_l(later, tot)


def _lfsuf_call(pt_flat, lf_rows, ts, tot, B, npg, base):
    PW = PAGE_SIZE * NH
    sel = pl.BlockSpec((NH, PAGE_SIZE, PW), lambda b, pt: (0, 0, 0))
    return pl.pallas_call(
        functools.partial(_lfsuf_kernel, npg=npg, base=base),
        grid_spec=pltpu.PrefetchScalarGridSpec(
            num_scalar_prefetch=1, grid=(B,),
            in_specs=[pl.BlockSpec(memory_space=pl.ANY), sel, sel],
            out_specs=pl.BlockSpec((None, npg, PW), lambda b, pt: (b, 0, 0)),
            scratch_shapes=[pltpu.VMEM((NH * npg, PAGE_SIZE), F32), pltpu.SemaphoreType.DMA((npg,))]),
        out_shape=jax.ShapeDtypeStruct((B, npg, PW), F32),
        compiler_params=_cp(("arbitrary",)), name="fox_logf_suffix")(pt_flat, lf_rows, ts, tot)


def _suffix_matrices():
    r = np.arange(PAGE_SIZE)
    dst_r = np.repeat(r, NH)
    dst_h = np.tile(np.arange(NH), PAGE_SIZE)
    same = np.arange(NH)[:, None, None] == dst_h[None, None, :]
    ts = same & (r[None, :, None] > dst_r[None, None, :])
    return jnp.asarray(ts, BF16), jnp.asarray(np.broadcast_to(same, ts.shape), BF16)


def _sattn_kernel(pt_ref, fq_ref, fkn_ref, fvn_ref, lfn_ref, mq_ref, mkn_ref, mvn_ref, R_ref, *rest, nb, S, scale):
    P = SAMPLE_PAGES
    fk, fv, mk, mv = (rest[t * P:(t + 1) * P] for t in range(4))
    yb_ref, yc_ref, fqb, mqb, mqf, cnb, fm, fl, facc, sc_all, m_all, l_all, o_all = rest[-13:]
    n = pl.program_id(1)
    R4 = NH * S
    PW = PAGE_SIZE * NH
    own_head = (_iota((R4, PW), 1) % NH) == (_iota((R4, PW), 0) // S)
    lane = _iota((R4, LANES), 1)

    def stack_heads(x):
        return jnp.concatenate([x[:, h * HD:(h + 1) * HD] for h in range(NH)], axis=0)

    def head_sums(kpage):
        return jnp.sum(kpage.reshape(PW // 8, 8, HD), axis=0)

    @pl.when(n == 0)
    def _():
        fqb[...] = stack_heads(fq_ref[...]).astype(BF16)
        mq = stack_heads(mq_ref[...])
        mqb[...] = mq.astype(BF16)
        mqf[...] = mq
        tri = (_iota((S, S), 0) >= _iota((S, S), 1)).astype(BF16)
        cn = _dot_exact_l(tri, lfn_ref[...])
        cnb[...] = jnp.concatenate([jnp.broadcast_to(cn[:, h:h + 1], (S, LANES)) for h in range(NH)], axis=0)
        fm[...] = jnp.full(fm.shape, NEG, F32)
        fl[...] = jnp.zeros(fl.shape, F32)
        facc[...] = jnp.zeros(facc.shape, F32)
        sc_all[...] = jnp.full(sc_all.shape, -jnp.inf, F32)
        m_all[...] = jnp.full(m_all.shape, NEG, F32)
        l_all[...] = jnp.zeros(l_all.shape, F32)

    def fox_update(s_list, v_list):
        m_prev = fm[...]
        m_new = m_prev
        for s in s_list:
            m_new = jnp.maximum(m_new, jnp.max(s, axis=1, keepdims=True))
        alpha = jnp.exp(m_prev - m_new)
        p_list = [jnp.exp(s - m_new) for s in s_list]
        l_new = alpha * fl[...]
        acc = alpha * facc[...]
        for p, v_bf in zip(p_list, v_list):
            l_new = l_new + jnp.sum(p, axis=1, keepdims=True)
            acc = acc + _dot(p.astype(BF16), v_bf)
        fl[...] = l_new
        facc[...] = acc
        fm[...] = m_new

    cn_b = cnb[...]
    cn_w = jnp.concatenate([cn_b] * NH, axis=1)
    fq_b = fqb[...]
    s_fox = [_dot(fq_b, fk[g][...].astype(BF16), NT) * scale for g in range(P)]
    s_fox = [jnp.where(own_head, s_fox[g] + cn_w + R_ref[pl.ds(P * n + g, 1), :], NEG) for g in range(P)]
    fox_update(s_fox, [fv[g][...].astype(BF16) for g in range(P)])

    mq_b = mqb[...]
    mq_f = mqf[...]
    kpg = [mk[g][...] for g in range(P)]
    s_mo = [jnp.where(own_head, _dot(mq_b, kpg[g].astype(BF16), NT) * scale, NEG) for g in range(P)]
    for blk in range(P // 2):
        g0, g1 = 2 * blk, 2 * blk + 1
        ks = head_sums(kpg[g0]) + head_sums(kpg[g1])
        kmean = (ks[0:NH] + ks[NH:2 * NH]) * (1.0 / MOBA_BLOCK)
        kmean = jnp.concatenate([jnp.broadcast_to(kmean[h:h + 1], (S, HD)) for h in range(NH)], axis=0)
        sc_col = jnp.sum(mq_f * kmean, axis=1, keepdims=True)
        m_b = jnp.maximum(jnp.max(s_mo[g0], axis=1, keepdims=True), jnp.max(s_mo[g1], axis=1, keepdims=True))
        p0 = jnp.exp(s_mo[g0] - m_b)
        p1 = jnp.exp(s_mo[g1] - m_b)
        here = lane == (P // 2) * n + blk
        sc_all[...] = jnp.where(here, sc_col, sc_all[...])
        m_all[...] = jnp.where(here, m_b, m_all[...])
        l_all[...] = jnp.where(here, jnp.sum(p0, axis=1, keepdims=True) + jnp.sum(p1, axis=1, keepdims=True),
                               l_all[...])
        o_all[(P // 2) * n + blk] = (_dot(p0.astype(BF16), mv[g0][...].astype(BF16))
                                     + _dot(p1.astype(BF16), mv[g1][...].astype(BF16)))

    @pl.when(n == pl.num_programs(1) - 1)
    def _():
        ri = _iota((R4, R4), 0)
        ci = _iota((R4, R4), 1)
        causal = ((ri // S) == (ci // S)) & ((ci % S) <= (ri % S))
        cn_col = cn_b[:, 0:1]
        cn_row = jnp.sum(jnp.where(ri == ci, jnp.broadcast_to(cn_col, (R4, R4)), 0.0), axis=0, keepdims=True)
        s = _dot(fqb[...], stack_heads(fkn_ref[...]).astype(BF16), NT) * scale
        s = s + (cn_col - cn_row)
        fox_update([jnp.where(causal, s, NEG)], [stack_heads(fvn_ref[...]).astype(BF16)])
        out = facc[...] / fl[...]
        for h in range(NH):
            yb_ref[:, h * HD:(h + 1) * HD] = out[h * S:(h + 1) * S, :]

        s = _dot(mqb[...], stack_heads(mkn_ref[...]).astype(BF16), NT) * scale
        s = jnp.where(causal, s, NEG)
        m_o = jnp.max(s, axis=1, keepdims=True)
        p = jnp.exp(s - m_o)
        l_o = jnp.sum(p, axis=1, keepdims=True)
        o_o = _dot(p.astype(BF16), stack_heads(mvn_ref[...]).astype(BF16))
        sel = _top3_select(sc_all[...], lane < nb, lane.astype(F32)) > 0.5
        m_past = m_all[...]
        m_tot = jnp.maximum(jnp.max(jnp.where(sel, m_past, NEG), axis=1, keepdims=True), m_o)
        w = jnp.where(sel, jnp.exp(m_past - m_tot), 0.0)
        w_o = jnp.exp(m_o - m_tot)
        den = jnp.sum(w * l_all[...], axis=1, keepdims=True) + w_o * l_o
        num = w_o * o_o
        for blk in range(nb):
            num = num + w[:, blk:blk + 1] * o_all[blk]
        out = num / den
        for h in range(NH):
            yc_ref[:, h * HD:(h + 1) * HD] = out[h * S:(h + 1) * S, :]


def _sattn_call(pt_flat, z, lf, R, caches, l, Mp, B, S, npg, n_pool, yb_into, yc_into):
    fk, fv, mk, mv = caches
    W = BRANCH_W
    PW = PAGE_SIZE * NH
    nb = npg // 2
    R4 = NH * S
    rb = Mp // S
    base = l * n_pool
    zrow = lambda cb: pl.BlockSpec((S, W), lambda b, n, pt: (rb + b, cb))
    P = SAMPLE_PAGES
    page = lambda g: pl.BlockSpec((PW, HD), lambda b, n, pt: (base + pt[b * npg + P * n + g], 0))
    pages = [page(g) for g in range(P)]
    cbB = Z_B // W
    cbC = Z_C // W
    in_specs = [zrow(cbB), zrow(cbB + 1), zrow(cbB + 2),
                pl.BlockSpec((S, LANES), lambda b, n, pt: (rb + b, 0)),
                zrow(cbC), zrow(cbC + 1), zrow(cbC + 2),
                pl.BlockSpec((None, npg, PW), lambda b, n, pt: (b, 0, 0))] + pages * 4
    in_specs += [pl.BlockSpec(memory_space=pl.ANY)] * 2
    n_in = 1 + len(in_specs)
    out_spec = pl.BlockSpec((S, W), lambda b, n, pt: (rb + b, 0))
    scratch = [pltpu.VMEM((R4, HD), BF16), pltpu.VMEM((R4, HD), BF16), pltpu.VMEM((R4, HD), F32),
               pltpu.VMEM((R4, LANES), F32),
               pltpu.VMEM((R4, 1), F32), pltpu.VMEM((R4, 1), F32), pltpu.VMEM((R4, HD), F32),
               pltpu.VMEM((R4, LANES), F32), pltpu.VMEM((R4, LANES), F32), pltpu.VMEM((R4, LANES), F32),
               pltpu.VMEM((nb, R4, HD), F32)]
    return pl.pallas_call(
        functools.partial(_sattn_kernel, nb=nb, S=S, scale=HD ** -0.5),
        grid_spec=pltpu.PrefetchScalarGridSpec(
            num_scalar_prefetch=1, grid=(B, npg // P), in_specs=in_specs,
            out_specs=[out_spec, out_spec], scratch_shapes=scratch),
        out_shape=[jax.ShapeDtypeStruct(yb_into.shape, F32), jax.ShapeDtypeStruct(yc_into.shape, F32)],
        input_output_aliases={n_in - 2: 0, n_in - 1: 1},
        compiler_params=_cp(("arbitrary", "arbitrary")), name="sample_attn")(
            pt_flat, z, z, z, lf, z, z, z, R, *([fk] * P + [fv] * P + [mk] * P + [mv] * P), yb_into, yc_into)


def _pick_tile(m, cands):
    for t in cands:
        if m % t == 0:
            return t
    raise ValueError("no row tile divides %d" % m)


def _pad_cols(x, n):
    return jnp.pad(x, ((0, 0), (0, n - x.shape[1])))


def kernel(x_prompt, x_sample, c_prompt, c_sample, page_table, cache_fox_k, cache_fox_v, cache_fox_logf, cache_moba_k, cache_moba_v, state_rwkv, state_rwkv_shift, state_ret, w_ada, b_ada, w_in, rwkv_mu, rwkv_w0, rwkv_w2, rwkv_a0, rwkv_a2, rwkv_g2, rwkv_k_k, rwkv_k_a, rwkv_r_k, rwkv_ln_g, rwkv_ln_b, fox_bf, ret_ln_g, w_branch, w_o, ln1_g, ln1_b, ln2_g, ln2_b, w_ffn_gate, w_ffn_in, w_ffn_out):
    Bp, Tp, D = x_prompt.shape
    Bs, S, _ = x_sample.shape
    depth = w_in.shape[0]
    n_pool = cache_fox_k.shape[1]
    npg = page_table.shape[1]
    past = npg * PAGE_SIZE
    Mp, Ms = Bp * Tp, Bs * S
    M = Mp + Ms
    assert Bp == 1 and D == D_MODEL and depth == DEPTH
    assert Mp % TM == 0 and Ms % TM == 0 and Tp % MOBA_BLOCK == 0 and Tp // MOBA_BLOCK <= LANES
    assert Tp % (RWKV_CHUNK * RWKV_CHUNKS_PER_STEP) == 0 and Tp % RET_CHUNK == 0 and S == 8 and past % MOBA_BLOCK == 0
    assert TM == MOBA_BLOCK and 2 * NH == 8 and MOBA_BLOCK == 2 * PAGE_SIZE
    assert Tp % FLASH_TQ == 0 and FLASH_TQ % MOBA_BLOCK == 0
    assert npg % SAMPLE_PAGES == 0 and SAMPLE_PAGES % 2 == 0 and npg // 2 <= LANES
    n_p = Mp // TM
    W = BRANCH_W
    tm_big = _pick_tile(M, (768, 512, 256))
    tm_small = _pick_tile(M, (384, 256))

    x = jnp.concatenate([x_prompt.reshape(Mp, D), x_sample.reshape(Ms, D)], axis=0)
    c_rows = 8 * (-(-(Bp + Bs) // 8))
    c_all = jnp.pad(jnp.concatenate([c_prompt, c_sample], axis=0), ((0, c_rows - Bp - Bs), (0, 0)))
    pt_flat = page_table.reshape(-1).astype(jnp.int32)
    cache2 = lambda t: t.reshape(depth * n_pool * PAGE_SIZE * NH, HD)
    caches = (cache2(cache_fox_k), cache2(cache_fox_v), cache2(cache_moba_k), cache2(cache_moba_v))
    lf_rows = jnp.swapaxes(cache_fox_logf, 2, 3).reshape(depth * n_pool * NH, PAGE_SIZE)
    ts_m, tot_m = _suffix_matrices()
    e64 = jnp.asarray(np.kron(np.eye(A_HEADS), np.ones((A_HD, A_HD))), BF16)
    b_ada3 = b_ada.reshape(depth, 1, -1)
    vec3 = lambda t: t.reshape(depth, 1, -1)
    ln1_g3, ln1_b3, ln2_g3, ln2_b3, ret_g3 = map(vec3, (ln1_g, ln1_b, ln2_g, ln2_b, ret_ln_g))

    assert w_in.shape[2] == _O_G + 4 * D_MODEL
    tm_in = _pick_tile(M, (768, 512, 256))

    per_layer = []
    for l in range(depth):
        ada = _ada_call(c_all, w_ada, b_ada3, l)
        ada_p = ada[0:8]
        mod_s = jnp.repeat(ada[Bp:Bp + Bs], S, axis=0)
        h = _lnmod_call(x, ada_p, mod_s, n_p, 1, 0)
        z = _win_call(h, w_in, l, tm_in)

        fill_s = jnp.repeat(_pad_cols(state_rwkv_shift[l], 2048), S, axis=0)
        pad_rows = lambda w2, r0: jnp.zeros((W, W), F32).at[r0:r0 + w2.shape[0]].set(w2)
        pre = _rwkv_pre_call(
            z, fill_s, _pad_cols(rwkv_mu[l][None], 2048), rwkv_w0[l][None], rwkv_a0[l][None],
            rwkv_k_k[l][None], rwkv_k_a[l][None], pad_rows(rwkv_w2[l], 0), pad_rows(rwkv_a2[l], A_LORA_W),
            pad_rows(rwkv_g2[l], A_LORA_W + A_LORA_A), e64, n_p, S)
        hv = lambda t: t.reshape(A_HEADS, 1, A_HD)
        rk, lg, lb = hv(rwkv_r_k[l]), hv(rwkv_ln_g[l]), hv(rwkv_ln_b[l])
        blank = jnp.zeros((M, W), F32)
        ya, sT_p = _rwkv_chunk_call(pre, jnp.zeros((Bp, A_HEADS, A_HD, A_HD), F32), rk, lg, lb,
                                    0, Bp, Tp, RWKV_CHUNK, RWKV_CHUNKS_PER_STEP, blank)
        ya, sT_s = _rwkv_chunk_call(pre, jnp.swapaxes(state_rwkv[l], -1, -2), rk, lg, lb, Mp, Bs, S, S, 1, ya)

        lf, cum = _foxcum_call(z, _pad_cols(fox_bf[l][None], LANES))
        nbk = Tp // MOBA_BLOCK
        kmean = _blockmean_call(z, Tp).reshape(nbk, W)
        kmean = jnp.pad(kmean, ((0, 8 * (-(-nbk // 8)) - nbk), (0, 0)))
        fqT, fkA, fvT, mqT, mkb, mvT, sel = _attn_prep_call(z, cum, kmean, Tp)
        tile_base = lambda t: jnp.transpose(jnp.concatenate(
            [jnp.zeros((1, NH), F32), cum[t - 1:Mp - 1:t, :NH]], axis=0)).reshape(-1)
        base_q, base_k = tile_base(FLASH_TQ), tile_base(MOBA_BLOCK)
        yb = _flash_call(fqT, fkA, fvT, base_q, base_k, None, Tp, blank, "fox_prompt")
        yc = _flash_call(mqT, mkb, mvT, base_q, base_k, sel, Tp, blank, "moba_prompt")
        R = _lfsuf_call(pt_flat, lf_rows, ts_m, tot_m, Bs, npg, l * n_pool)
        yb, yc = _sattn_call(pt_flat, z, lf, R, caches, l, Mp, Bs, S, npg, n_pool, yb, yc)

        yd, rS_p = _ret_call(z, jnp.zeros((Bp, NH, HD, HD), F32), ret_g3, l, 0, Bp, Tp, RET_CHUNK, 0, blank)
        yd, rS_s = _ret_call(z, state_ret[l], ret_g3, l, Mp, Bs, S, S, past, yd)

        mixin = _merge_call((ya, yb, yc, yd), w_branch, z, l, tm_big, 512)
        mix = _mm_call(mixin, w_o, l, tm_big, 512, F32, "w_o")
        x1, h2 = _resid_call(x, mix, ada_p, mod_s, ln1_g3, ln1_b3, l, n_p, 2, 4, 3)
        act = _ffn_call(h2, w_ffn_gate, w_ffn_in, l, tm_big, 512)
        f = _mm_call(act, w_ffn_out, l, tm_small, 512, F32, "ffn_out")
        x = _resid_call(x1, f, ada_p, mod_s, ln2_g3, ln2_b3, l, n_p, 5)

        zb = Z_B
        zc = Z_C
        hp = lambda t: t.reshape(Bp, Tp, NH, HD)
        hs = lambda t: t.reshape(Bs, S, NH, HD)
        per_layer.append(dict(
            fox_k_p=hp(z[:Mp, zb + W:zb + 2 * W]), fox_v_p=hp(z[:Mp, zb + 2 * W:zb + 3 * W]),
            fox_lf_p=lf[:Mp, :NH].reshape(Bp, Tp, NH),
            moba_k_p=hp(z[:Mp, zc + W:zc + 2 * W]), moba_v_p=hp(z[:Mp, zc + 2 * W:zc + 3 * W]),
            rwkv_S_p=jnp.swapaxes(sT_p, -1, -2), shift_p=z[Mp - 1:Mp, Z_A:Z_A + A_COLS], ret_S_p=rS_p,
            fox_k_s=hs(z[Mp:, zb + W:zb + 2 * W]), fox_v_s=hs(z[Mp:, zb + 2 * W:zb + 3 * W]),
            fox_lf_s=lf[Mp:, :NH].reshape(Bs, S, NH),
            moba_k_s=hs(z[Mp:, zc + W:zc + 2 * W]), moba_v_s=hs(z[Mp:, zc + 2 * W:zc + 3 * W]),
            rwkv_S_s=jnp.swapaxes(sT_s, -1, -2),
            shift_s=z[Mp:, Z_A:Z_A + A_COLS].reshape(Bs, S, A_COLS)[:, S - 1], ret_S_s=rS_s))

    st = lambda name: jnp.stack([p[name] for p in per_layer])
    return (x[:Mp].reshape(Bp, Tp, D), x[Mp:].reshape(Bs, S, D),
            st("fox_k_p"), st("fox_v_p"), st("fox_lf_p"), st("moba_k_p"), st("moba_v_p"),
            st("rwkv_S_p"), st("shift_p"), st("ret_S_p"),
            st("fox_k_s"), st("fox_v_s"), st("fox_lf_s"), st("moba_k_s"), st("moba_v_s"),
            st("rwkv_S_s"), st("shift_s"), st("ret_S_s"))
```

```python
import functools

import numpy as np
import jax
import jax.numpy as jnp
from jax import lax
from jax.experimental import pallas as pl
from jax.experimental.pallas import tpu as pltpu

F32 = jnp.float32
BF16 = jnp.bfloat16

D_MODEL = 2048
DEPTH = 2
PAGE_SIZE = 128
BRANCH_W = 512
A_HD = 64
A_HEADS = 8
A_LORA_W = 96
A_LORA_A = 96
A_LORA_G = 256
A_COLS = 3 * BRANCH_W + A_LORA_W + A_LORA_A + A_LORA_G
B_COLS = 3 * BRANCH_W + 4
C_COLS = 3 * BRANCH_W
D_COLS = 4 * BRANCH_W
HD = 128
NH = 4
RWKV_GN_EPS = 64e-5
RET_GN_EPS = 1e-6
LN_EPS = 1e-5
MOBA_BLOCK = 256
MOBA_TOPK = 3
RET_CHUNK = 128
RWKV_CHUNK = 64
RWKV_CHUNKS_PER_STEP = 4
ROPE_BASE = 10000.0
FFN_HIDDEN = 5632
ALPHA = (2 * DEPTH) ** 0.25

Z_G = 0
Z_A = 4 * D_MODEL
Z_D = Z_A + 2048
Z_C = Z_D + 2048
Z_B = Z_C + 2048
Z_COLS = Z_B + 2048

TM = 256
FLASH_TQ = 512
SAMPLE_PAGES = 8
VROWS = HD + 16
LANES = 128
NEG = -1e30
LOG2E = 1.4426950408889634
VMEM_LIMIT = 56 << 20

NN = (((1,), (0,)), ((), ()))
NT = (((1,), (1,)), ((), ()))
TN = (((0,), (0,)), ((), ()))


def _cp(sem):
    return pltpu.CompilerParams(dimension_semantics=sem, vmem_limit_bytes=VMEM_LIMIT)


def _dot(a, b, dims=NN):
    return lax.dot_general(a, b, dims, preferred_element_type=F32)


def _split2(x):
    hi = x.astype(BF16)
    return hi, (x - hi.astype(F32)).astype(BF16)


def _split3(x):
    hi = x.astype(BF16)
    r1 = x - hi.astype(F32)
    mid = r1.astype(BF16)
    return hi, mid, (r1 - mid.astype(F32)).astype(BF16)


def _dot3(a, b, dims=NN):
    ah, al = _split2(a)
    bh, bl = _split2(b)
    return _dot(ah, bh, dims) + (_dot(ah, bl, dims) + _dot(al, bh, dims))


def _dot1(a, b, dims=NN):
    return _dot(a.astype(BF16), b.astype(BF16), dims)


def _dot_exact_l(m_bf16, x, n=3):
    parts = _split3(x) if n == 3 else _split2(x)
    acc = _dot(m_bf16, parts[0])
    for p in parts[1:]:
        acc = acc + _dot(m_bf16, p)
    return acc


def _dot_exact_r(x, m_bf16, n=3):
    parts = _split3(x) if n == 3 else _split2(x)
    acc = _dot(parts[0], m_bf16)
    for p in parts[1:]:
        acc = acc + _dot(p, m_bf16)
    return acc


def _ln(x, eps):
    mu = jnp.mean(x, axis=-1, keepdims=True)
    xc = x - mu
    var = jnp.mean(xc * xc, axis=-1, keepdims=True)
    return xc * lax.rsqrt(var + eps)


def _sigmoid(x):
    return 0.5 * jnp.tanh(0.5 * x) + 0.5


def _log_sigmoid(x):
    return jnp.minimum(x, 0.0) - jnp.log(1.0 + jnp.exp(-jnp.abs(x)))


def _iota(shape, dim):
    return lax.broadcasted_iota(jnp.int32, shape, dim)


def _ada_kernel(c_ref, w_ref, b_ref, o_ref):
    c = c_ref[...]
    s = (c * _sigmoid(c)).astype(BF16)
    o_ref[...] = _dot(s, w_ref[...].astype(BF16)) + b_ref[...]


def _ada_call(c_all, w_ada, b_ada, l):
    R, D = c_all.shape
    N = w_ada.shape[2]
    tn = 2048
    return pl.pallas_call(
        _ada_kernel, grid=(N // tn,),
        in_specs=[pl.BlockSpec((R, D), lambda j: (0, 0)),
                  pl.BlockSpec((None, D, tn), lambda j: (l, 0, j)),
                  pl.BlockSpec((None, 1, tn), lambda j: (l, 0, j))],
        out_specs=pl.BlockSpec((R, tn), lambda j: (0, j)),
        out_shape=jax.ShapeDtypeStruct((R, N), F32),
        compiler_params=_cp(("arbitrary",)), name="ada")(c_all, w_ada, b_ada)


def _lnmod_kernel(x_ref, scp_ref, shp_ref, scs_ref, shs_ref, o_ref, *, n_p):
    is_s = pl.program_id(0) >= n_p
    sc = jnp.where(is_s, scs_ref[...], scp_ref[0:1, :])
    sh = jnp.where(is_s, shs_ref[...], shp_ref[0:1, :])
    o_ref[...] = (_ln(x_ref[...], LN_EPS) * (1.0 + sc) + sh).astype(BF16)


def _mod_specs(n_p, cols):
    specs = []
    for cb in cols:
        specs.append(pl.BlockSpec((8, D_MODEL), lambda i, cb=cb: (0, cb)))
    for cb in cols:
        specs.append(pl.BlockSpec((TM, D_MODEL), lambda i, cb=cb: (jnp.maximum(i - n_p, 0), cb)))
    return specs


def _lnmod_call(x, ada_p, mod_s, n_p, sc_col, sh_col):
    M, D = x.shape
    return pl.pallas_call(
        functools.partial(_lnmod_kernel, n_p=n_p), grid=(M // TM,),
        in_specs=[pl.BlockSpec((TM, D), lambda i: (i, 0))] + _mod_specs(n_p, (sc_col, sh_col)),
        out_specs=pl.BlockSpec((TM, D), lambda i: (i, 0)),
        out_shape=jax.ShapeDtypeStruct((M, D), BF16),
        compiler_params=_cp(("arbitrary",)), name="lnmod")(x, ada_p, ada_p, mod_s, mod_s)


def _mm_kernel(a_ref, w_ref, o_ref, wb_ref):
    @pl.when(pl.program_id(1) == 0)
    def _():
        wb_ref[...] = w_ref[...].astype(BF16)

    o_ref[...] = _dot(a_ref[...].astype(BF16), wb_ref[...]).astype(o_ref.dtype)


def _win_kernel(st_ref, a_ref, w_ref, o_ref, wb_ref, *, l):
    @pl.when(pl.program_id(1) == 0)
    def _():
        wb_ref[...] = w_ref[:, l, :].T.astype(BF16)

    o_ref[...] = _dot(a_ref[...], wb_ref[...])


W_IN_TN = 1024
_O_B, _O_C, _O_D = A_COLS, A_COLS + B_COLS, A_COLS + B_COLS + C_COLS
_O_G = _O_D + D_COLS
_W_IN_TILE_SRC = np.concatenate([src + W_IN_TN * np.arange(width // W_IN_TN) for src, width in (
    (_O_G, 4 * D_MODEL), (0, Z_D - Z_A), (_O_D, Z_C - Z_D), (_O_C, Z_B - Z_C), (_O_B, Z_COLS - Z_B))]).astype(np.int32)


def _win_call(a, w_in, l, tm):
    M, K = a.shape
    depth = w_in.shape[0]
    w_t = jnp.transpose(w_in, (2, 0, 1))
    tn = W_IN_TN
    nt = Z_COLS // tn
    assert int(_W_IN_TILE_SRC.max()) + tn <= w_in.shape[2] and len(_W_IN_TILE_SRC) == nt
    return pl.pallas_call(
        functools.partial(_win_kernel, l=l),
        grid_spec=pltpu.PrefetchScalarGridSpec(
            num_scalar_prefetch=1, grid=(nt, M // tm),
            in_specs=[pl.BlockSpec((tm, K), lambda j, i, st: (i, 0)),
                      pl.BlockSpec((pl.Element(tn), pl.Element(depth), pl.Element(K)),
                                   lambda j, i, st: (st[j], 0, 0))],
            out_specs=pl.BlockSpec((tm, tn), lambda j, i, st: (i, j)),
            scratch_shapes=[pltpu.VMEM((K, tn), BF16)]),
        out_shape=jax.ShapeDtypeStruct((M, Z_COLS), F32),
        compiler_params=_cp(("arbitrary", "arbitrary")), name="w_in")(jnp.asarray(_W_IN_TILE_SRC), a, w_t)


def _mm_call(a, w, w_index, tm, tn, out_dtype, name):
    M, K = a.shape
    N = w.shape[-1]
    if w.ndim == 3:
        w_spec = pl.BlockSpec((None, K, tn), lambda j, i: (w_index, 0, j))
    else:
        w_spec = pl.BlockSpec((K, tn), lambda j, i: (0, j))
    return pl.pallas_call(
        _mm_kernel, grid=(N // tn, M // tm),
        in_specs=[pl.BlockSpec((tm, K), lambda j, i: (i, 0)), w_spec],
        out_specs=pl.BlockSpec((tm, tn), lambda j, i: (i, j)),
        out_shape=jax.ShapeDtypeStruct((M, N), out_dtype),
        scratch_shapes=[pltpu.VMEM((K, tn), BF16)],
        compiler_params=_cp(("arbitrary", "arbitrary")), name=name)(a, w)


def _resid_kernel(*refs, n_p, with_h):
    if with_h:
        (x_ref, mix_ref, gp_ref, scp_ref, shp_ref, gs_ref, scs_ref, shs_ref, lg_ref, lb_ref,
         x1_ref, h_ref) = refs
    else:
        x_ref, mix_ref, gp_ref, gs_ref, lg_ref, lb_ref, x1_ref = refs
    is_s = pl.program_id(0) >= n_p
    g = jnp.where(is_s, gs_ref[...], gp_ref[0:1, :])
    y = ALPHA * x_ref[...] + (1.0 + g) * mix_ref[...]
    x1 = _ln(y, LN_EPS) * lg_ref[...] + lb_ref[...]
    x1_ref[...] = x1
    if with_h:
        sc = jnp.where(is_s, scs_ref[...], scp_ref[0:1, :])
        sh = jnp.where(is_s, shs_ref[...], shp_ref[0:1, :])
        h_ref[...] = (_ln(x1, LN_EPS) * (1.0 + sc) + sh).astype(BF16)


def _resid_call(x, mix, ada_p, mod_s, ln_g, ln_b, l, n_p, g_col, sc_col=None, sh_col=None):
    M, D = x.shape
    with_h = sc_col is not None
    cols = (g_col, sc_col, sh_col) if with_h else (g_col,)
    row = pl.BlockSpec((TM, D), lambda i: (i, 0))
    vec = pl.BlockSpec((None, 1, D), lambda i: (l, 0, 0))
    in_specs = [row, row] + _mod_specs(n_p, cols) + [vec, vec]
    args = [x, mix] + [ada_p] * len(cols) + [mod_s] * len(cols) + [ln_g, ln_b]
    out_shape = [jax.ShapeDtypeStruct((M, D), F32)]
    out_specs = [row]
    if with_h:
        out_shape.append(jax.ShapeDtypeStruct((M, D), BF16))
        out_specs.append(row)
    res = pl.pallas_call(
        functools.partial(_resid_kernel, n_p=n_p, with_h=with_h), grid=(M // TM,),
        in_specs=in_specs, out_specs=out_specs, out_shape=out_shape,
        compiler_params=_cp(("arbitrary",)), name="resid_h" if with_h else "resid")(*args)
    return res if with_h else res[0]


def _ffn_kernel(h_ref, wg_ref, wi_ref, o_ref, wgb_ref, wib_ref):
    @pl.when(pl.program_id(1) == 0)
    def _():
        wgb_ref[...] = wg_ref[...].astype(BF16)
        wib_ref[...] = wi_ref[...].astype(BF16)

    h = h_ref[...]
    a = _dot(h, wgb_ref[...])
    b = _dot(h, wib_ref[...])
    o_ref[...] = (a * _sigmoid(a) * b).astype(BF16)


def _ffn_call(h, wg, wi, l, tm, tn):
    M, K = h.shape
    N = wg.shape[-1]
    w_spec = pl.BlockSpec((None, K, tn), lambda j, i: (l, 0, j))
    return pl.pallas_call(
        _ffn_kernel, grid=(N // tn, M // tm),
        in_specs=[pl.BlockSpec((tm, K), lambda j, i: (i, 0)), w_spec, w_spec],
        out_specs=pl.BlockSpec((tm, tn), lambda j, i: (i, j)),
        out_shape=jax.ShapeDtypeStruct((M, N), BF16),
        scratch_shapes=[pltpu.VMEM((K, tn), BF16), pltpu.VMEM((K, tn), BF16)],
        compiler_params=_cp(("arbitrary", "arbitrary")), name="ffn_act")(h, wg, wi)


def _merge_kernel(ya_ref, yb_ref, yc_ref, yd_ref, wb_ref, g0_ref, g1_ref, g2_ref, g3_ref, o_ref, wbb_ref):
    @pl.when(pl.program_id(1) == 0)
    def _():
        wbb_ref[...] = wb_ref[...].astype(BF16)

    acc = None
    for n, (y_ref, g_ref) in enumerate(((ya_ref, g0_ref), (yb_ref, g1_ref), (yc_ref, g2_ref), (yd_ref, g3_ref))):
        up = _dot(y_ref[...].astype(BF16), wbb_ref[n])
        t = _sigmoid(g_ref[...]) * up
        acc = t if acc is None else acc + t
    o_ref[...] = acc.astype(BF16)


def _merge_call(ys, w_branch, z, l, tm, tn):
    M = z.shape[0]
    W = BRANCH_W
    D = D_MODEL
    nj = D // tn
    y_spec = pl.BlockSpec((tm, W), lambda j, i: (i, 0))
    g_specs = [pl.BlockSpec((tm, tn), lambda j, i, n=n: (i, (Z_G + n * D) // tn + j)) for n in range(4)]
    return pl.pallas_call(
        _merge_kernel, grid=(nj, M // tm),
        in_specs=[y_spec] * 4 + [pl.BlockSpec((None, 4, W, tn), lambda j, i: (l, 0, 0, j))] + g_specs,
        out_specs=pl.BlockSpec((tm, tn), lambda j, i: (i, j)),
        out_shape=jax.ShapeDtypeStruct((M, D), BF16),
        scratch_shapes=[pltpu.VMEM((4, W, tn), BF16)],
        compiler_params=_cp(("arbitrary", "arbitrary")), name="merge")(*ys, w_branch, z, z, z, z)


def _rwkv_pre_kernel(za_ref, prev_ref, fill_ref, mu_ref, w0_ref, a0_ref, kk_ref, ka_ref, w2_ref, a2_ref,
                     g2_ref, e_ref, r_o, lw_o, k_o, v_o, kn_o, b_o, g_o, *, n_p, seq_s):
    i = pl.program_id(0)
    is_s = i >= n_p
    za = za_ref[...]
    rows = _iota(za.shape, 0)
    prev = jnp.where(rows == 0, prev_ref[7:8, :], pltpu.roll(za, 1, axis=0))
    base = jnp.where(is_s, n_p * TM, 0)
    pmask = jnp.where(is_s, seq_s - 1, 0x3FFFFFFF)
    start = ((rows + (i * TM - base)) & pmask) == 0
    fill = jnp.where(is_s, fill_ref[...], 0.0)
    prev = jnp.where(start, fill, prev)
    zm = za + mu_ref[...] * (prev - za)
    W = BRANCH_W
    r = zm[:, 0:W]
    k = zm[:, W:2 * W]
    v = zm[:, 2 * W:3 * W]
    x = zm[:, 3 * W:4 * W]
    wl = _dot(jnp.tanh(x).astype(BF16), w2_ref[...].astype(BF16))
    y = -(w0_ref[...] + wl)
    softplus = jnp.maximum(y, 0.0) + jnp.log(1.0 + jnp.exp(-jnp.abs(y)))
    w = -softplus - 0.5
    logw = -jnp.exp(w)
    a = _sigmoid(a0_ref[...] + _dot(x.astype(BF16), a2_ref[...].astype(BF16)))
    g = _dot(_sigmoid(x).astype(BF16), g2_ref[...].astype(BF16))
    kk0 = k * kk_ref[...]
    ss = _dot_exact_r(kk0 * kk0, e_ref[...], n=3)
    kn = kk0 / jnp.maximum(jnp.sqrt(ss), 1e-12)
    k2 = k * (1.0 + (a - 1.0) * ka_ref[...])
    b = kn * a
    for h in range(A_HEADS):
        sl = slice(h * A_HD, (h + 1) * A_HD)
        r_o[h] = r[:, sl]
        lw_o[h] = logw[:, sl]
        k_o[h] = k2[:, sl]
        v_o[h] = v[:, sl]
        kn_o[h] = kn[:, sl]
        b_o[h] = b[:, sl]
        g_o[h] = g[:, sl]


def _rwkv_pre_call(z, fill_s, mu, w0, a0, k_k, k_a, w2p, a2p, g2p, e64, n_p, seq_s):
    M = z.shape[0]
    W = BRANCH_W
    cb = Z_A // 2048
    vec = lambda n: pl.BlockSpec((1, n), lambda i: (0, 0))
    mat = pl.BlockSpec((W, W), lambda i: (0, 0))
    out = jax.ShapeDtypeStruct((A_HEADS, M, A_HD), F32)
    ospec = pl.BlockSpec((A_HEADS, TM, A_HD), lambda i: (0, i, 0))
    return pl.pallas_call(
        functools.partial(_rwkv_pre_kernel, n_p=n_p, seq_s=seq_s), grid=(M // TM,),
        in_specs=[pl.BlockSpec((TM, 2048), lambda i: (i, cb)),
                  pl.BlockSpec((8, 2048), lambda i: (jnp.maximum(i * (TM // 8) - 1, 0), cb)),
                  pl.BlockSpec((TM, 2048), lambda i: (jnp.maximum(i - n_p, 0), 0)),
                  vec(2048), vec(W), vec(W), vec(W), vec(W), mat, mat, mat, mat],
        out_specs=[ospec] * 7, out_shape=[out] * 7,
        compiler_params=_cp(("arbitrary",)), name="rwkv_pre")(
            z, z, fill_s, mu, w0, a0, k_k, k_a, w2p, a2p, g2p, e64)


def _rwkv_chunk_kernel(r_ref, lw_ref, k_ref, v_ref, kn_ref, b_ref, g_ref, s0_ref, rk_ref, lng_ref, lnb_ref,
                       *rest, C, Q, nc):
    y_ref, sfin_ref, S_ref = rest[-3:]
    c = pl.program_id(1)

    @pl.when(c == 0)
    def _():
        S_ref[...] = s0_ref[...]

    H = range(A_HEADS)
    row = _iota((C, C), 0)
    col = _iota((C, C), 1)
    tri = (row >= col).astype(BF16)
    eye = (row == col).astype(F32)
    row2 = _iota((2 * C, C), 0)
    col2 = _iota((2 * C, C), 1)
    low2 = jnp.where(row2 < C, row2 - 1, row2 - C) >= col2
    eye_k = _iota((A_HD, A_HD), 0) == _iota((A_HD, A_HD), 1)
    HQ = [(h, q) for q in range(Q) for h in H]
    take = lambda ref: {(h, q): ref[h, q * C:(q + 1) * C, :] for h, q in HQ}
    r, lw, k, v, kn, b, g = (take(ref) for ref in (r_ref, lw_ref, k_ref, v_ref, kn_ref, b_ref, g_ref))
    L = {p: _dot_exact_l(tri, lw[p]) for p in HQ}
    eL = {p: jnp.exp(L[p]) for p in HQ}
    eN = {p: jnp.exp(-L[p]) for p in HQ}
    eE = {p: jnp.exp(L[p][C - 1:C, :] - L[p]) for p in HQ}
    lhs = {p: jnp.concatenate([kn[p] * jnp.exp(L[p] - lw[p]), r[p] * eL[p]], axis=0) for p in HQ}
    Ab = {p: jnp.where(low2, _dot1(lhs[p], b[p] * eN[p], NT), 0.0) for p in HQ}
    Ak = {p: jnp.where(low2, _dot1(lhs[p], k[p] * eN[p], NT), 0.0) for p in HQ}
    N = {p: Ab[p][0:C] for p in HQ}
    X = {p: eye - N[p] for p in HQ}
    P = {p: _dot1(N[p], N[p]) for p in HQ}
    n = 2
    while n < C:
        X = {p: X[p] + _dot1(X[p], P[p]) for p in HQ}
        n *= 2
        if n < C:
            P = {p: _dot1(P[p], P[p]) for p in HQ}
    AV = {p: _dot1(Ak[p], v[p]) for p in HQ}
    gcol = {p: jnp.sum(jnp.where(eye_k, jnp.broadcast_to(eL[p][C - 1:C, :], (A_HD, A_HD)), 0.0), axis=1,
                       keepdims=True) for p in HQ}
    kb = {p: jnp.concatenate([k[p] * eE[p], -(b[p] * eE[p])], axis=0) for p in HQ}
    bonus = {p: jnp.sum(r[p] * k[p] * rk_ref[p[0]], axis=1, keepdims=True) * v[p] for p in HQ}
    S = [S_ref[h] for h in H]
    rows = []
    for q in range(Q):
        KS = [_dot1(lhs[h, q], S[h]) for h in H]
        U = [_dot1(X[h, q], KS[h][0:C] + AV[h, q][0:C]) for h in H]
        Y = [KS[h][C:] + AV[h, q][C:] - _dot1(Ab[h, q][C:], U[h]) for h in H]
        S = [S[h] * gcol[h, q] + _dot1(kb[h, q], jnp.concatenate([v[h, q], U[h]], axis=0), TN) for h in H]
        rows.append(jnp.concatenate(
            [(_ln(Y[h], RWKV_GN_EPS) * lng_ref[h] + lnb_ref[h] + bonus[h, q]) * g[h, q] for h in H], axis=1))
    y_ref[...] = jnp.concatenate(rows, axis=0) if Q > 1 else rows[0]
    for h in H:
        S_ref[h] = S[h]

    @pl.when(c == nc - 1)
    def _():
        for h in H:
            sfin_ref[h] = S[h]


def _rwkv_chunk_call(pre, s0t, r_k, ln_g, ln_b, row0, B, T, C, Q, into):
    R = C * Q
    nc = T // R
    blk0 = row0 // R
    in_spec = pl.BlockSpec((A_HEADS, R, A_HD), lambda b, c: (0, blk0 + b * nc + c, 0))
    hvec = pl.BlockSpec((A_HEADS, 1, A_HD), lambda b, c: (0, 0, 0))
    st = pl.BlockSpec((None, A_HEADS, A_HD, A_HD), lambda b, c: (b, 0, 0, 0))
    return pl.pallas_call(
        functools.partial(_rwkv_chunk_kernel, C=C, Q=Q, nc=nc), grid=(B, nc),
        in_specs=[in_spec] * 7 + [st, hvec, hvec, hvec, pl.BlockSpec(memory_space=pl.ANY)],
        out_specs=[pl.BlockSpec((R, BRANCH_W), lambda b, c: (blk0 + b * nc + c, 0)), st],
        out_shape=[jax.ShapeDtypeStruct(into.shape, F32),
                   jax.ShapeDtypeStruct((B, A_HEADS, A_HD, A_HD), F32)],
        scratch_shapes=[pltpu.VMEM((A_HEADS, A_HD, A_HD), F32)],
        input_output_aliases={11: 0},
        compiler_params=_cp(("arbitrary", "arbitrary")), name="rwkv_chunk_c%d" % C)(
            *pre, s0t, r_k, ln_g, ln_b, into)


def _foxcum_kernel(fl_ref, bf_ref, lf_ref, cum_ref, carry_ref):
    @pl.when(pl.program_id(0) == 0)
    def _():
        carry_ref[...] = jnp.zeros_like(carry_ref)

    lane = _iota((TM, LANES), 1)
    lf = jnp.where(lane < NH, _log_sigmoid(fl_ref[...] + bf_ref[...]), 0.0)
    tri = (_iota((TM, TM), 0) >= _iota((TM, TM), 1)).astype(BF16)
    cum = _dot_exact_l(tri, lf) + carry_ref[...]
    lf_ref[...] = lf
    cum_ref[...] = cum
    carry_ref[...] = cum[TM - 1:TM, :]


def _foxcum_call(z, bf):
    M = z.shape[0]
    cb = (Z_B + 3 * BRANCH_W) // LANES
    spec = pl.BlockSpec((TM, LANES), lambda i: (i, 0))
    return pl.pallas_call(
        _foxcum_kernel, grid=(M // TM,),
        in_specs=[pl.BlockSpec((TM, LANES), lambda i: (i, cb)), pl.BlockSpec((1, LANES), lambda i: (0, 0))],
        out_specs=[spec, spec], out_shape=[jax.ShapeDtypeStruct((M, LANES), F32)] * 2,
        scratch_shapes=[pltpu.VMEM((1, LANES), F32)],
        compiler_params=_cp(("arbitrary",)), name="fox_cum")(z, bf)


def _attn_prep_kernel(fq_ref, fk_ref, fv_ref, mq_ref, mk_ref, mv_ref, cum_ref, cprev_ref, cqprev_ref, km_ref,
                      fqT_o, fkA_o, fvT_o, mqT_o, mk_o, mvT_o, sel_o, fk_rows, fv_rows, mk_rows, mv_rows, *, scale):
    i = pl.program_id(0)
    lane = _iota((TM, LANES), 1)
    cum = cum_ref[...]
    qscale = scale * LOG2E
    crel_k = (cum - jnp.where(i == 0, 0.0, cprev_ref[7:8, :])) * LOG2E
    crel_q = (cum - jnp.where(i < FLASH_TQ // TM, 0.0, cqprev_ref[7:8, :])) * LOG2E
    fq = fq_ref[...]
    fk = fk_ref[...]
    fv = fv_ref[...]
    mq = mq_ref[...]
    mk = mk_ref[...]
    mv = mv_ref[...]
    km = km_ref[...]
    blk = _iota((km.shape[0], TM), 0)
    blk_f = blk.astype(F32)
    ones_rows = jnp.where(_iota((VROWS - HD, TM), 0) == 0, 1.0, 0.0).astype(BF16)
    for h in range(NH):
        sl = slice(h * HD, (h + 1) * HD)
        q3 = [p.astype(F32) for p in _split3(jnp.broadcast_to(crel_q[:, h:h + 1], (TM, LANES)))]
        k3 = [p.astype(F32) for p in _split3(jnp.broadcast_to(crel_k[:, h:h + 1], (TM, LANES)))]
        qb = jnp.where(lane == 0, q3[0], jnp.where(lane == 1, q3[1], jnp.where(lane == 2, q3[2],
                                                                             jnp.where(lane < 6, 1.0, 0.0))))
        kb = jnp.where(lane < 3, 1.0, jnp.where(lane == 3, -k3[0], jnp.where(lane == 4, -k3[1],
                                                                              jnp.where(lane == 5, -k3[2], 0.0))))
        fqT_o[h, 0:HD, :] = (fq[:, sl] * qscale).T.astype(BF16)
        fqT_o[h, HD:2 * HD, :] = qb.T.astype(BF16)
        for rows_o, src in ((fk_rows, fk), (fv_rows, fv), (mk_rows, mk), (mv_rows, mv)):
            rows_o[pl.ds(h, TM, stride=NH), :] = src[:, sl]
        fkA_o[h, :, 0:HD] = fk[:, sl].astype(BF16)
        fkA_o[h, :, HD:2 * HD] = kb.astype(BF16)
        fvT_o[h, 0:HD, :] = fv[:, sl].T.astype(BF16)
        fvT_o[h, HD:VROWS, :] = ones_rows
        mqT = mq[:, sl].T
        mqT_o[h] = (mqT * qscale).astype(BF16)
        mk_o[h] = mk[:, sl].astype(BF16)
        mvT_o[h, 0:HD, :] = mv[:, sl].T.astype(BF16)
        mvT_o[h, HD:VROWS, :] = ones_rows
        sc = _dot3(km[:, sl], mqT)
        sel_o[h] = _top3_select(sc, blk < i, blk_f, axis=0)


def _attn_prep_call(z, cum, kmean, Tp):
    W = BRANCH_W
    nbp = kmean.shape[0]
    cbB = Z_B // W
    cbC = Z_C // W
    rq = FLASH_TQ // TM
    zs = lambda cb: pl.BlockSpec((TM, W), lambda i: (i, cb))
    colT = lambda rows: pl.BlockSpec((NH, rows, TM), lambda i: (0, 0, i))
    rowm = lambda cols: pl.BlockSpec((NH, TM, cols), lambda i: (0, i, 0))
    sd = jax.ShapeDtypeStruct
    return pl.pallas_call(
        functools.partial(_attn_prep_kernel, scale=HD ** -0.5), grid=(Tp // TM,),
        in_specs=[zs(cbB), zs(cbB + 1), zs(cbB + 2), zs(cbC), zs(cbC + 1), zs(cbC + 2),
                  pl.BlockSpec((TM, LANES), lambda i: (i, 0)),
                  pl.BlockSpec((8, LANES), lambda i: (jnp.maximum(i * (TM // 8) - 1, 0), 0)),
                  pl.BlockSpec((8, LANES), lambda i: (jnp.maximum((i // rq) * (FLASH_TQ // 8) - 1, 0), 0)),
                  pl.BlockSpec((nbp, W), lambda i: (0, 0))],
        out_specs=[colT(2 * HD), rowm(2 * HD), colT(VROWS), colT(HD), rowm(HD), colT(VROWS), colT(nbp)]
        + [pl.BlockSpec((TM * NH, HD), lambda i: (i, 0))] * 4,
        out_shape=[sd((NH, 2 * HD, Tp), BF16), sd((NH, Tp, 2 * HD), BF16), sd((NH, VROWS, Tp), BF16),
                   sd((NH, HD, Tp), BF16), sd((NH, Tp, HD), BF16), sd((NH, VROWS, Tp), BF16),
                   sd((NH, nbp, Tp), F32)] + [sd((Tp * NH, HD), F32)] * 4,
        compiler_params=_cp(("arbitrary",)), name="attn_prep")(z, z, z, z, z, z, cum, cum, cum, kmean)


def _flash_kernel(ti_ref, tj_ref, bq_ref, bk_ref, qT_ref, kA_ref, vT_ref, *rest, moba):
    sel_ref = rest[0] if moba else None
    o_ref, m_ref, acc_ref = rest[-3:]
    step = pl.program_id(0)
    i = ti_ref[step]
    j = tj_ref[step]
    T = kA_ref.shape[1]
    Tq = qT_ref.shape[2]
    ratio = Tq // T
    nqt = bq_ref.shape[0] // NH
    nkt = bk_ref.shape[0] // NH

    @pl.when(j == 0)
    def _():
        m_ref[...] = jnp.full(m_ref.shape, NEG, F32)
        acc_ref[...] = jnp.zeros(acc_ref.shape, F32)

    def step_body(on_diagonal):
        H = range(NH)
        sT = {h: _dot(kA_ref[h], qT_ref[h]) for h in H}
        if on_diagonal:
            causal = (_iota((T, Tq), 0) - _iota((T, Tq), 1)) <= (i * Tq - j * T)
            sT = {h: jnp.where(causal, sT[h], NEG) for h in H}
        if moba:
            own = (_iota((1, Tq), 1) // T + i * ratio) == j
            sT = {h: sT[h] + (jnp.where(own, 1.0, sel_ref[h, pl.ds(j, 1), :]) - 1.0) * (-NEG) for h in H}
            off = {h: 0.0 for h in H}
        else:
            off = {h: (bq_ref[h * nqt + i] - bk_ref[h * nkt + j]) * LOG2E for h in H}
        m_prev = {h: m_ref[h] for h in H}
        m_new = {h: jnp.maximum(m_prev[h], jnp.max(sT[h], axis=0, keepdims=True) + off[h]) for h in H}
        p = {h: jnp.exp2((sT[h] - (m_new[h] - off[h])).astype(BF16)) for h in H}
        alpha = {h: jnp.exp2(m_prev[h] - m_new[h]) for h in H}
        pv = {h: _dot(vT_ref[h], p[h]) for h in H}
        for h in H:
            acc_ref[h] = alpha[h] * acc_ref[h] + pv[h]
            m_ref[h] = m_new[h]

    crosses = (j + 1) * T - 1 > i * Tq
    pl.when(crosses)(functools.partial(step_body, True))
    pl.when(jnp.logical_not(crosses))(functools.partial(step_body, False))

    @pl.when(j == (i + 1) * ratio - 1)
    def _():
        for h in range(NH):
            acc = acc_ref[h]
            o_ref[:, h * HD:(h + 1) * HD] = (acc[0:HD] / acc[HD:HD + 1]).T


def _flash_call(qT, kA, vT, base_q, base_k, sel, Tp, into, name):
    T = MOBA_BLOCK
    Tq = FLASH_TQ
    ratio = Tq // T
    Kc = kA.shape[2]
    moba = sel is not None
    ti = np.concatenate([np.full((i + 1) * ratio, i, np.int32) for i in range(Tp // Tq)])
    tj = np.concatenate([np.arange((i + 1) * ratio, dtype=np.int32) for i in range(Tp // Tq)])
    in_specs = [pl.BlockSpec((NH, Kc, Tq), lambda s, ti, tj, bq, bk: (0, 0, ti[s])),
                pl.BlockSpec((NH, T, Kc), lambda s, ti, tj, bq, bk: (0, tj[s], 0)),
                pl.BlockSpec((NH, VROWS, T), lambda s, ti, tj, bq, bk: (0, 0, tj[s]))]
    args = [qT, kA, vT]
    if moba:
        in_specs.append(pl.BlockSpec((NH, sel.shape[1], Tq), lambda s, ti, tj, bq, bk: (0, 0, ti[s])))
        args.append(sel)
    in_specs.append(pl.BlockSpec(memory_space=pl.ANY))
    return pl.pallas_call(
        functools.partial(_flash_kernel, moba=moba),
        grid_spec=pltpu.PrefetchScalarGridSpec(
            num_scalar_prefetch=4, grid=(len(ti),), in_specs=in_specs,
            out_specs=pl.BlockSpec((Tq, NH * HD), lambda s, ti, tj, bq, bk: (ti[s], 0)),
            scratch_shapes=[pltpu.VMEM((NH, 1, Tq), F32), pltpu.VMEM((NH, VROWS, Tq), F32)]),
        out_shape=jax.ShapeDtypeStruct(into.shape, F32),
        input_output_aliases={4 + len(args): 0},
        compiler_params=_cp(("arbitrary",)), name=name)(
            jnp.asarray(ti), jnp.asarray(tj), base_q, base_k, *args, into)


def _top3_select(sc, valid, idx_f, axis=1):
    sc = jnp.where(valid, sc, -jnp.inf)
    sel = jnp.zeros(sc.shape, F32)
    for _ in range(MOBA_TOPK):
        mx = jnp.max(sc, axis=axis, keepdims=True)
        idx = jnp.min(jnp.where(sc == mx, idx_f, 1e9), axis=axis, keepdims=True)
        hit = idx_f == idx
        sel = jnp.where(hit & valid, 1.0, sel)
        sc = jnp.where(hit, -jnp.inf, sc)
    return sel


def _blockmean_kernel(k_ref, o_ref):
    o_ref[...] = jnp.sum(k_ref[...], axis=0, keepdims=True) * (1.0 / MOBA_BLOCK)


def _blockmean_call(z, Tp):
    nb = Tp // MOBA_BLOCK
    W = BRANCH_W
    cb = Z_C // W + 1
    return pl.pallas_call(
        _blockmean_kernel, grid=(nb,),
        in_specs=[pl.BlockSpec((MOBA_BLOCK, W), lambda i: (i, cb))],
        out_specs=pl.BlockSpec((None, 1, W), lambda i: (i, 0, 0)),
        out_shape=jax.ShapeDtypeStruct((nb, 1, W), F32),
        compiler_params=_cp(("arbitrary",)), name="moba_kmean")(z)


def _ret_kernel(q_ref, k_ref, v_ref, gd_ref, cos_ref, sin_ref, din_ref, qd_ref, kd_ref, cd_ref, s0_ref, lng_ref,
                *rest, nc):
    y_ref, sfin_ref, S_ref = rest[-3:]
    c = pl.program_id(1)

    @pl.when(c == 0)
    def _():
        S_ref[...] = s0_ref[...]

    q = q_ref[...]
    k = k_ref[...]
    v = v_ref[...]
    gd = gd_ref[...]
    cos = cos_ref[...]
    sin = sin_ref[...]
    lng = lng_ref[...]
    for h in range(NH):
        sl = slice(h * HD, (h + 1) * HD)
        qh = q[:, sl]
        kh = k[:, sl]
        qr = qh * cos + pltpu.roll(qh, HD // 2, axis=1) * sin
        kr = (kh * cos + pltpu.roll(kh, HD // 2, axis=1) * sin) * (HD ** -0.5)
        vb = v[:, sl].astype(BF16)
        qb = qr.astype(BF16)
        att = _dot(qb, kr.astype(BF16), NT) * din_ref[h]
        S = S_ref[h]
        o = _dot(att.astype(BF16), vb) + _dot(qb, S.astype(BF16)) * qd_ref[h]
        S_ref[h] = S * cd_ref[h] + _dot((kr * kd_ref[h]).astype(BF16), vb, TN)
        y_ref[:, sl] = _ln(o, RET_GN_EPS) * lng[:, sl] * (gd[:, sl] * _sigmoid(gd[:, sl]))

    @pl.when(c == nc - 1)
    def _():
        sfin_ref[...] = S_ref[...]


def _ret_tables(C):
    lg = np.log(1.0 - 2.0 ** (-5.0 - np.arange(NH, dtype=np.float32))).astype(np.float32)
    i = np.arange(C, dtype=np.float32)
    diff = i[:, None] - i[None, :]
    din = np.where(diff[None] >= 0, np.exp(np.maximum(diff, 0.0)[None] * lg[:, None, None]), 0.0)
    qd = np.exp((i[None, :] + 1.0) * lg[:, None])
    kd = np.exp((C - 1.0 - i)[None, :] * lg[:, None])
    cd = np.exp(C * lg)
    bc = lambda t: np.ascontiguousarray(np.broadcast_to(t[:, :, None], (NH, C, HD))).astype(np.float32)
    cdb = np.ascontiguousarray(np.broadcast_to(cd[:, None, None], (NH, 1, HD))).astype(np.float32)
    return din.astype(np.float32), bc(qd), bc(kd), cdb


def _rope_tables(pos0, T):
    half = HD // 2
    inv = 1.0 / (ROPE_BASE ** (jnp.arange(half, dtype=F32) / half))
    ang = (pos0 + jnp.arange(T)).astype(F32)[:, None] * inv[None, :]
    cos = jnp.cos(ang)
    sin = jnp.sin(ang)
    return jnp.concatenate([cos, cos], axis=1), jnp.concatenate([-sin, sin], axis=1)


def _ret_call(z, s0, ln_g, l, row0, B, T, C, pos0, into):
    nc = T // C
    W = BRANCH_W
    cb = Z_D // W
    blk0 = row0 // C
    cos, sin = _rope_tables(pos0, T)
    din, qd, kd, cd = _ret_tables(C)
    zs = lambda off: pl.BlockSpec((C, W), lambda b, c: (blk0 + b * nc + c, cb + off))
    tab = pl.BlockSpec((C, HD), lambda b, c: (c, 0))
    full = lambda shape: pl.BlockSpec(shape, lambda b, c: (0,) * len(shape))
    st = pl.BlockSpec((None, NH, HD, HD), lambda b, c: (b, 0, 0, 0))
    return pl.pallas_call(
        functools.partial(_ret_kernel, nc=nc), grid=(B, nc),
        in_specs=[zs(0), zs(1), zs(2), zs(3), tab, tab, full((NH, C, C)), full((NH, C, HD)), full((NH, C, HD)),
                  full((NH, 1, HD)), st, pl.BlockSpec((None, 1, W), lambda b, c: (l, 0, 0)),
                  pl.BlockSpec(memory_space=pl.ANY)],
        out_specs=[pl.BlockSpec((C, W), lambda b, c: (blk0 + b * nc + c, 0)), st],
        out_shape=[jax.ShapeDtypeStruct(into.shape, F32), jax.ShapeDtypeStruct((B, NH, HD, HD), F32)],
        scratch_shapes=[pltpu.VMEM((NH, HD, HD), F32)],
        input_output_aliases={12: 0},
        compiler_params=_cp(("arbitrary", "arbitrary")), name="ret_c%d" % C)(
            z, z, z, z, cos, sin, din, qd, kd, cd, s0, ln_g, into)


def _lfsuf_kernel(pt_ref, lf_hbm, ts_ref, tot_ref, o_ref, buf, sem, *, npg, base):
    b = pl.program_id(0)

    def page_copy(p):
        return pltpu.make_async_copy(lf_hbm.at[pl.ds(NH * (base + pt_ref[b * npg + p]), NH), :],
                                     buf.at[pl.ds(NH * p, NH), :], sem.at[p])

    for p in range(npg):
        page_copy(p).start()
    for p in range(npg):
        page_copy(p).wait()
    loc = None
    tot = None
    for h in range(NH):
        x = buf[pl.ds(h, npg, stride=NH), :]
        a = _dot_exact_r(x, ts_ref[h])
        t = _dot_exact_r(x, tot_ref[h])
        loc = a if loc is None else loc + a
        tot = t if tot is None else tot + t
    later = (_iota((npg, npg), 1) > _iota((npg, npg), 0)).astype(BF16)
    o_ref[...] = loc + _dot_exact_l(later, tot)


def _lfsuf_call(pt_flat, lf_rows, ts, tot, B, npg, base):
    PW = PAGE_SIZE * NH
    sel = pl.BlockSpec((NH, PAGE_SIZE, PW), lambda b, pt: (0, 0, 0))
    return pl.pallas_call(
        functools.partial(_lfsuf_kernel, npg=npg, base=base),
        grid_spec=pltpu.PrefetchScalarGridSpec(
            num_scalar_prefetch=1, grid=(B,),
            in_specs=[pl.BlockSpec(memory_space=pl.ANY), sel, sel],
            out_specs=pl.BlockSpec((None, npg, PW), lambda b, pt: (b, 0, 0)),
            scratch_shapes=[pltpu.VMEM((NH * npg, PAGE_SIZE), F32), pltpu.SemaphoreType.DMA((npg,))]),
        out_shape=jax.ShapeDtypeStruct((B, npg, PW), F32),
        compiler_params=_cp(("arbitrary",)), name="fox_logf_suffix")(pt_flat, lf_rows, ts, tot)


def _suffix_matrices():
    r = np.arange(PAGE_SIZE)
    dst_r = np.repeat(r, NH)
    dst_h = np.tile(np.arange(NH), PAGE_SIZE)
    same = np.arange(NH)[:, None, None] == dst_h[None, None, :]
    ts = same & (r[None, :, None] > dst_r[None, None, :])
    return jnp.asarray(ts, BF16), jnp.asarray(np.broadcast_to(same, ts.shape), BF16)


def _sattn_kernel(pt_ref, fq_ref, fkn_ref, fvn_ref, lfn_ref, mq_ref, mkn_ref, mvn_ref, R_ref, *rest, nb, S, scale):
    P = SAMPLE_PAGES
    fk, fv, mk, mv = (rest[t * P:(t + 1) * P] for t in range(4))
    yb_ref, yc_ref, fqb, mqb, mqf, cnb, fm, fl, facc, sc_all, m_all, l_all, o_all = rest[-13:]
    n = pl.program_id(1)
    R4 = NH * S
    PW = PAGE_SIZE * NH
    own_head = (_iota((R4, PW), 1) % NH) == (_iota((R4, PW), 0) // S)
    lane = _iota((R4, LANES), 1)

    def stack_heads(x):
        return jnp.concatenate([x[:, h * HD:(h + 1) * HD] for h in range(NH)], axis=0)

    def head_sums(kpage):
        return jnp.sum(kpage.reshape(PW // 8, 8, HD), axis=0)

    @pl.when(n == 0)
    def _():
        fqb[...] = stack_heads(fq_ref[...]).astype(BF16)
        mq = stack_heads(mq_ref[...])
        mqb[...] = mq.astype(BF16)
        mqf[...] = mq
        tri = (_iota((S, S), 0) >= _iota((S, S), 1)).astype(BF16)
        cn = _dot_exact_l(tri, lfn_ref[...])
        cnb[...] = jnp.concatenate([jnp.broadcast_to(cn[:, h:h + 1], (S, LANES)) for h in range(NH)], axis=0)
        fm[...] = jnp.full(fm.shape, NEG, F32)
        fl[...] = jnp.zeros(fl.shape, F32)
        facc[...] = jnp.zeros(facc.shape, F32)
        sc_all[...] = jnp.full(sc_all.shape, -jnp.inf, F32)
        m_all[...] = jnp.full(m_all.shape, NEG, F32)
        l_all[...] = jnp.zeros(l_all.shape, F32)

    def fox_update(s_list, v_list):
        m_prev = fm[...]
        m_new = m_prev
        for s in s_list:
            m_new = jnp.maximum(m_new, jnp.max(s, axis=1, keepdims=True))
        alpha = jnp.exp(m_prev - m_new)
        p_list = [jnp.exp(s - m_new) for s in s_list]
        l_new = alpha * fl[...]
        acc = alpha * facc[...]
        for p, v_bf in zip(p_list, v_list):
            l_new = l_new + jnp.sum(p, axis=1, keepdims=True)
            acc = acc + _dot(p.astype(BF16), v_bf)
        fl[...] = l_new
        facc[...] = acc
        fm[...] = m_new

    cn_b = cnb[...]
    cn_w = jnp.concatenate([cn_b] * NH, axis=1)
    fq_b = fqb[...]
    s_fox = [_dot(fq_b, fk[g][...].astype(BF16), NT) * scale for g in range(P)]
    s_fox = [jnp.where(own_head, s_fox[g] + cn_w + R_ref[pl.ds(P * n + g, 1), :], NEG) for g in range(P)]
    fox_update(s_fox, [fv[g][...].astype(BF16) for g in range(P)])

    mq_b = mqb[...]
    mq_f = mqf[...]
    kpg = [mk[g][...] for g in range(P)]
    s_mo = [jnp.where(own_head, _dot(mq_b, kpg[g].astype(BF16), NT) * scale, NEG) for g in range(P)]
    for blk in range(P // 2):
        g0, g1 = 2 * blk, 2 * blk + 1
        ks = head_sums(kpg[g0]) + head_sums(kpg[g1])
        kmean = (ks[0:NH] + ks[NH:2 * NH]) * (1.0 / MOBA_BLOCK)
        kmean = jnp.concatenate([jnp.broadcast_to(kmean[h:h + 1], (S, HD)) for h in range(NH)], axis=0)
        sc_col = jnp.sum(mq_f * kmean, axis=1, keepdims=True)
        m_b = jnp.maximum(jnp.max(s_mo[g0], axis=1, keepdims=True), jnp.max(s_mo[g1], axis=1, keepdims=True))
        p0 = jnp.exp(s_mo[g0] - m_b)
        p1 = jnp.exp(s_mo[g1] - m_b)
        here = lane == (P // 2) * n + blk
        sc_all[...] = jnp.where(here, sc_col, sc_all[...])
        m_all[...] = jnp.where(here, m_b, m_all[...])
        l_all[...] = jnp.where(here, jnp.sum(p0, axis=1, keepdims=True) + jnp.sum(p1, axis=1, keepdims=True),
                               l_all[...])
        o_all[(P // 2) * n + blk] = (_dot(p0.astype(BF16), mv[g0][...].astype(BF16))
                                     + _dot(p1.astype(BF16), mv[g1][...].astype(BF16)))

    @pl.when(n == pl.num_programs(1) - 1)
    def _():
        ri = _iota((R4, R4), 0)
        ci = _iota((R4, R4), 1)
        causal = ((ri // S) == (ci // S)) & ((ci % S) <= (ri % S))
        cn_col = cn_b[:, 0:1]
        cn_row = jnp.sum(jnp.where(ri == ci, jnp.broadcast_to(cn_col, (R4, R4)), 0.0), axis=0, keepdims=True)
        s = _dot(fqb[...], stack_heads(fkn_ref[...]).astype(BF16), NT) * scale
        s = s + (cn_col - cn_row)
        fox_update([jnp.where(causal, s, NEG)], [stack_heads(fvn_ref[...]).astype(BF16)])
        out = facc[...] / fl[...]
        for h in range(NH):
            yb_ref[:, h * HD:(h + 1) * HD] = out[h * S:(h + 1) * S, :]

        s = _dot(mqb[...], stack_heads(mkn_ref[...]).astype(BF16), NT) * scale
        s = jnp.where(causal, s, NEG)
        m_o = jnp.max(s, axis=1, keepdims=True)
        p = jnp.exp(s - m_o)
        l_o = jnp.sum(p, axis=1, keepdims=True)
        o_o = _dot(p.astype(BF16), stack_heads(mvn_ref[...]).astype(BF16))
        sel = _top3_select(sc_all[...], lane < nb, lane.astype(F32)) > 0.5
        m_past = m_all[...]
        m_tot = jnp.maximum(jnp.max(jnp.where(sel, m_past, NEG), axis=1, keepdims=True), m_o)
        w = jnp.where(sel, jnp.exp(m_past - m_tot), 0.0)
        w_o = jnp.exp(m_o - m_tot)
        den = jnp.sum(w * l_all[...], axis=1, keepdims=True) + w_o * l_o
        num = w_o * o_o
        for blk in range(nb):
            num = num + w[:, blk:blk + 1] * o_all[blk]
        out = num / den
        for h in range(NH):
            yc_ref[:, h * HD:(h + 1) * HD] = out[h * S:(h + 1) * S, :]


def _sattn_call(pt_flat, z, lf, R, caches, l, Mp, B, S, npg, n_pool, yb_into, yc_into):
    fk, fv, mk, mv = caches
    W = BRANCH_W
    PW = PAGE_SIZE * NH
    nb = npg // 2
    R4 = NH * S
    rb = Mp // S
    base = l * n_pool
    zrow = lambda cb: pl.BlockSpec((S, W), lambda b, n, pt: (rb + b, cb))
    P = SAMPLE_PAGES
    page = lambda g: pl.BlockSpec((PW, HD), lambda b, n, pt: (base + pt[b * npg + P * n + g], 0))
    pages = [page(g) for g in range(P)]
    cbB = Z_B // W
    cbC = Z_C // W
    in_specs = [zrow(cbB), zrow(cbB + 1), zrow(cbB + 2),
                pl.BlockSpec((S, LANES), lambda b, n, pt: (rb + b, 0)),
                zrow(cbC), zrow(cbC + 1), zrow(cbC + 2),
                pl.BlockSpec((None, npg, PW), lambda b, n, pt: (b, 0, 0))] + pages * 4
    in_specs += [pl.BlockSpec(memory_space=pl.ANY)] * 2
    n_in = 1 + len(in_specs)
    out_spec = pl.BlockSpec((S, W), lambda b, n, pt: (rb + b, 0))
    scratch = [pltpu.VMEM((R4, HD), BF16), pltpu.VMEM((R4, HD), BF16), pltpu.VMEM((R4, HD), F32),
               pltpu.VMEM((R4, LANES), F32),
               pltpu.VMEM((R4, 1), F32), pltpu.VMEM((R4, 1), F32), pltpu.VMEM((R4, HD), F32),
               pltpu.VMEM((R4, LANES), F32), pltpu.VMEM((R4, LANES), F32), pltpu.VMEM((R4, LANES), F32),
               pltpu.VMEM((nb, R4, HD), F32)]
    return pl.pallas_call(
        functools.partial(_sattn_kernel, nb=nb, S=S, scale=HD ** -0.5),
        grid_spec=pltpu.PrefetchScalarGridSpec(
            num_scalar_prefetch=1, grid=(B, npg // P), in_specs=in_specs,
            out_specs=[out_spec, out_spec], scratch_shapes=scratch),
        out_shape=[jax.ShapeDtypeStruct(yb_into.shape, F32), jax.ShapeDtypeStruct(yc_into.shape, F32)],
        input_output_aliases={n_in - 2: 0, n_in - 1: 1},
        compiler_params=_cp(("arbitrary", "arbitrary")), name="sample_attn")(
            pt_flat, z, z, z, lf, z, z, z, R, *([fk] * P + [fv] * P + [mk] * P + [mv] * P), yb_into, yc_into)


def _pick_tile(m, cands):
    for t in cands:
        if m % t == 0:
            return t
    raise ValueError("no row tile divides %d" % m)


def _pad_cols(x, n):
    return jnp.pad(x, ((0, 0), (0, n - x.shape[1])))


def kernel(x_prompt, x_sample, c_prompt, c_sample, page_table, cache_fox_k, cache_fox_v, cache_fox_logf, cache_moba_k, cache_moba_v, state_rwkv, state_rwkv_shift, state_ret, w_ada, b_ada, w_in, rwkv_mu, rwkv_w0, rwkv_w2, rwkv_a0, rwkv_a2, rwkv_g2, rwkv_k_k, rwkv_k_a, rwkv_r_k, rwkv_ln_g, rwkv_ln_b, fox_bf, ret_ln_g, w_branch, w_o, ln1_g, ln1_b, ln2_g, ln2_b, w_ffn_gate, w_ffn_in, w_ffn_out):
    Bp, Tp, D = x_prompt.shape
    Bs, S, _ = x_sample.shape
    depth = w_in.shape[0]
    n_pool = cache_fox_k.shape[1]
    npg = page_table.shape[1]
    past = npg * PAGE_SIZE
    Mp, Ms = Bp * Tp, Bs * S
    M = Mp + Ms
    assert Bp == 1 and D == D_MODEL and depth == DEPTH
    assert Mp % TM == 0 and Ms % TM == 0 and Tp % MOBA_BLOCK == 0 and Tp // MOBA_BLOCK <= LANES
    assert Tp % (RWKV_CHUNK * RWKV_CHUNKS_PER_STEP) == 0 and Tp % RET_CHUNK == 0 and S == 8 and past % MOBA_BLOCK == 0
    assert TM == MOBA_BLOCK and 2 * NH == 8 and MOBA_BLOCK == 2 * PAGE_SIZE
    assert Tp % FLASH_TQ == 0 and FLASH_TQ % MOBA_BLOCK == 0
    assert npg % SAMPLE_PAGES == 0 and SAMPLE_PAGES % 2 == 0 and npg // 2 <= LANES
    n_p = Mp // TM
    W = BRANCH_W
    tm_big = _pick_tile(M, (768, 512, 256))
    tm_small = _pick_tile(M, (384, 256))

    x = jnp.concatenate([x_prompt.reshape(Mp, D), x_sample.reshape(Ms, D)], axis=0)
    c_rows = 8 * (-(-(Bp + Bs) // 8))
    c_all = jnp.pad(jnp.concatenate([c_prompt, c_sample], axis=0), ((0, c_rows - Bp - Bs), (0, 0)))
    pt_flat = page_table.reshape(-1).astype(jnp.int32)
    cache2 = lambda t: t.reshape(depth * n_pool * PAGE_SIZE * NH, HD)
    caches = (cache2(cache_fox_k), cache2(cache_fox_v), cache2(cache_moba_k), cache2(cache_moba_v))
    lf_rows = jnp.swapaxes(cache_fox_logf, 2, 3).reshape(depth * n_pool * NH, PAGE_SIZE)
    ts_m, tot_m = _suffix_matrices()
    e64 = jnp.asarray(np.kron(np.eye(A_HEADS), np.ones((A_HD, A_HD))), BF16)
    b_ada3 = b_ada.reshape(depth, 1, -1)
    vec3 = lambda t: t.reshape(depth, 1, -1)
    ln1_g3, ln1_b3, ln2_g3, ln2_b3, ret_g3 = map(vec3, (ln1_g, ln1_b, ln2_g, ln2_b, ret_ln_g))

    assert w_in.shape[2] == _O_G + 4 * D_MODEL
    tm_in = _pick_tile(M, (768, 512, 256))

    per_layer = []
    for l in range(depth):
        ada = _ada_call(c_all, w_ada, b_ada3, l)
        ada_p = ada[0:8]
        mod_s = jnp.repeat(ada[Bp:Bp + Bs], S, axis=0)
        h = _lnmod_call(x, ada_p, mod_s, n_p, 1, 0)
        z = _win_call(h, w_in, l, tm_in)

        fill_s = jnp.repeat(_pad_cols(state_rwkv_shift[l], 2048), S, axis=0)
        pad_rows = lambda w2, r0: jnp.zeros((W, W), F32).at[r0:r0 + w2.shape[0]].set(w2)
        pre = _rwkv_pre_call(
            z, fill_s, _pad_cols(rwkv_mu[l][None], 2048), rwkv_w0[l][None], rwkv_a0[l][None],
            rwkv_k_k[l][None], rwkv_k_a[l][None], pad_rows(rwkv_w2[l], 0), pad_rows(rwkv_a2[l], A_LORA_W),
            pad_rows(rwkv_g2[l], A_LORA_W + A_LORA_A), e64, n_p, S)
        hv = lambda t: t.reshape(A_HEADS, 1, A_HD)
        rk, lg, lb = hv(rwkv_r_k[l]), hv(rwkv_ln_g[l]), hv(rwkv_ln_b[l])
        blank = jnp.zeros((M, W), F32)
        ya, sT_p = _rwkv_chunk_call(pre, jnp.zeros((Bp, A_HEADS, A_HD, A_HD), F32), rk, lg, lb,
                                    0, Bp, Tp, RWKV_CHUNK, RWKV_CHUNKS_PER_STEP, blank)
        ya, sT_s = _rwkv_chunk_call(pre, jnp.swapaxes(state_rwkv[l], -1, -2), rk, lg, lb, Mp, Bs, S, S, 1, ya)

        lf, cum = _foxcum_call(z, _pad_cols(fox_bf[l][None], LANES))
        nbk = Tp // MOBA_BLOCK
        kmean = _blockmean_call(z, Tp).reshape(nbk, W)
        kmean = jnp.pad(kmean, ((0, 8 * (-(-nbk // 8)) - nbk), (0, 0)))
        fqT, fkA, fvT, mqT, mkb, mvT, sel, fk_rows, fv_rows, mk_rows, mv_rows = _attn_prep_call(z, cum, kmean, Tp)
        tile_base = lambda t: jnp.transpose(jnp.concatenate(
            [jnp.zeros((1, NH), F32), cum[t - 1:Mp - 1:t, :NH]], axis=0)).reshape(-1)
        base_q, base_k = tile_base(FLASH_TQ), tile_base(MOBA_BLOCK)
        yb = _flash_call(fqT, fkA, fvT, base_q, base_k, None, Tp, blank, "fox_prompt")
        yc = _flash_call(mqT, mkb, mvT, base_q, base_k, sel, Tp, blank, "moba_prompt")
        R = _lfsuf_call(pt_flat, lf_rows, ts_m, tot_m, Bs, npg, l * n_pool)
        yb, yc = _sattn_call(pt_flat, z, lf, R, caches, l, Mp, Bs, S, npg, n_pool, yb, yc)

        yd, rS_p = _ret_call(z, jnp.zeros((Bp, NH, HD, HD), F32), ret_g3, l, 0, Bp, Tp, RET_CHUNK, 0, blank)
        yd, rS_s = _ret_call(z, state_ret[l], ret_g3, l, Mp, Bs, S, S, past, yd)

        mixin = _merge_call((ya, yb, yc, yd), w_branch, z, l, tm_big, 512)
        mix = _mm_call(mixin, w_o, l, tm_big, 512, F32, "w_o")
        x1, h2 = _resid_call(x, mix, ada_p, mod_s, ln1_g3, ln1_b3, l, n_p, 2, 4, 3)
        act = _ffn_call(h2, w_ffn_gate, w_ffn_in, l, tm_big, 512)
        f = _mm_call(act, w_ffn_out, l, tm_small, 512, F32, "ffn_out")
        x = _resid_call(x1, f, ada_p, mod_s, ln2_g3, ln2_b3, l, n_p, 5)

        zb = Z_B
        zc = Z_C
        hp = lambda t: t.reshape(Bp, Tp, NH, HD)
        hs = lambda t: t.reshape(Bs, S, NH, HD)
        per_layer.append(dict(
            fox_k_p=hp(fk_rows), fox_v_p=hp(fv_rows),
            fox_lf_p=lf[:Mp, :NH].reshape(Bp, Tp, NH),
            moba_k_p=hp(mk_rows), moba_v_p=hp(mv_rows),
            rwkv_S_p=jnp.swapaxes(sT_p, -1, -2), shift_p=z[Mp - 1:Mp, Z_A:Z_A + A_COLS], ret_S_p=rS_p,
            fox_k_s=hs(z[Mp:, zb + W:zb + 2 * W]), fox_v_s=hs(z[Mp:, zb + 2 * W:zb + 3 * W]),
            fox_lf_s=lf[Mp:, :NH].reshape(Bs, S, NH),
            moba_k_s=hs(z[Mp:, zc + W:zc + 2 * W]), moba_v_s=hs(z[Mp:, zc + 2 * W:zc + 3 * W]),
            rwkv_S_s=jnp.swapaxes(sT_s, -1, -2),
            shift_s=z[Mp:, Z_A:Z_A + A_COLS].reshape(Bs, S, A_COLS)[:, S - 1], ret_S_s=rS_s))

    st = lambda name: jnp.stack([p[name] for p in per_layer])
    return (x[:Mp].reshape(Bp, Tp, D), x[Mp:].reshape(Bs, S, D),
            st("fox_k_p"), st("fox_v_p"), st("fox_lf_p"), st("moba_k_p"), st("moba_v_p"),
            st("rwkv_S_p"), st("shift_p"), st("ret_S_p"),
            st("fox_k_s"), st("fox_v_s"), st("fox_lf_s"), st("moba_k_s"), st("moba_v_s"),
            st("rwkv_S_s"), st("shift_s"), st("ret_S_s"))
```

```python
import functools

import numpy as np
import jax
import jax.numpy as jnp
from jax import lax
from jax.experimental import pallas as pl
from jax.experimental.pallas import tpu as pltpu

F32 = jnp.float32
BF16 = jnp.bfloat16

D_MODEL = 2048
DEPTH = 2
PAGE_SIZE = 128
BRANCH_W = 512
A_HD = 64
A_HEADS = 8
A_LORA_W = 96
A_LORA_A = 96
A_LORA_G = 256
A_COLS = 3 * BRANCH_W + A_LORA_W + A_LORA_A + A_LORA_G
B_COLS = 3 * BRANCH_W + 4
C_COLS = 3 * BRANCH_W
D_COLS = 4 * BRANCH_W
HD = 128
NH = 4
RWKV_GN_EPS = 64e-5
RET_GN_EPS = 1e-6
LN_EPS = 1e-5
MOBA_BLOCK = 256
MOBA_TOPK = 3
RET_CHUNK = 128
RWKV_CHUNK = 64
RWKV_CHUNKS_PER_STEP = 4
ROPE_BASE = 10000.0
FFN_HIDDEN = 5632
ALPHA = (2 * DEPTH) ** 0.25

Z_G = 0
Z_A = 4 * D_MODEL
Z_D = Z_A + 2048
Z_C = Z_D + 2048
Z_B = Z_C + 2048
Z_COLS = Z_B + 2048

TM = 256
FLASH_TQ = 1024
SAMPLE_PAGES = 8
VROWS = HD + 16
LANES = 128
NEG = -1e30
LOG2E = 1.4426950408889634
VMEM_LIMIT = 56 << 20

NN = (((1,), (0,)), ((), ()))
NT = (((1,), (1,)), ((), ()))
TN = (((0,), (0,)), ((), ()))


def _cp(sem):
    return pltpu.CompilerParams(dimension_semantics=sem, vmem_limit_bytes=VMEM_LIMIT)


def _dot(a, b, dims=NN):
    return lax.dot_general(a, b, dims, preferred_element_type=F32)


def _split2(x):
    hi = x.astype(BF16)
    return hi, (x - hi.astype(F32)).astype(BF16)


def _split3(x):
    hi = x.astype(BF16)
    r1 = x - hi.astype(F32)
    mid = r1.astype(BF16)
    return hi, mid, (r1 - mid.astype(F32)).astype(BF16)


def _dot3(a, b, dims=NN):
    ah, al = _split2(a)
    bh, bl = _split2(b)
    return _dot(ah, bh, dims) + (_dot(ah, bl, dims) + _dot(al, bh, dims))


def _dot1(a, b, dims=NN):
    return _dot(a.astype(BF16), b.astype(BF16), dims)


def _dot_exact_l(m_bf16, x, n=3):
    parts = _split3(x) if n == 3 else _split2(x)
    acc = _dot(m_bf16, parts[0])
    for p in parts[1:]:
        acc = acc + _dot(m_bf16, p)
    return acc


def _dot_exact_r(x, m_bf16, n=3):
    parts = _split3(x) if n == 3 else _split2(x)
    acc = _dot(parts[0], m_bf16)
    for p in parts[1:]:
        acc = acc + _dot(p, m_bf16)
    return acc


def _ln(x, eps):
    mu = jnp.mean(x, axis=-1, keepdims=True)
    xc = x - mu
    var = jnp.mean(xc * xc, axis=-1, keepdims=True)
    return xc * lax.rsqrt(var + eps)


def _sigmoid(x):
    return 0.5 * jnp.tanh(0.5 * x) + 0.5


def _log_sigmoid(x):
    return jnp.minimum(x, 0.0) - jnp.log(1.0 + jnp.exp(-jnp.abs(x)))


def _iota(shape, dim):
    return lax.broadcasted_iota(jnp.int32, shape, dim)


def _ada_kernel(c_ref, w_ref, b_ref, o_ref):
    c = c_ref[...]
    s = (c * _sigmoid(c)).astype(BF16)
    o_ref[...] = _dot(s, w_ref[...].astype(BF16)) + b_ref[...]


def _ada_call(c_all, w_ada, b_ada, l):
    R, D = c_all.shape
    N = w_ada.shape[2]
    tn = 2048
    return pl.pallas_call(
        _ada_kernel, grid=(N // tn,),
        in_specs=[pl.BlockSpec((R, D), lambda j: (0, 0)),
                  pl.BlockSpec((None, D, tn), lambda j: (l, 0, j)),
                  pl.BlockSpec((None, 1, tn), lambda j: (l, 0, j))],
        out_specs=pl.BlockSpec((R, tn), lambda j: (0, j)),
        out_shape=jax.ShapeDtypeStruct((R, N), F32),
        compiler_params=_cp(("arbitrary",)), name="ada")(c_all, w_ada, b_ada)


def _lnmod_kernel(x_ref, scp_ref, shp_ref, scs_ref, shs_ref, o_ref, *, n_p):
    is_s = pl.program_id(0) >= n_p
    sc = jnp.where(is_s, scs_ref[...], scp_ref[0:1, :])
    sh = jnp.where(is_s, shs_ref[...], shp_ref[0:1, :])
    o_ref[...] = (_ln(x_ref[...], LN_EPS) * (1.0 + sc) + sh).astype(BF16)


def _mod_specs(n_p, cols):
    specs = []
    for cb in cols:
        specs.append(pl.BlockSpec((8, D_MODEL), lambda i, cb=cb: (0, cb)))
    for cb in cols:
        specs.append(pl.BlockSpec((TM, D_MODEL), lambda i, cb=cb: (jnp.maximum(i - n_p, 0), cb)))
    return specs


def _lnmod_call(x, ada_p, mod_s, n_p, sc_col, sh_col):
    M, D = x.shape
    return pl.pallas_call(
        functools.partial(_lnmod_kernel, n_p=n_p), grid=(M // TM,),
        in_specs=[pl.BlockSpec((TM, D), lambda i: (i, 0))] + _mod_specs(n_p, (sc_col, sh_col)),
        out_specs=pl.BlockSpec((TM, D), lambda i: (i, 0)),
        out_shape=jax.ShapeDtypeStruct((M, D), BF16),
        compiler_params=_cp(("arbitrary",)), name="lnmod")(x, ada_p, ada_p, mod_s, mod_s)


def _mm_kernel(a_ref, w_ref, o_ref, wb_ref):
    @pl.when(pl.program_id(1) == 0)
    def _():
        wb_ref[...] = w_ref[...].astype(BF16)

    o_ref[...] = _dot(a_ref[...].astype(BF16), wb_ref[...]).astype(o_ref.dtype)


def _win_kernel(st_ref, a_ref, w_ref, o_ref, wb_ref, *, l):
    @pl.when(pl.program_id(1) == 0)
    def _():
        wb_ref[...] = w_ref[:, l, :].T.astype(BF16)

    o_ref[...] = _dot(a_ref[...], wb_ref[...])


W_IN_TN = 1024
_O_B, _O_C, _O_D = A_COLS, A_COLS + B_COLS, A_COLS + B_COLS + C_COLS
_O_G = _O_D + D_COLS
_W_IN_TILE_SRC = np.concatenate([src + W_IN_TN * np.arange(width // W_IN_TN) for src, width in (
    (_O_G, 4 * D_MODEL), (0, Z_D - Z_A), (_O_D, Z_C - Z_D), (_O_C, Z_B - Z_C), (_O_B, Z_COLS - Z_B))]).astype(np.int32)


def _win_call(a, w_in, l, tm):
    M, K = a.shape
    depth = w_in.shape[0]
    w_t = jnp.transpose(w_in, (2, 0, 1))
    tn = W_IN_TN
    nt = Z_COLS // tn
    assert int(_W_IN_TILE_SRC.max()) + tn <= w_in.shape[2] and len(_W_IN_TILE_SRC) == nt
    return pl.pallas_call(
        functools.partial(_win_kernel, l=l),
        grid_spec=pltpu.PrefetchScalarGridSpec(
            num_scalar_prefetch=1, grid=(nt, M // tm),
            in_specs=[pl.BlockSpec((tm, K), lambda j, i, st: (i, 0)),
                      pl.BlockSpec((pl.Element(tn), pl.Element(depth), pl.Element(K)),
                                   lambda j, i, st: (st[j], 0, 0))],
            out_specs=pl.BlockSpec((tm, tn), lambda j, i, st: (i, j)),
            scratch_shapes=[pltpu.VMEM((K, tn), BF16)]),
        out_shape=jax.ShapeDtypeStruct((M, Z_COLS), F32),
        compiler_params=_cp(("arbitrary", "arbitrary")), name="w_in")(jnp.asarray(_W_IN_TILE_SRC), a, w_t)


def _mm_call(a, w, w_index, tm, tn, out_dtype, name):
    M, K = a.shape
    N = w.shape[-1]
    if w.ndim == 3:
        w_spec = pl.BlockSpec((None, K, tn), lambda j, i: (w_index, 0, j))
    else:
        w_spec = pl.BlockSpec((K, tn), lambda j, i: (0, j))
    return pl.pallas_call(
        _mm_kernel, grid=(N // tn, M // tm),
        in_specs=[pl.BlockSpec((tm, K), lambda j, i: (i, 0)), w_spec],
        out_specs=pl.BlockSpec((tm, tn), lambda j, i: (i, j)),
        out_shape=jax.ShapeDtypeStruct((M, N), out_dtype),
        scratch_shapes=[pltpu.VMEM((K, tn), BF16)],
        compiler_params=_cp(("arbitrary", "arbitrary")), name=name)(a, w)


def _resid_kernel(*refs, n_p, with_h):
    if with_h:
        (x_ref, mix_ref, gp_ref, scp_ref, shp_ref, gs_ref, scs_ref, shs_ref, lg_ref, lb_ref,
         x1_ref, h_ref) = refs
    else:
        x_ref, mix_ref, gp_ref, gs_ref, lg_ref, lb_ref, x1_ref = refs
    is_s = pl.program_id(0) >= n_p
    g = jnp.where(is_s, gs_ref[...], gp_ref[0:1, :])
    y = ALPHA * x_ref[...] + (1.0 + g) * mix_ref[...]
    x1 = _ln(y, LN_EPS) * lg_ref[...] + lb_ref[...]
    x1_ref[...] = x1
    if with_h:
        sc = jnp.where(is_s, scs_ref[...], scp_ref[0:1, :])
        sh = jnp.where(is_s, shs_ref[...], shp_ref[0:1, :])
        h_ref[...] = (_ln(x1, LN_EPS) * (1.0 + sc) + sh).astype(BF16)


def _resid_call(x, mix, ada_p, mod_s, ln_g, ln_b, l, n_p, g_col, sc_col=None, sh_col=None):
    M, D = x.shape
    with_h = sc_col is not None
    cols = (g_col, sc_col, sh_col) if with_h else (g_col,)
    row = pl.BlockSpec((TM, D), lambda i: (i, 0))
    vec = pl.BlockSpec((None, 1, D), lambda i: (l, 0, 0))
    in_specs = [row, row] + _mod_specs(n_p, cols) + [vec, vec]
    args = [x, mix] + [ada_p] * len(cols) + [mod_s] * len(cols) + [ln_g, ln_b]
    out_shape = [jax.ShapeDtypeStruct((M, D), F32)]
    out_specs = [row]
    if with_h:
        out_shape.append(jax.ShapeDtypeStruct((M, D), BF16))
        out_specs.append(row)
    res = pl.pallas_call(
        functools.partial(_resid_kernel, n_p=n_p, with_h=with_h), grid=(M // TM,),
        in_specs=in_specs, out_specs=out_specs, out_shape=out_shape,
        compiler_params=_cp(("arbitrary",)), name="resid_h" if with_h else "resid")(*args)
    return res if with_h else res[0]


def _ffn_kernel(h_ref, wg_ref, wi_ref, o_ref, wgb_ref, wib_ref):
    @pl.when(pl.program_id(1) == 0)
    def _():
        wgb_ref[...] = wg_ref[...].astype(BF16)
        wib_ref[...] = wi_ref[...].astype(BF16)

    h = h_ref[...]
    a = _dot(h, wgb_ref[...])
    b = _dot(h, wib_ref[...])
    o_ref[...] = (a * _sigmoid(a) * b).astype(BF16)


def _ffn_call(h, wg, wi, l, tm, tn):
    M, K = h.shape
    N = wg.shape[-1]
    w_spec = pl.BlockSpec((None, K, tn), lambda j, i: (l, 0, j))
    return pl.pallas_call(
        _ffn_kernel, grid=(N // tn, M // tm),
        in_specs=[pl.BlockSpec((tm, K), lambda j, i: (i, 0)), w_spec, w_spec],
        out_specs=pl.BlockSpec((tm, tn), lambda j, i: (i, j)),
        out_shape=jax.ShapeDtypeStruct((M, N), BF16),
        scratch_shapes=[pltpu.VMEM((K, tn), BF16), pltpu.VMEM((K, tn), BF16)],
        compiler_params=_cp(("arbitrary", "arbitrary")), name="ffn_act")(h, wg, wi)


def _merge_kernel(ya_ref, yb_ref, yc_ref, yd_ref, wb_ref, g0_ref, g1_ref, g2_ref, g3_ref, o_ref, wbb_ref):
    @pl.when(pl.program_id(1) == 0)
    def _():
        wbb_ref[...] = wb_ref[...].astype(BF16)

    acc = None
    for n, (y_ref, g_ref) in enumerate(((ya_ref, g0_ref), (yb_ref, g1_ref), (yc_ref, g2_ref), (yd_ref, g3_ref))):
        up = _dot(y_ref[...].astype(BF16), wbb_ref[n])
        t = _sigmoid(g_ref[...]) * up
        acc = t if acc is None else acc + t
    o_ref[...] = acc.astype(BF16)


def _merge_call(ys, w_branch, z, l, tm, tn):
    M = z.shape[0]
    W = BRANCH_W
    D = D_MODEL
    nj = D // tn
    y_spec = pl.BlockSpec((tm, W), lambda j, i: (i, 0))
    g_specs = [pl.BlockSpec((tm, tn), lambda j, i, n=n: (i, (Z_G + n * D) // tn + j)) for n in range(4)]
    return pl.pallas_call(
        _merge_kernel, grid=(nj, M // tm),
        in_specs=[y_spec] * 4 + [pl.BlockSpec((None, 4, W, tn), lambda j, i: (l, 0, 0, j))] + g_specs,
        out_specs=pl.BlockSpec((tm, tn), lambda j, i: (i, j)),
        out_shape=jax.ShapeDtypeStruct((M, D), BF16),
        scratch_shapes=[pltpu.VMEM((4, W, tn), BF16)],
        compiler_params=_cp(("arbitrary", "arbitrary")), name="merge")(*ys, w_branch, z, z, z, z)


def _rwkv_pre_kernel(za_ref, prev_ref, fill_ref, mu_ref, w0_ref, a0_ref, kk_ref, ka_ref, w2_ref, a2_ref,
                     g2_ref, e_ref, r_o, lw_o, k_o, v_o, kn_o, b_o, g_o, *, n_p, seq_s):
    i = pl.program_id(0)
    is_s = i >= n_p
    za = za_ref[...]
    rows = _iota(za.shape, 0)
    prev = jnp.where(rows == 0, prev_ref[7:8, :], pltpu.roll(za, 1, axis=0))
    base = jnp.where(is_s, n_p * TM, 0)
    pmask = jnp.where(is_s, seq_s - 1, 0x3FFFFFFF)
    start = ((rows + (i * TM - base)) & pmask) == 0
    fill = jnp.where(is_s, fill_ref[...], 0.0)
    prev = jnp.where(start, fill, prev)
    zm = za + mu_ref[...] * (prev - za)
    W = BRANCH_W
    r = zm[:, 0:W]
    k = zm[:, W:2 * W]
    v = zm[:, 2 * W:3 * W]
    x = zm[:, 3 * W:4 * W]
    wl = _dot(jnp.tanh(x).astype(BF16), w2_ref[...].astype(BF16))
    y = -(w0_ref[...] + wl)
    softplus = jnp.maximum(y, 0.0) + jnp.log(1.0 + jnp.exp(-jnp.abs(y)))
    w = -softplus - 0.5
    logw = -jnp.exp(w)
    a = _sigmoid(a0_ref[...] + _dot(x.astype(BF16), a2_ref[...].astype(BF16)))
    g = _dot(_sigmoid(x).astype(BF16), g2_ref[...].astype(BF16))
    kk0 = k * kk_ref[...]
    ss = _dot_exact_r(kk0 * kk0, e_ref[...], n=3)
    kn = kk0 / jnp.maximum(jnp.sqrt(ss), 1e-12)
    k2 = k * (1.0 + (a - 1.0) * ka_ref[...])
    b = kn * a
    for h in range(A_HEADS):
        sl = slice(h * A_HD, (h + 1) * A_HD)
        r_o[h] = r[:, sl]
        lw_o[h] = logw[:, sl]
        k_o[h] = k2[:, sl]
        v_o[h] = v[:, sl]
        kn_o[h] = kn[:, sl]
        b_o[h] = b[:, sl]
        g_o[h] = g[:, sl]


def _rwkv_pre_call(z, fill_s, mu, w0, a0, k_k, k_a, w2p, a2p, g2p, e64, n_p, seq_s):
    M = z.shape[0]
    W = BRANCH_W
    cb = Z_A // 2048
    vec = lambda n: pl.BlockSpec((1, n), lambda i: (0, 0))
    mat = pl.BlockSpec((W, W), lambda i: (0, 0))
    out = jax.ShapeDtypeStruct((A_HEADS, M, A_HD), F32)
    ospec = pl.BlockSpec((A_HEADS, TM, A_HD), lambda i: (0, i, 0))
    return pl.pallas_call(
        functools.partial(_rwkv_pre_kernel, n_p=n_p, seq_s=seq_s), grid=(M // TM,),
        in_specs=[pl.BlockSpec((TM, 2048), lambda i: (i, cb)),
                  pl.BlockSpec((8, 2048), lambda i: (jnp.maximum(i * (TM // 8) - 1, 0), cb)),
                  pl.BlockSpec((TM, 2048), lambda i: (jnp.maximum(i - n_p, 0), 0)),
                  vec(2048), vec(W), vec(W), vec(W), vec(W), mat, mat, mat, mat],
        out_specs=[ospec] * 7, out_shape=[out] * 7,
        compiler_params=_cp(("arbitrary",)), name="rwkv_pre")(
            z, z, fill_s, mu, w0, a0, k_k, k_a, w2p, a2p, g2p, e64)


def _rwkv_chunk_kernel(r_ref, lw_ref, k_ref, v_ref, kn_ref, b_ref, g_ref, s0_ref, rk_ref, lng_ref, lnb_ref,
                       *rest, C, Q, nc):
    y_ref, sfin_ref, S_ref = rest[-3:]
    c = pl.program_id(1)

    @pl.when(c == 0)
    def _():
        S_ref[...] = s0_ref[...]

    H = range(A_HEADS)
    row = _iota((C, C), 0)
    col = _iota((C, C), 1)
    tri = (row >= col).astype(BF16)
    eye = (row == col).astype(F32)
    row2 = _iota((2 * C, C), 0)
    col2 = _iota((2 * C, C), 1)
    low2 = jnp.where(row2 < C, row2 - 1, row2 - C) >= col2
    eye_k = _iota((A_HD, A_HD), 0) == _iota((A_HD, A_HD), 1)
    HQ = [(h, q) for q in range(Q) for h in H]
    take = lambda ref: {(h, q): ref[h, q * C:(q + 1) * C, :] for h, q in HQ}
    r, lw, k, v, kn, b, g = (take(ref) for ref in (r_ref, lw_ref, k_ref, v_ref, kn_ref, b_ref, g_ref))
    L = {p: _dot_exact_l(tri, lw[p]) for p in HQ}
    eL = {p: jnp.exp(L[p]) for p in HQ}
    eN = {p: jnp.exp(-L[p]) for p in HQ}
    eE = {p: jnp.exp(L[p][C - 1:C, :] - L[p]) for p in HQ}
    lhs = {p: jnp.concatenate([kn[p] * jnp.exp(L[p] - lw[p]), r[p] * eL[p]], axis=0) for p in HQ}
    Ab = {p: jnp.where(low2, _dot1(lhs[p], b[p] * eN[p], NT), 0.0) for p in HQ}
    Ak = {p: jnp.where(low2, _dot1(lhs[p], k[p] * eN[p], NT), 0.0) for p in HQ}
    N = {p: Ab[p][0:C] for p in HQ}
    X = {p: eye - N[p] for p in HQ}
    P = {p: _dot1(N[p], N[p]) for p in HQ}
    n = 2
    while n < C:
        X = {p: X[p] + _dot1(X[p], P[p]) for p in HQ}
        n *= 2
        if n < C:
            P = {p: _dot1(P[p], P[p]) for p in HQ}
    AV = {p: _dot1(Ak[p], v[p]) for p in HQ}
    gcol = {p: jnp.sum(jnp.where(eye_k, jnp.broadcast_to(eL[p][C - 1:C, :], (A_HD, A_HD)), 0.0), axis=1,
                       keepdims=True) for p in HQ}
    kb = {p: jnp.concatenate([k[p] * eE[p], -(b[p] * eE[p])], axis=0) for p in HQ}
    bonus = {p: jnp.sum(r[p] * k[p] * rk_ref[p[0]], axis=1, keepdims=True) * v[p] for p in HQ}
    S = [S_ref[h] for h in H]
    rows = []
    for q in range(Q):
        KS = [_dot1(lhs[h, q], S[h]) for h in H]
        U = [_dot1(X[h, q], KS[h][0:C] + AV[h, q][0:C]) for h in H]
        Y = [KS[h][C:] + AV[h, q][C:] - _dot1(Ab[h, q][C:], U[h]) for h in H]
        S = [S[h] * gcol[h, q] + _dot1(kb[h, q], jnp.concatenate([v[h, q], U[h]], axis=0), TN) for h in H]
        rows.append(jnp.concatenate(
            [(_ln(Y[h], RWKV_GN_EPS) * lng_ref[h] + lnb_ref[h] + bonus[h, q]) * g[h, q] for h in H], axis=1))
    y_ref[...] = jnp.concatenate(rows, axis=0) if Q > 1 else rows[0]
    for h in H:
        S_ref[h] = S[h]

    @pl.when(c == nc - 1)
    def _():
        for h in H:
            sfin_ref[h] = S[h]


def _rwkv_chunk_call(pre, s0t, r_k, ln_g, ln_b, row0, B, T, C, Q, into):
    R = C * Q
    nc = T // R
    blk0 = row0 // R
    in_spec = pl.BlockSpec((A_HEADS, R, A_HD), lambda b, c: (0, blk0 + b * nc + c, 0))
    hvec = pl.BlockSpec((A_HEADS, 1, A_HD), lambda b, c: (0, 0, 0))
    st = pl.BlockSpec((None, A_HEADS, A_HD, A_HD), lambda b, c: (b, 0, 0, 0))
    return pl.pallas_call(
        functools.partial(_rwkv_chunk_kernel, C=C, Q=Q, nc=nc), grid=(B, nc),
        in_specs=[in_spec] * 7 + [st, hvec, hvec, hvec, pl.BlockSpec(memory_space=pl.ANY)],
        out_specs=[pl.BlockSpec((R, BRANCH_W), lambda b, c: (blk0 + b * nc + c, 0)), st],
        out_shape=[jax.ShapeDtypeStruct(into.shape, F32),
                   jax.ShapeDtypeStruct((B, A_HEADS, A_HD, A_HD), F32)],
        scratch_shapes=[pltpu.VMEM((A_HEADS, A_HD, A_HD), F32)],
        input_output_aliases={11: 0},
        compiler_params=_cp(("arbitrary", "arbitrary")), name="rwkv_chunk_c%d" % C)(
            *pre, s0t, r_k, ln_g, ln_b, into)


def _foxcum_kernel(fl_ref, bf_ref, lf_ref, cum_ref, carry_ref):
    @pl.when(pl.program_id(0) == 0)
    def _():
        carry_ref[...] = jnp.zeros_like(carry_ref)

    lane = _iota((TM, LANES), 1)
    lf = jnp.where(lane < NH, _log_sigmoid(fl_ref[...] + bf_ref[...]), 0.0)
    tri = (_iota((TM, TM), 0) >= _iota((TM, TM), 1)).astype(BF16)
    cum = _dot_exact_l(tri, lf) + carry_ref[...]
    lf_ref[...] = lf
    cum_ref[...] = cum
    carry_ref[...] = cum[TM - 1:TM, :]


def _foxcum_call(z, bf):
    M = z.shape[0]
    cb = (Z_B + 3 * BRANCH_W) // LANES
    spec = pl.BlockSpec((TM, LANES), lambda i: (i, 0))
    return pl.pallas_call(
        _foxcum_kernel, grid=(M // TM,),
        in_specs=[pl.BlockSpec((TM, LANES), lambda i: (i, cb)), pl.BlockSpec((1, LANES), lambda i: (0, 0))],
        out_specs=[spec, spec], out_shape=[jax.ShapeDtypeStruct((M, LANES), F32)] * 2,
        scratch_shapes=[pltpu.VMEM((1, LANES), F32)],
        compiler_params=_cp(("arbitrary",)), name="fox_cum")(z, bf)


def _attn_prep_kernel(fq_ref, fk_ref, fv_ref, mq_ref, mk_ref, mv_ref, cum_ref, cprev_ref, cqprev_ref, km_ref,
                      fqT_o, fkA_o, fvT_o, mqT_o, mk_o, mvT_o, sel_o, fk_rows, fv_rows, mk_rows, mv_rows, *, scale):
    i = pl.program_id(0)
    lane = _iota((TM, LANES), 1)
    cum = cum_ref[...]
    qscale = scale * LOG2E
    crel_k = (cum - jnp.where(i == 0, 0.0, cprev_ref[7:8, :])) * LOG2E
    crel_q = (cum - jnp.where(i < FLASH_TQ // TM, 0.0, cqprev_ref[7:8, :])) * LOG2E
    fq = fq_ref[...]
    fk = fk_ref[...]
    fv = fv_ref[...]
    mq = mq_ref[...]
    mk = mk_ref[...]
    mv = mv_ref[...]
    km = km_ref[...]
    blk = _iota((km.shape[0], TM), 0)
    blk_f = blk.astype(F32)
    ones_rows = jnp.where(_iota((VROWS - HD, TM), 0) == 0, 1.0, 0.0).astype(BF16)
    for h in range(NH):
        sl = slice(h * HD, (h + 1) * HD)
        q3 = [p.astype(F32) for p in _split3(jnp.broadcast_to(crel_q[:, h:h + 1], (TM, LANES)))]
        k3 = [p.astype(F32) for p in _split3(jnp.broadcast_to(crel_k[:, h:h + 1], (TM, LANES)))]
        qb = jnp.where(lane == 0, q3[0], jnp.where(lane == 1, q3[1], jnp.where(lane == 2, q3[2],
                                                                             jnp.where(lane < 6, 1.0, 0.0))))
        kb = jnp.where(lane < 3, 1.0, jnp.where(lane == 3, -k3[0], jnp.where(lane == 4, -k3[1],
                                                                              jnp.where(lane == 5, -k3[2], 0.0))))
        fqT_o[h, 0:HD, :] = (fq[:, sl] * qscale).T.astype(BF16)
        fqT_o[h, HD:2 * HD, :] = qb.T.astype(BF16)
        for rows_o, src in ((fk_rows, fk), (fv_rows, fv), (mk_rows, mk), (mv_rows, mv)):
            rows_o[pl.ds(h, TM, stride=NH), :] = src[:, sl]
        fkA_o[h, :, 0:HD] = fk[:, sl].astype(BF16)
        fkA_o[h, :, HD:2 * HD] = kb.astype(BF16)
        fvT_o[h, 0:HD, :] = fv[:, sl].T.astype(BF16)
        fvT_o[h, HD:VROWS, :] = ones_rows
        mqT = mq[:, sl].T
        mqT_o[h] = (mqT * qscale).astype(BF16)
        mk_o[h] = mk[:, sl].astype(BF16)
        mvT_o[h, 0:HD, :] = mv[:, sl].T.astype(BF16)
        mvT_o[h, HD:VROWS, :] = ones_rows
        sc = _dot3(km[:, sl], mqT)
        sel_o[h] = _top3_select(sc, blk < i, blk_f, axis=0)


def _attn_prep_call(z, cum, kmean, Tp):
    W = BRANCH_W
    nbp = kmean.shape[0]
    cbB = Z_B // W
    cbC = Z_C // W
    rq = FLASH_TQ // TM
    zs = lambda cb: pl.BlockSpec((TM, W), lambda i: (i, cb))
    colT = lambda rows: pl.BlockSpec((NH, rows, TM), lambda i: (0, 0, i))
    rowm = lambda cols: pl.BlockSpec((NH, TM, cols), lambda i: (0, i, 0))
    sd = jax.ShapeDtypeStruct
    return pl.pallas_call(
        functools.partial(_attn_prep_kernel, scale=HD ** -0.5), grid=(Tp // TM,),
        in_specs=[zs(cbB), zs(cbB + 1), zs(cbB + 2), zs(cbC), zs(cbC + 1), zs(cbC + 2),
                  pl.BlockSpec((TM, LANES), lambda i: (i, 0)),
                  pl.BlockSpec((8, LANES), lambda i: (jnp.maximum(i * (TM // 8) - 1, 0), 0)),
                  pl.BlockSpec((8, LANES), lambda i: (jnp.maximum((i // rq) * (FLASH_TQ // 8) - 1, 0), 0)),
                  pl.BlockSpec((nbp, W), lambda i: (0, 0))],
        out_specs=[colT(2 * HD), rowm(2 * HD), colT(VROWS), colT(HD), rowm(HD), colT(VROWS), colT(nbp)]
        + [pl.BlockSpec((TM * NH, HD), lambda i: (i, 0))] * 4,
        out_shape=[sd((NH, 2 * HD, Tp), BF16), sd((NH, Tp, 2 * HD), BF16), sd((NH, VROWS, Tp), BF16),
                   sd((NH, HD, Tp), BF16), sd((NH, Tp, HD), BF16), sd((NH, VROWS, Tp), BF16),
                   sd((NH, nbp, Tp), F32)] + [sd((Tp * NH, HD), F32)] * 4,
        compiler_params=_cp(("arbitrary",)), name="attn_prep")(z, z, z, z, z, z, cum, cum, cum, kmean)


def _flash_kernel(ti_ref, tj_ref, bq_ref, bk_ref, qT_ref, kA_ref, vT_ref, *rest, moba):
    sel_ref = rest[0] if moba else None
    o_ref, m_ref, acc_ref = rest[-3:]
    step = pl.program_id(0)
    i = ti_ref[step]
    j = tj_ref[step]
    T = kA_ref.shape[1]
    Tq = qT_ref.shape[2]
    ratio = Tq // T
    nqt = bq_ref.shape[0] // NH
    nkt = bk_ref.shape[0] // NH

    @pl.when(j == 0)
    def _():
        m_ref[...] = jnp.full(m_ref.shape, NEG, F32)
        acc_ref[...] = jnp.zeros(acc_ref.shape, F32)

    def step_body(on_diagonal):
        H = range(NH)
        sT = {h: _dot(kA_ref[h], qT_ref[h]) for h in H}
        if on_diagonal:
            causal = (_iota((T, Tq), 0) - _iota((T, Tq), 1)) <= (i * Tq - j * T)
            sT = {h: jnp.where(causal, sT[h], NEG) for h in H}
        if moba:
            own = (_iota((1, Tq), 1) // T + i * ratio) == j
            sT = {h: sT[h] + (jnp.where(own, 1.0, sel_ref[h, pl.ds(j, 1), :]) - 1.0) * (-NEG) for h in H}
            off = {h: 0.0 for h in H}
        else:
            off = {h: (bq_ref[h * nqt + i] - bk_ref[h * nkt + j]) * LOG2E for h in H}
        m_prev = {h: m_ref[h] for h in H}
        m_new = {h: jnp.maximum(m_prev[h], jnp.max(sT[h], axis=0, keepdims=True) + off[h]) for h in H}
        p = {h: jnp.exp2((sT[h] - (m_new[h] - off[h])).astype(BF16)) for h in H}
        alpha = {h: jnp.exp2(m_prev[h] - m_new[h]) for h in H}
        pv = {h: _dot(vT_ref[h], p[h]) for h in H}
        for h in H:
            acc_ref[h] = alpha[h] * acc_ref[h] + pv[h]
            m_ref[h] = m_new[h]

    crosses = (j + 1) * T - 1 > i * Tq
    pl.when(crosses)(functools.partial(step_body, True))
    pl.when(jnp.logical_not(crosses))(functools.partial(step_body, False))

    @pl.when(j == (i + 1) * ratio - 1)
    def _():
        for h in range(NH):
            acc = acc_ref[h]
            o_ref[:, h * HD:(h + 1) * HD] = (acc[0:HD] / acc[HD:HD + 1]).T


def _flash_call(qT, kA, vT, base_q, base_k, sel, Tp, into, name):
    T = MOBA_BLOCK
    Tq = FLASH_TQ
    ratio = Tq // T
    Kc = kA.shape[2]
    moba = sel is not None
    ti = np.concatenate([np.full((i + 1) * ratio, i, np.int32) for i in range(Tp // Tq)])
    tj = np.concatenate([np.arange((i + 1) * ratio, dtype=np.int32) for i in range(Tp // Tq)])
    in_specs = [pl.BlockSpec((NH, Kc, Tq), lambda s, ti, tj, bq, bk: (0, 0, ti[s])),
                pl.BlockSpec((NH, T, Kc), lambda s, ti, tj, bq, bk: (0, tj[s], 0)),
                pl.BlockSpec((NH, VROWS, T), lambda s, ti, tj, bq, bk: (0, 0, tj[s]))]
    args = [qT, kA, vT]
    if moba:
        in_specs.append(pl.BlockSpec((NH, sel.shape[1], Tq), lambda s, ti, tj, bq, bk: (0, 0, ti[s])))
        args.append(sel)
    in_specs.append(pl.BlockSpec(memory_space=pl.ANY))
    return pl.pallas_call(
        functools.partial(_flash_kernel, moba=moba),
        grid_spec=pltpu.PrefetchScalarGridSpec(
            num_scalar_prefetch=4, grid=(len(ti),), in_specs=in_specs,
            out_specs=pl.BlockSpec((Tq, NH * HD), lambda s, ti, tj, bq, bk: (ti[s], 0)),
            scratch_shapes=[pltpu.VMEM((NH, 1, Tq), F32), pltpu.VMEM((NH, VROWS, Tq), F32)]),
        out_shape=jax.ShapeDtypeStruct(into.shape, F32),
        input_output_aliases={4 + len(args): 0},
        compiler_params=_cp(("arbitrary",)), name=name)(
            jnp.asarray(ti), jnp.asarray(tj), base_q, base_k, *args, into)


def _top3_select(sc, valid, idx_f, axis=1):
    sc = jnp.where(valid, sc, -jnp.inf)
    sel = jnp.zeros(sc.shape, F32)
    for _ in range(MOBA_TOPK):
        mx = jnp.max(sc, axis=axis, keepdims=True)
        idx = jnp.min(jnp.where(sc == mx, idx_f, 1e9), axis=axis, keepdims=True)
        hit = idx_f == idx
        sel = jnp.where(hit & valid, 1.0, sel)
        sc = jnp.where(hit, -jnp.inf, sc)
    return sel


def _blockmean_kernel(k_ref, o_ref):
    o_ref[...] = jnp.sum(k_ref[...], axis=0, keepdims=True) * (1.0 / MOBA_BLOCK)


def _blockmean_call(z, Tp):
    nb = Tp // MOBA_BLOCK
    W = BRANCH_W
    cb = Z_C // W + 1
    return pl.pallas_call(
        _blockmean_kernel, grid=(nb,),
        in_specs=[pl.BlockSpec((MOBA_BLOCK, W), lambda i: (i, cb))],
        out_specs=pl.BlockSpec((None, 1, W), lambda i: (i, 0, 0)),
        out_shape=jax.ShapeDtypeStruct((nb, 1, W), F32),
        compiler_params=_cp(("arbitrary",)), name="moba_kmean")(z)


def _ret_kernel(q_ref, k_ref, v_ref, gd_ref, cos_ref, sin_ref, din_ref, qd_ref, kd_ref, cd_ref, s0_ref, lng_ref,
                *rest, nc):
    y_ref, sfin_ref, S_ref = rest[-3:]
    c = pl.program_id(1)

    @pl.when(c == 0)
    def _():
        S_ref[...] = s0_ref[...]

    q = q_ref[...]
    k = k_ref[...]
    v = v_ref[...]
    gd = gd_ref[...]
    cos = cos_ref[...]
    sin = sin_ref[...]
    lng = lng_ref[...]
    for h in range(NH):
        sl = slice(h * HD, (h + 1) * HD)
        qh = q[:, sl]
        kh = k[:, sl]
        qr = qh * cos + pltpu.roll(qh, HD // 2, axis=1) * sin
        kr = (kh * cos + pltpu.roll(kh, HD // 2, axis=1) * sin) * (HD ** -0.5)
        vb = v[:, sl].astype(BF16)
        qb = qr.astype(BF16)
        att = _dot(qb, kr.astype(BF16), NT) * din_ref[h]
        S = S_ref[h]
        o = _dot(att.astype(BF16), vb) + _dot(qb, S.astype(BF16)) * qd_ref[h]
        S_ref[h] = S * cd_ref[h] + _dot((kr * kd_ref[h]).astype(BF16), vb, TN)
        y_ref[:, sl] = _ln(o, RET_GN_EPS) * lng[:, sl] * (gd[:, sl] * _sigmoid(gd[:, sl]))

    @pl.when(c == nc - 1)
    def _():
        sfin_ref[...] = S_ref[...]


def _ret_tables(C):
    lg = np.log(1.0 - 2.0 ** (-5.0 - np.arange(NH, dtype=np.float32))).astype(np.float32)
    i = np.arange(C, dtype=np.float32)
    diff = i[:, None] - i[None, :]
    din = np.where(diff[None] >= 0, np.exp(np.maximum(diff, 0.0)[None] * lg[:, None, None]), 0.0)
    qd = np.exp((i[None, :] + 1.0) * lg[:, None])
    kd = np.exp((C - 1.0 - i)[None, :] * lg[:, None])
    cd = np.exp(C * lg)
    bc = lambda t: np.ascontiguousarray(np.broadcast_to(t[:, :, None], (NH, C, HD))).astype(np.float32)
    cdb = np.ascontiguousarray(np.broadcast_to(cd[:, None, None], (NH, 1, HD))).astype(np.float32)
    return din.astype(np.float32), bc(qd), bc(kd), cdb


def _rope_tables(pos0, T):
    half = HD // 2
    inv = 1.0 / (ROPE_BASE ** (jnp.arange(half, dtype=F32) / half))
    ang = (pos0 + jnp.arange(T)).astype(F32)[:, None] * inv[None, :]
    cos = jnp.cos(ang)
    sin = jnp.sin(ang)
    return jnp.concatenate([cos, cos], axis=1), jnp.concatenate([-sin, sin], axis=1)


def _ret_call(z, s0, ln_g, l, row0, B, T, C, pos0, into):
    nc = T // C
    W = BRANCH_W
    cb = Z_D // W
    blk0 = row0 // C
    cos, sin = _rope_tables(pos0, T)
    din, qd, kd, cd = _ret_tables(C)
    zs = lambda off: pl.BlockSpec((C, W), lambda b, c: (blk0 + b * nc + c, cb + off))
    tab = pl.BlockSpec((C, HD), lambda b, c: (c, 0))
    full = lambda shape: pl.BlockSpec(shape, lambda b, c: (0,) * len(shape))
    st = pl.BlockSpec((None, NH, HD, HD), lambda b, c: (b, 0, 0, 0))
    return pl.pallas_call(
        functools.partial(_ret_kernel, nc=nc), grid=(B, nc),
        in_specs=[zs(0), zs(1), zs(2), zs(3), tab, tab, full((NH, C, C)), full((NH, C, HD)), full((NH, C, HD)),
                  full((NH, 1, HD)), st, pl.BlockSpec((None, 1, W), lambda b, c: (l, 0, 0)),
                  pl.BlockSpec(memory_space=pl.ANY)],
        out_specs=[pl.BlockSpec((C, W), lambda b, c: (blk0 + b * nc + c, 0)), st],
        out_shape=[jax.ShapeDtypeStruct(into.shape, F32), jax.ShapeDtypeStruct((B, NH, HD, HD), F32)],
        scratch_shapes=[pltpu.VMEM((NH, HD, HD), F32)],
        input_output_aliases={12: 0},
        compiler_params=_cp(("arbitrary", "arbitrary")), name="ret_c%d" % C)(
            z, z, z, z, cos, sin, din, qd, kd, cd, s0, ln_g, into)


def _lfsuf_kernel(pt_ref, lf_hbm, ts_ref, tot_ref, o_ref, buf, sem, *, npg, base):
    b = pl.program_id(0)

    def page_copy(p):
        return pltpu.make_async_copy(lf_hbm.at[pl.ds(NH * (base + pt_ref[b * npg + p]), NH), :],
                                     buf.at[pl.ds(NH * p, NH), :], sem.at[p])

    for p in range(npg):
        page_copy(p).start()
    for p in range(npg):
        page_copy(p).wait()
    loc = None
    tot = None
    for h in range(NH):
        x = buf[pl.ds(h, npg, stride=NH), :]
        a = _dot_exact_r(x, ts_ref[h])
        t = _dot_exact_r(x, tot_ref[h])
        loc = a if loc is None else loc + a
        tot = t if tot is None else tot + t
    later = (_iota((npg, npg), 1) > _iota((npg, npg), 0)).astype(BF16)
    o_ref[...] = loc + _dot_exact_l(later, tot)


def _lfsuf_call(pt_flat, lf_rows, ts, tot, B, npg, base):
    PW = PAGE_SIZE * NH
    sel = pl.BlockSpec((NH, PAGE_SIZE, PW), lambda b, pt: (0, 0, 0))
    return pl.pallas_call(
        functools.partial(_lfsuf_kernel, npg=npg, base=base),
        grid_spec=pltpu.PrefetchScalarGridSpec(
            num_scalar_prefetch=1, grid=(B,),
            in_specs=[pl.BlockSpec(memory_space=pl.ANY), sel, sel],
            out_specs=pl.BlockSpec((None, npg, PW), lambda b, pt: (b, 0, 0)),
            scratch_shapes=[pltpu.VMEM((NH * npg, PAGE_SIZE), F32), pltpu.SemaphoreType.DMA((npg,))]),
        out_shape=jax.ShapeDtypeStruct((B, npg, PW), F32),
        compiler_params=_cp(("arbitrary",)), name="fox_logf_suffix")(pt_flat, lf_rows, ts, tot)


def _suffix_matrices():
    r = np.arange(PAGE_SIZE)
    dst_r = np.repeat(r, NH)
    dst_h = np.tile(np.arange(NH), PAGE_SIZE)
    same = np.arange(NH)[:, None, None] == dst_h[None, None, :]
    ts = same & (r[None, :, None] > dst_r[None, None, :])
    return jnp.asarray(ts, BF16), jnp.asarray(np.broadcast_to(same, ts.shape), BF16)


def _sattn_kernel(pt_ref, fq_ref, fkn_ref, fvn_ref, lfn_ref, mq_ref, mkn_ref, mvn_ref, R_ref, *rest, nb, S, scale):
    P = SAMPLE_PAGES
    fk, fv, mk, mv = (rest[t * P:(t + 1) * P] for t in range(4))
    yb_ref, yc_ref, fqb, mqb, mqf, cnb, fm, fl, facc, sc_all, m_all, l_all, o_all = rest[-13:]
    n = pl.program_id(1)
    R4 = NH * S
    PW = PAGE_SIZE * NH
    own_head = (_iota((R4, PW), 1) % NH) == (_iota((R4, PW), 0) // S)
    lane = _iota((R4, LANES), 1)

    def stack_heads(x):
        return jnp.concatenate([x[:, h * HD:(h + 1) * HD] for h in range(NH)], axis=0)

    def head_sums(kpage):
        return jnp.sum(kpage.reshape(PW // 8, 8, HD), axis=0)

    @pl.when(n == 0)
    def _():
        fqb[...] = stack_heads(fq_ref[...]).astype(BF16)
        mq = stack_heads(mq_ref[...])
        mqb[...] = mq.astype(BF16)
        mqf[...] = mq
        tri = (_iota((S, S), 0) >= _iota((S, S), 1)).astype(BF16)
        cn = _dot_exact_l(tri, lfn_ref[...])
        cnb[...] = jnp.concatenate([jnp.broadcast_to(cn[:, h:h + 1], (S, LANES)) for h in range(NH)], axis=0)
        fm[...] = jnp.full(fm.shape, NEG, F32)
        fl[...] = jnp.zeros(fl.shape, F32)
        facc[...] = jnp.zeros(facc.shape, F32)
        sc_all[...] = jnp.full(sc_all.shape, -jnp.inf, F32)
        m_all[...] = jnp.full(m_all.shape, NEG, F32)
        l_all[...] = jnp.zeros(l_all.shape, F32)

    def fox_update(s_list, v_list):
        m_prev = fm[...]
        m_new = m_prev
        for s in s_list:
            m_new = jnp.maximum(m_new, jnp.max(s, axis=1, keepdims=True))
        alpha = jnp.exp(m_prev - m_new)
        p_list = [jnp.exp(s - m_new) for s in s_list]
        l_new = alpha * fl[...]
        acc = alpha * facc[...]
        for p, v_bf in zip(p_list, v_list):
            l_new = l_new + jnp.sum(p, axis=1, keepdims=True)
            acc = acc + _dot(p.astype(BF16), v_bf)
        fl[...] = l_new
        facc[...] = acc
        fm[...] = m_new

    cn_b = cnb[...]
    cn_w = jnp.concatenate([cn_b] * NH, axis=1)
    fq_b = fqb[...]
    s_fox = [_dot(fq_b, fk[g][...].astype(BF16), NT) * scale for g in range(P)]
    s_fox = [jnp.where(own_head, s_fox[g] + cn_w + R_ref[pl.ds(P * n + g, 1), :], NEG) for g in range(P)]
    fox_update(s_fox, [fv[g][...].astype(BF16) for g in range(P)])

    mq_b = mqb[...]
    mq_f = mqf[...]
    kpg = [mk[g][...] for g in range(P)]
    s_mo = [jnp.where(own_head, _dot(mq_b, kpg[g].astype(BF16), NT) * scale, NEG) for g in range(P)]
    for blk in range(P // 2):
        g0, g1 = 2 * blk, 2 * blk + 1
        ks = head_sums(kpg[g0]) + head_sums(kpg[g1])
        kmean = (ks[0:NH] + ks[NH:2 * NH]) * (1.0 / MOBA_BLOCK)
        kmean = jnp.concatenate([jnp.broadcast_to(kmean[h:h + 1], (S, HD)) for h in range(NH)], axis=0)
        sc_col = jnp.sum(mq_f * kmean, axis=1, keepdims=True)
        m_b = jnp.maximum(jnp.max(s_mo[g0], axis=1, keepdims=True), jnp.max(s_mo[g1], axis=1, keepdims=True))
        p0 = jnp.exp(s_mo[g0] - m_b)
        p1 = jnp.exp(s_mo[g1] - m_b)
        here = lane == (P // 2) * n + blk
        sc_all[...] = jnp.where(here, sc_col, sc_all[...])
        m_all[...] = jnp.where(here, m_b, m_all[...])
        l_all[...] = jnp.where(here, jnp.sum(p0, axis=1, keepdims=True) + jnp.sum(p1, axis=1, keepdims=True),
                               l_all[...])
        o_all[(P // 2) * n + blk] = (_dot(p0.astype(BF16), mv[g0][...].astype(BF16))
                                     + _dot(p1.astype(BF16), mv[g1][...].astype(BF16)))

    @pl.when(n == pl.num_programs(1) - 1)
    def _():
        ri = _iota((R4, R4), 0)
        ci = _iota((R4, R4), 1)
        causal = ((ri // S) == (ci // S)) & ((ci % S) <= (ri % S))
        cn_col = cn_b[:, 0:1]
        cn_row = jnp.sum(jnp.where(ri == ci, jnp.broadcast_to(cn_col, (R4, R4)), 0.0), axis=0, keepdims=True)
        s = _dot(fqb[...], stack_heads(fkn_ref[...]).astype(BF16), NT) * scale
        s = s + (cn_col - cn_row)
        fox_update([jnp.where(causal, s, NEG)], [stack_heads(fvn_ref[...]).astype(BF16)])
        out = facc[...] / fl[...]
        for h in range(NH):
            yb_ref[:, h * HD:(h + 1) * HD] = out[h * S:(h + 1) * S, :]

        s = _dot(mqb[...], stack_heads(mkn_ref[...]).astype(BF16), NT) * scale
        s = jnp.where(causal, s, NEG)
        m_o = jnp.max(s, axis=1, keepdims=True)
        p = jnp.exp(s - m_o)
        l_o = jnp.sum(p, axis=1, keepdims=True)
        o_o = _dot(p.astype(BF16), stack_heads(mvn_ref[...]).astype(BF16))
        sel = _top3_select(sc_all[...], lane < nb, lane.astype(F32)) > 0.5
        m_past = m_all[...]
        m_tot = jnp.maximum(jnp.max(jnp.where(sel, m_past, NEG), axis=1, keepdims=True), m_o)
        w = jnp.where(sel, jnp.exp(m_past - m_tot), 0.0)
        w_o = jnp.exp(m_o - m_tot)
        den = jnp.sum(w * l_all[...], axis=1, keepdims=True) + w_o * l_o
        num = w_o * o_o
        for blk in range(nb):
            num = num + w[:, blk:blk + 1] * o_all[blk]
        out = num / den
        for h in range(NH):
            yc_ref[:, h * HD:(h + 1) * HD] = out[h * S:(h + 1) * S, :]


def _sattn_call(pt_flat, z, lf, R, caches, l, Mp, B, S, npg, n_pool, yb_into, yc_into):
    fk, fv, mk, mv = caches
    W = BRANCH_W
    PW = PAGE_SIZE * NH
    nb = npg // 2
    R4 = NH * S
    rb = Mp // S
    base = l * n_pool
    zrow = lambda cb: pl.BlockSpec((S, W), lambda b, n, pt: (rb + b, cb))
    P = SAMPLE_PAGES
    page = lambda g: pl.BlockSpec((PW, HD), lambda b, n, pt: (base + pt[b * npg + P * n + g], 0))
    pages = [page(g) for g in range(P)]
    cbB = Z_B // W
    cbC = Z_C // W
    in_specs = [zrow(cbB), zrow(cbB + 1), zrow(cbB + 2),
                pl.BlockSpec((S, LANES), lambda b, n, pt: (rb + b, 0)),
                zrow(cbC), zrow(cbC + 1), zrow(cbC + 2),
                pl.BlockSpec((None, npg, PW), lambda b, n, pt: (b, 0, 0))] + pages * 4
    in_specs += [pl.BlockSpec(memory_space=pl.ANY)] * 2
    n_in = 1 + len(in_specs)
    out_spec = pl.BlockSpec((S, W), lambda b, n, pt: (rb + b, 0))
    scratch = [pltpu.VMEM((R4, HD), BF16), pltpu.VMEM((R4, HD), BF16), pltpu.VMEM((R4, HD), F32),
               pltpu.VMEM((R4, LANES), F32),
               pltpu.VMEM((R4, 1), F32), pltpu.VMEM((R4, 1), F32), pltpu.VMEM((R4, HD), F32),
               pltpu.VMEM((R4, LANES), F32), pltpu.VMEM((R4, LANES), F32), pltpu.VMEM((R4, LANES), F32),
               pltpu.VMEM((nb, R4, HD), F32)]
    return pl.pallas_call(
        functools.partial(_sattn_kernel, nb=nb, S=S, scale=HD ** -0.5),
        grid_spec=pltpu.PrefetchScalarGridSpec(
            num_scalar_prefetch=1, grid=(B, npg // P), in_specs=in_specs,
            out_specs=[out_spec, out_spec], scratch_shapes=scratch),
        out_shape=[jax.ShapeDtypeStruct(yb_into.shape, F32), jax.ShapeDtypeStruct(yc_into.shape, F32)],
        input_output_aliases={n_in - 2: 0, n_in - 1: 1},
        compiler_params=_cp(("arbitrary", "arbitrary")), name="sample_attn")(
            pt_flat, z, z, z, lf, z, z, z, R, *([fk] * P + [fv] * P + [mk] * P + [mv] * P), yb_into, yc_into)


def _pick_tile(m, cands):
    for t in cands:
        if m % t == 0:
            return t
    raise ValueError("no row tile divides %d" % m)


def _pad_cols(x, n):
    return jnp.pad(x, ((0, 0), (0, n - x.shape[1])))


def kernel(x_prompt, x_sample, c_prompt, c_sample, page_table, cache_fox_k, cache_fox_v, cache_fox_logf, cache_moba_k, cache_moba_v, state_rwkv, state_rwkv_shift, state_ret, w_ada, b_ada, w_in, rwkv_mu, rwkv_w0, rwkv_w2, rwkv_a0, rwkv_a2, rwkv_g2, rwkv_k_k, rwkv_k_a, rwkv_r_k, rwkv_ln_g, rwkv_ln_b, fox_bf, ret_ln_g, w_branch, w_o, ln1_g, ln1_b, ln2_g, ln2_b, w_ffn_gate, w_ffn_in, w_ffn_out):
    Bp, Tp, D = x_prompt.shape
    Bs, S, _ = x_sample.shape
    depth = w_in.shape[0]
    n_pool = cache_fox_k.shape[1]
    npg = page_table.shape[1]
    past = npg * PAGE_SIZE
    Mp, Ms = Bp * Tp, Bs * S
    M = Mp + Ms
    assert Bp == 1 and D == D_MODEL and depth == DEPTH
    assert Mp % TM == 0 and Ms % TM == 0 and Tp % MOBA_BLOCK == 0 and Tp // MOBA_BLOCK <= LANES
    assert Tp % (RWKV_CHUNK * RWKV_CHUNKS_PER_STEP) == 0 and Tp % RET_CHUNK == 0 and S == 8 and past % MOBA_BLOCK == 0
    assert TM == MOBA_BLOCK and 2 * NH == 8 and MOBA_BLOCK == 2 * PAGE_SIZE
    assert Tp % FLASH_TQ == 0 and FLASH_TQ % MOBA_BLOCK == 0
    assert npg % SAMPLE_PAGES == 0 and SAMPLE_PAGES % 2 == 0 and npg // 2 <= LANES
    n_p = Mp // TM
    W = BRANCH_W
    tm_big = _pick_tile(M, (768, 512, 256))
    tm_small = _pick_tile(M, (384, 256))

    x = jnp.concatenate([x_prompt.reshape(Mp, D), x_sample.reshape(Ms, D)], axis=0)
    c_rows = 8 * (-(-(Bp + Bs) // 8))
    c_all = jnp.pad(jnp.concatenate([c_prompt, c_sample], axis=0), ((0, c_rows - Bp - Bs), (0, 0)))
    pt_flat = page_table.reshape(-1).astype(jnp.int32)
    cache2 = lambda t: t.reshape(depth * n_pool * PAGE_SIZE * NH, HD)
    caches = (cache2(cache_fox_k), cache2(cache_fox_v), cache2(cache_moba_k), cache2(cache_moba_v))
    lf_rows = jnp.swapaxes(cache_fox_logf, 2, 3).reshape(depth * n_pool * NH, PAGE_SIZE)
    ts_m, tot_m = _suffix_matrices()
    e64 = jnp.asarray(np.kron(np.eye(A_HEADS), np.ones((A_HD, A_HD))), BF16)
    b_ada3 = b_ada.reshape(depth, 1, -1)
    vec3 = lambda t: t.reshape(depth, 1, -1)
    ln1_g3, ln1_b3, ln2_g3, ln2_b3, ret_g3 = map(vec3, (ln1_g, ln1_b, ln2_g, ln2_b, ret_ln_g))

    assert w_in.shape[2] == _O_G + 4 * D_MODEL
    tm_in = _pick_tile(M, (768, 512, 256))

    per_layer = []
    for l in range(depth):
        ada = _ada_call(c_all, w_ada, b_ada3, l)
        ada_p = ada[0:8]
        mod_s = jnp.repeat(ada[Bp:Bp + Bs], S, axis=0)
        h = _lnmod_call(x, ada_p, mod_s, n_p, 1, 0)
        z = _win_call(h, w_in, l, tm_in)

        fill_s = jnp.repeat(_pad_cols(state_rwkv_shift[l], 2048), S, axis=0)
        pad_rows = lambda w2, r0: jnp.zeros((W, W), F32).at[r0:r0 + w2.shape[0]].set(w2)
        pre = _rwkv_pre_call(
            z, fill_s, _pad_cols(rwkv_mu[l][None], 2048), rwkv_w0[l][None], rwkv_a0[l][None],
            rwkv_k_k[l][None], rwkv_k_a[l][None], pad_rows(rwkv_w2[l], 0), pad_rows(rwkv_a2[l], A_LORA_W),
            pad_rows(rwkv_g2[l], A_LORA_W + A_LORA_A), e64, n_p, S)
        hv = lambda t: t.reshape(A_HEADS, 1, A_HD)
        rk, lg, lb = hv(rwkv_r_k[l]), hv(rwkv_ln_g[l]), hv(rwkv_ln_b[l])
        blank = jnp.zeros((M, W), F32)
        ya, sT_p = _rwkv_chunk_call(pre, jnp.zeros((Bp, A_HEADS, A_HD, A_HD), F32), rk, lg, lb,
                                    0, Bp, Tp, RWKV_CHUNK, RWKV_CHUNKS_PER_STEP, blank)
        ya, sT_s = _rwkv_chunk_call(pre, jnp.swapaxes(state_rwkv[l], -1, -2), rk, lg, lb, Mp, Bs, S, S, 1, ya)

        lf, cum = _foxcum_call(z, _pad_cols(fox_bf[l][None], LANES))
        nbk = Tp // MOBA_BLOCK
        kmean = _blockmean_call(z, Tp).reshape(nbk, W)
        kmean = jnp.pad(kmean, ((0, 8 * (-(-nbk // 8)) - nbk), (0, 0)))
        fqT, fkA, fvT, mqT, mkb, mvT, sel, fk_rows, fv_rows, mk_rows, mv_rows = _attn_prep_call(z, cum, kmean, Tp)
        tile_base = lambda t: jnp.transpose(jnp.concatenate(
            [jnp.zeros((1, NH), F32), cum[t - 1:Mp - 1:t, :NH]], axis=0)).reshape(-1)
        base_q, base_k = tile_base(FLASH_TQ), tile_base(MOBA_BLOCK)
        yb = _flash_call(fqT, fkA, fvT, base_q, base_k, None, Tp, blank, "fox_prompt")
        yc = _flash_call(mqT, mkb, mvT, base_q, base_k, sel, Tp, blank, "moba_prompt")
        R = _lfsuf_call(pt_flat, lf_rows, ts_m, tot_m, Bs, npg, l * n_pool)
        yb, yc = _sattn_call(pt_flat, z, lf, R, caches, l, Mp, Bs, S, npg, n_pool, yb, yc)

        yd, rS_p = _ret_call(z, jnp.zeros((Bp, NH, HD, HD), F32), ret_g3, l, 0, Bp, Tp, RET_CHUNK, 0, blank)
        yd, rS_s = _ret_call(z, state_ret[l], ret_g3, l, Mp, Bs, S, S, past, yd)

        mixin = _merge_call((ya, yb, yc, yd), w_branch, z, l, tm_big, 512)
        mix = _mm_call(mixin, w_o, l, tm_big, 512, F32, "w_o")
        x1, h2 = _resid_call(x, mix, ada_p, mod_s, ln1_g3, ln1_b3, l, n_p, 2, 4, 3)
        act = _ffn_call(h2, w_ffn_gate, w_ffn_in, l, tm_big, 512)
        f = _mm_call(act, w_ffn_out, l, tm_small, 512, F32, "ffn_out")
        x = _resid_call(x1, f, ada_p, mod_s, ln2_g3, ln2_b3, l, n_p, 5)

        zb = Z_B
        zc = Z_C
        hp = lambda t: t.reshape(Bp, Tp, NH, HD)
        hs = lambda t: t.reshape(Bs, S, NH, HD)
        per_layer.append(dict(
            fox_k_p=hp(fk_rows), fox_v_p=hp(fv_rows),
            fox_lf_p=lf[:Mp, :NH].reshape(Bp, Tp, NH),
            moba_k_p=hp(mk_rows), moba_v_p=hp(mv_rows),
            rwkv_S_p=jnp.swapaxes(sT_p, -1, -2), shift_p=z[Mp - 1:Mp, Z_A:Z_A + A_COLS], ret_S_p=rS_p,
            fox_k_s=hs(z[Mp:, zb + W:zb + 2 * W]), fox_v_s=hs(z[Mp:, zb + 2 * W:zb + 3 * W]),
            fox_lf_s=lf[Mp:, :NH].reshape(Bs, S, NH),
            moba_k_s=hs(z[Mp:, zc + W:zc + 2 * W]), moba_v_s=hs(z[Mp:, zc + 2 * W:zc + 3 * W]),
            rwkv_S_s=jnp.swapaxes(sT_s, -1, -2),
            shift_s=z[Mp:, Z_A:Z_A + A_COLS].reshape(Bs, S, A_COLS)[:, S - 1], ret_S_s=rS_s))

    st = lambda name: jnp.stack([p[name] for p in per_layer])
    return (x[:Mp].reshape(Bp, Tp, D), x[Mp:].reshape(Bs, S, D),
            st("fox_k_p"), st("fox_v_p"), st("fox_lf_p"), st("moba_k_p"), st("moba_v_p"),
            st("rwkv_S_p"), st("shift_p"), st("ret_S_p"),
            st("fox_k_s"), st("fox_v_s"), st("fox_lf_s"), st("moba_k_s"), st("moba_v_s"),
            st("rwkv_S_s"), st("shift_s"), st("ret_S_s"))
```

```python
import functools

import numpy as np
import jax
import jax.numpy as jnp
from jax import lax
from jax.experimental import pallas as pl
from jax.experimental.pallas import tpu as pltpu

F32 = jnp.float32
BF16 = jnp.bfloat16

D_MODEL = 2048
DEPTH = 2
PAGE_SIZE = 128
BRANCH_W = 512
A_HD = 64
A_HEADS = 8
A_LORA_W = 96
A_LORA_A = 96
A_LORA_G = 256
A_COLS = 3 * BRANCH_W + A_LORA_W + A_LORA_A + A_LORA_G
B_COLS = 3 * BRANCH_W + 4
C_COLS = 3 * BRANCH_W
D_COLS = 4 * BRANCH_W
HD = 128
NH = 4
RWKV_GN_EPS = 64e-5
RET_GN_EPS = 1e-6
LN_EPS = 1e-5
MOBA_BLOCK = 256
MOBA_TOPK = 3
RET_CHUNK = 128
RWKV_CHUNK = 64
RWKV_CHUNKS_PER_STEP = 4
ROPE_BASE = 10000.0
FFN_HIDDEN = 5632
ALPHA = (2 * DEPTH) ** 0.25

Z_G = 0
Z_A = 4 * D_MODEL
Z_D = Z_A + 2048
Z_C = Z_D + 2048
Z_B = Z_C + 2048
Z_COLS = Z_B + 2048

TM = 256
FLASH_TQ = 1024
SAMPLE_PAGES = 4
RING = 3
VROWS = HD + 16
LANES = 128
NEG = -1e30
LOG2E = 1.4426950408889634
VMEM_LIMIT = 56 << 20

NN = (((1,), (0,)), ((), ()))
NT = (((1,), (1,)), ((), ()))
TN = (((0,), (0,)), ((), ()))


def _cp(sem):
    return pltpu.CompilerParams(dimension_semantics=sem, vmem_limit_bytes=VMEM_LIMIT)


def _dot(a, b, dims=NN):
    return lax.dot_general(a, b, dims, preferred_element_type=F32)


def _split2(x):
    hi = x.astype(BF16)
    return hi, (x - hi.astype(F32)).astype(BF16)


def _split3(x):
    hi = x.astype(BF16)
    r1 = x - hi.astype(F32)
    mid = r1.astype(BF16)
    return hi, mid, (r1 - mid.astype(F32)).astype(BF16)


def _dot3(a, b, dims=NN):
    ah, al = _split2(a)
    bh, bl = _split2(b)
    return _dot(ah, bh, dims) + (_dot(ah, bl, dims) + _dot(al, bh, dims))


def _dot1(a, b, dims=NN):
    return _dot(a.astype(BF16), b.astype(BF16), dims)


def _dot_exact_l(m_bf16, x, n=3):
    parts = _split3(x) if n == 3 else _split2(x)
    acc = _dot(m_bf16, parts[0])
    for p in parts[1:]:
        acc = acc + _dot(m_bf16, p)
    return acc


def _dot_exact_r(x, m_bf16, n=3):
    parts = _split3(x) if n == 3 else _split2(x)
    acc = _dot(parts[0], m_bf16)
    for p in parts[1:]:
        acc = acc + _dot(p, m_bf16)
    return acc


def _ln(x, eps):
    mu = jnp.mean(x, axis=-1, keepdims=True)
    xc = x - mu
    var = jnp.mean(xc * xc, axis=-1, keepdims=True)
    return xc * lax.rsqrt(var + eps)


def _sigmoid(x):
    return 0.5 * jnp.tanh(0.5 * x) + 0.5


def _log_sigmoid(x):
    return jnp.minimum(x, 0.0) - jnp.log(1.0 + jnp.exp(-jnp.abs(x)))


def _iota(shape, dim):
    return lax.broadcasted_iota(jnp.int32, shape, dim)


def _ada_kernel(c_ref, w_ref, b_ref, o_ref):
    c = c_ref[...]
    s = (c * _sigmoid(c)).astype(BF16)
    o_ref[...] = _dot(s, w_ref[...].astype(BF16)) + b_ref[...]


def _ada_call(c_all, w_ada, b_ada, l):
    R, D = c_all.shape
    N = w_ada.shape[2]
    tn = 2048
    return pl.pallas_call(
        _ada_kernel, grid=(N // tn,),
        in_specs=[pl.BlockSpec((R, D), lambda j: (0, 0)),
                  pl.BlockSpec((None, D, tn), lambda j: (l, 0, j)),
                  pl.BlockSpec((None, 1, tn), lambda j: (l, 0, j))],
        out_specs=pl.BlockSpec((R, tn), lambda j: (0, j)),
        out_shape=jax.ShapeDtypeStruct((R, N), F32),
        compiler_params=_cp(("arbitrary",)), name="ada")(c_all, w_ada, b_ada)


def _lnmod_kernel(x_ref, scp_ref, shp_ref, scs_ref, shs_ref, o_ref, *, n_p):
    is_s = pl.program_id(0) >= n_p
    sc = jnp.where(is_s, scs_ref[...], scp_ref[0:1, :])
    sh = jnp.where(is_s, shs_ref[...], shp_ref[0:1, :])
    o_ref[...] = (_ln(x_ref[...], LN_EPS) * (1.0 + sc) + sh).astype(BF16)


def _mod_specs(n_p, cols):
    specs = []
    for cb in cols:
        specs.append(pl.BlockSpec((8, D_MODEL), lambda i, cb=cb: (0, cb)))
    for cb in cols:
        specs.append(pl.BlockSpec((TM, D_MODEL), lambda i, cb=cb: (jnp.maximum(i - n_p, 0), cb)))
    return specs


def _lnmod_call(x, ada_p, mod_s, n_p, sc_col, sh_col):
    M, D = x.shape
    return pl.pallas_call(
        functools.partial(_lnmod_kernel, n_p=n_p), grid=(M // TM,),
        in_specs=[pl.BlockSpec((TM, D), lambda i: (i, 0))] + _mod_specs(n_p, (sc_col, sh_col)),
        out_specs=pl.BlockSpec((TM, D), lambda i: (i, 0)),
        out_shape=jax.ShapeDtypeStruct((M, D), BF16),
        compiler_params=_cp(("arbitrary",)), name="lnmod")(x, ada_p, ada_p, mod_s, mod_s)


def _mm_kernel(a_ref, w_ref, o_ref, wb_ref):
    @pl.when(pl.program_id(1) == 0)
    def _():
        wb_ref[...] = w_ref[...].astype(BF16)

    o_ref[...] = _dot(a_ref[...].astype(BF16), wb_ref[...]).astype(o_ref.dtype)


def _win_kernel(st_ref, a_ref, w_ref, o_ref, wb_ref, *, l):
    @pl.when(pl.program_id(1) == 0)
    def _():
        wb_ref[...] = w_ref[:, l, :].T.astype(BF16)

    o_ref[...] = _dot(a_ref[...], wb_ref[...])


W_IN_TN = 1024
_O_B, _O_C, _O_D = A_COLS, A_COLS + B_COLS, A_COLS + B_COLS + C_COLS
_O_G = _O_D + D_COLS
_W_IN_TILE_SRC = np.concatenate([src + W_IN_TN * np.arange(width // W_IN_TN) for src, width in (
    (_O_G, 4 * D_MODEL), (0, Z_D - Z_A), (_O_D, Z_C - Z_D), (_O_C, Z_B - Z_C), (_O_B, Z_COLS - Z_B))]).astype(np.int32)


def _win_call(a, w_in, l, tm):
    M, K = a.shape
    depth = w_in.shape[0]
    w_t = jnp.transpose(w_in, (2, 0, 1))
    tn = W_IN_TN
    nt = Z_COLS // tn
    assert int(_W_IN_TILE_SRC.max()) + tn <= w_in.shape[2] and len(_W_IN_TILE_SRC) == nt
    return pl.pallas_call(
        functools.partial(_win_kernel, l=l),
        grid_spec=pltpu.PrefetchScalarGridSpec(
            num_scalar_prefetch=1, grid=(nt, M // tm),
            in_specs=[pl.BlockSpec((tm, K), lambda j, i, st: (i, 0)),
                      pl.BlockSpec((pl.Element(tn), pl.Element(depth), pl.Element(K)),
                                   lambda j, i, st: (st[j], 0, 0))],
            out_specs=pl.BlockSpec((tm, tn), lambda j, i, st: (i, j)),
            scratch_shapes=[pltpu.VMEM((K, tn), BF16)]),
        out_shape=jax.ShapeDtypeStruct((M, Z_COLS), F32),
        compiler_params=_cp(("arbitrary", "arbitrary")), name="w_in")(jnp.asarray(_W_IN_TILE_SRC), a, w_t)


def _mm_call(a, w, w_index, tm, tn, out_dtype, name):
    M, K = a.shape
    N = w.shape[-1]
    if w.ndim == 3:
        w_spec = pl.BlockSpec((None, K, tn), lambda j, i: (w_index, 0, j))
    else:
        w_spec = pl.BlockSpec((K, tn), lambda j, i: (0, j))
    return pl.pallas_call(
        _mm_kernel, grid=(N // tn, M // tm),
        in_specs=[pl.BlockSpec((tm, K), lambda j, i: (i, 0)), w_spec],
        out_specs=pl.BlockSpec((tm, tn), lambda j, i: (i, j)),
        out_shape=jax.ShapeDtypeStruct((M, N), out_dtype),
        scratch_shapes=[pltpu.VMEM((K, tn), BF16)],
        compiler_params=_cp(("arbitrary", "arbitrary")), name=name)(a, w)


def _resid_kernel(*refs, n_p, with_h):
    if with_h:
        (x_ref, mix_ref, gp_ref, scp_ref, shp_ref, gs_ref, scs_ref, shs_ref, lg_ref, lb_ref,
         x1_ref, h_ref) = refs
    else:
        x_ref, mix_ref, gp_ref, gs_ref, lg_ref, lb_ref, x1_ref = refs
    is_s = pl.program_id(0) >= n_p
    g = jnp.where(is_s, gs_ref[...], gp_ref[0:1, :])
    y = ALPHA * x_ref[...] + (1.0 + g) * mix_ref[...]
    x1 = _ln(y, LN_EPS) * lg_ref[...] + lb_ref[...]
    x1_ref[...] = x1
    if with_h:
        sc = jnp.where(is_s, scs_ref[...], scp_ref[0:1, :])
        sh = jnp.where(is_s, shs_ref[...], shp_ref[0:1, :])
        h_ref[...] = (_ln(x1, LN_EPS) * (1.0 + sc) + sh).astype(BF16)


def _resid_call(x, mix, ada_p, mod_s, ln_g, ln_b, l, n_p, g_col, sc_col=None, sh_col=None):
    M, D = x.shape
    with_h = sc_col is not None
    cols = (g_col, sc_col, sh_col) if with_h else (g_col,)
    row = pl.BlockSpec((TM, D), lambda i: (i, 0))
    vec = pl.BlockSpec((None, 1, D), lambda i: (l, 0, 0))
    in_specs = [row, row] + _mod_specs(n_p, cols) + [vec, vec]
    args = [x, mix] + [ada_p] * len(cols) + [mod_s] * len(cols) + [ln_g, ln_b]
    out_shape = [jax.ShapeDtypeStruct((M, D), F32)]
    out_specs = [row]
    if with_h:
        out_shape.append(jax.ShapeDtypeStruct((M, D), BF16))
        out_specs.append(row)
    res = pl.pallas_call(
        functools.partial(_resid_kernel, n_p=n_p, with_h=with_h), grid=(M // TM,),
        in_specs=in_specs, out_specs=out_specs, out_shape=out_shape,
        compiler_params=_cp(("arbitrary",)), name="resid_h" if with_h else "resid")(*args)
    return res if with_h else res[0]


def _ffn_kernel(h_ref, wg_ref, wi_ref, o_ref, wgb_ref, wib_ref):
    @pl.when(pl.program_id(1) == 0)
    def _():
        wgb_ref[...] = wg_ref[...].astype(BF16)
        wib_ref[...] = wi_ref[...].astype(BF16)

    h = h_ref[...]
    a = _dot(h, wgb_ref[...])
    b = _dot(h, wib_ref[...])
    o_ref[...] = (a * _sigmoid(a) * b).astype(BF16)


def _ffn_call(h, wg, wi, l, tm, tn):
    M, K = h.shape
    N = wg.shape[-1]
    w_spec = pl.BlockSpec((None, K, tn), lambda j, i: (l, 0, j))
    return pl.pallas_call(
        _ffn_kernel, grid=(N // tn, M // tm),
        in_specs=[pl.BlockSpec((tm, K), lambda j, i: (i, 0)), w_spec, w_spec],
        out_specs=pl.BlockSpec((tm, tn), lambda j, i: (i, j)),
        out_shape=jax.ShapeDtypeStruct((M, N), BF16),
        scratch_shapes=[pltpu.VMEM((K, tn), BF16), pltpu.VMEM((K, tn), BF16)],
        compiler_params=_cp(("arbitrary", "arbitrary")), name="ffn_act")(h, wg, wi)


def _merge_kernel(ya_ref, yb_ref, yc_ref, yd_ref, wb_ref, g0_ref, g1_ref, g2_ref, g3_ref, o_ref, wbb_ref):
    @pl.when(pl.program_id(1) == 0)
    def _():
        wbb_ref[...] = wb_ref[...].astype(BF16)

    acc = None
    for n, (y_ref, g_ref) in enumerate(((ya_ref, g0_ref), (yb_ref, g1_ref), (yc_ref, g2_ref), (yd_ref, g3_ref))):
        up = _dot(y_ref[...].astype(BF16), wbb_ref[n])
        t = _sigmoid(g_ref[...]) * up
        acc = t if acc is None else acc + t
    o_ref[...] = acc.astype(BF16)


def _merge_call(ys, w_branch, z, l, tm, tn):
    M = z.shape[0]
    W = BRANCH_W
    D = D_MODEL
    nj = D // tn
    y_spec = pl.BlockSpec((tm, W), lambda j, i: (i, 0))
    g_specs = [pl.BlockSpec((tm, tn), lambda j, i, n=n: (i, (Z_G + n * D) // tn + j)) for n in range(4)]
    return pl.pallas_call(
        _merge_kernel, grid=(nj, M // tm),
        in_specs=[y_spec] * 4 + [pl.BlockSpec((None, 4, W, tn), lambda j, i: (l, 0, 0, j))] + g_specs,
        out_specs=pl.BlockSpec((tm, tn), lambda j, i: (i, j)),
        out_shape=jax.ShapeDtypeStruct((M, D), BF16),
        scratch_shapes=[pltpu.VMEM((4, W, tn), BF16)],
        compiler_params=_cp(("arbitrary", "arbitrary")), name="merge")(*ys, w_branch, z, z, z, z)


def _rwkv_pre_kernel(za_ref, prev_ref, fill_ref, mu_ref, w0_ref, a0_ref, kk_ref, ka_ref, w2_ref, a2_ref,
                     g2_ref, e_ref, r_o, lw_o, k_o, v_o, kn_o, b_o, g_o, *, n_p, seq_s):
    i = pl.program_id(0)
    is_s = i >= n_p
    za = za_ref[...]
    rows = _iota(za.shape, 0)
    prev = jnp.where(rows == 0, prev_ref[7:8, :], pltpu.roll(za, 1, axis=0))
    base = jnp.where(is_s, n_p * TM, 0)
    pmask = jnp.where(is_s, seq_s - 1, 0x3FFFFFFF)
    start = ((rows + (i * TM - base)) & pmask) == 0
    fill = jnp.where(is_s, fill_ref[...], 0.0)
    prev = jnp.where(start, fill, prev)
    zm = za + mu_ref[...] * (prev - za)
    W = BRANCH_W
    r = zm[:, 0:W]
    k = zm[:, W:2 * W]
    v = zm[:, 2 * W:3 * W]
    x = zm[:, 3 * W:4 * W]
    wl = _dot(jnp.tanh(x).astype(BF16), w2_ref[...].astype(BF16))
    y = -(w0_ref[...] + wl)
    softplus = jnp.maximum(y, 0.0) + jnp.log(1.0 + jnp.exp(-jnp.abs(y)))
    w = -softplus - 0.5
    logw = -jnp.exp(w)
    a = _sigmoid(a0_ref[...] + _dot(x.astype(BF16), a2_ref[...].astype(BF16)))
    g = _dot(_sigmoid(x).astype(BF16), g2_ref[...].astype(BF16))
    kk0 = k * kk_ref[...]
    ss = _dot_exact_r(kk0 * kk0, e_ref[...], n=3)
    kn = kk0 / jnp.maximum(jnp.sqrt(ss), 1e-12)
    k2 = k * (1.0 + (a - 1.0) * ka_ref[...])
    b = kn * a
    for h in range(A_HEADS):
        sl = slice(h * A_HD, (h + 1) * A_HD)
        r_o[h] = r[:, sl]
        lw_o[h] = logw[:, sl]
        k_o[h] = k2[:, sl]
        v_o[h] = v[:, sl]
        kn_o[h] = kn[:, sl]
        b_o[h] = b[:, sl]
        g_o[h] = g[:, sl]


def _rwkv_pre_call(z, fill_s, mu, w0, a0, k_k, k_a, w2p, a2p, g2p, e64, n_p, seq_s):
    M = z.shape[0]
    W = BRANCH_W
    cb = Z_A // 2048
    vec = lambda n: pl.BlockSpec((1, n), lambda i: (0, 0))
    mat = pl.BlockSpec((W, W), lambda i: (0, 0))
    out = jax.ShapeDtypeStruct((A_HEADS, M, A_HD), F32)
    ospec = pl.BlockSpec((A_HEADS, TM, A_HD), lambda i: (0, i, 0))
    return pl.pallas_call(
        functools.partial(_rwkv_pre_kernel, n_p=n_p, seq_s=seq_s), grid=(M // TM,),
        in_specs=[pl.BlockSpec((TM, 2048), lambda i: (i, cb)),
                  pl.BlockSpec((8, 2048), lambda i: (jnp.maximum(i * (TM // 8) - 1, 0), cb)),
                  pl.BlockSpec((TM, 2048), lambda i: (jnp.maximum(i - n_p, 0), 0)),
                  vec(2048), vec(W), vec(W), vec(W), vec(W), mat, mat, mat, mat],
        out_specs=[ospec] * 7, out_shape=[out] * 7,
        compiler_params=_cp(("arbitrary",)), name="rwkv_pre")(
            z, z, fill_s, mu, w0, a0, k_k, k_a, w2p, a2p, g2p, e64)


def _rwkv_chunk_kernel(r_ref, lw_ref, k_ref, v_ref, kn_ref, b_ref, g_ref, s0_ref, rk_ref, lng_ref, lnb_ref,
                       *rest, C, Q, nc):
    y_ref, sfin_ref, S_ref = rest[-3:]
    c = pl.program_id(1)

    @pl.when(c == 0)
    def _():
        S_ref[...] = s0_ref[...]

    H = range(A_HEADS)
    row = _iota((C, C), 0)
    col = _iota((C, C), 1)
    tri = (row >= col).astype(BF16)
    eye = (row == col).astype(F32)
    row2 = _iota((2 * C, C), 0)
    col2 = _iota((2 * C, C), 1)
    low2 = jnp.where(row2 < C, row2 - 1, row2 - C) >= col2
    eye_k = _iota((A_HD, A_HD), 0) == _iota((A_HD, A_HD), 1)
    HQ = [(h, q) for q in range(Q) for h in H]
    take = lambda ref: {(h, q): ref[h, q * C:(q + 1) * C, :] for h, q in HQ}
    r, lw, k, v, kn, b, g = (take(ref) for ref in (r_ref, lw_ref, k_ref, v_ref, kn_ref, b_ref, g_ref))
    L = {p: _dot_exact_l(tri, lw[p]) for p in HQ}
    eL = {p: jnp.exp(L[p]) for p in HQ}
    eN = {p: jnp.exp(-L[p]) for p in HQ}
    eE = {p: jnp.exp(L[p][C - 1:C, :] - L[p]) for p in HQ}
    lhs = {p: jnp.concatenate([kn[p] * jnp.exp(L[p] - lw[p]), r[p] * eL[p]], axis=0) for p in HQ}
    Ab = {p: jnp.where(low2, _dot1(lhs[p], b[p] * eN[p], NT), 0.0) for p in HQ}
    Ak = {p: jnp.where(low2, _dot1(lhs[p], k[p] * eN[p], NT), 0.0) for p in HQ}
    N = {p: Ab[p][0:C] for p in HQ}
    X = {p: eye - N[p] for p in HQ}
    P = {p: _dot1(N[p], N[p]) for p in HQ}
    n = 2
    while n < C:
        X = {p: X[p] + _dot1(X[p], P[p]) for p in HQ}
        n *= 2
        if n < C:
            P = {p: _dot1(P[p], P[p]) for p in HQ}
    AV = {p: _dot1(Ak[p], v[p]) for p in HQ}
    gcol = {p: jnp.sum(jnp.where(eye_k, jnp.broadcast_to(eL[p][C - 1:C, :], (A_HD, A_HD)), 0.0), axis=1,
                       keepdims=True) for p in HQ}
    kb = {p: jnp.concatenate([k[p] * eE[p], -(b[p] * eE[p])], axis=0) for p in HQ}
    bonus = {p: jnp.sum(r[p] * k[p] * rk_ref[p[0]], axis=1, keepdims=True) * v[p] for p in HQ}
    S = [S_ref[h] for h in H]
    rows = []
    for q in range(Q):
        KS = [_dot1(lhs[h, q], S[h]) for h in H]
        U = [_dot1(X[h, q], KS[h][0:C] + AV[h, q][0:C]) for h in H]
        Y = [KS[h][C:] + AV[h, q][C:] - _dot1(Ab[h, q][C:], U[h]) for h in H]
        S = [S[h] * gcol[h, q] + _dot1(kb[h, q], jnp.concatenate([v[h, q], U[h]], axis=0), TN) for h in H]
        rows.append(jnp.concatenate(
            [(_ln(Y[h], RWKV_GN_EPS) * lng_ref[h] + lnb_ref[h] + bonus[h, q]) * g[h, q] for h in H], axis=1))
    y_ref[...] = jnp.concatenate(rows, axis=0) if Q > 1 else rows[0]
    for h in H:
        S_ref[h] = S[h]

    @pl.when(c == nc - 1)
    def _():
        for h in H:
            sfin_ref[h] = S[h]


def _rwkv_chunk_call(pre, s0t, r_k, ln_g, ln_b, row0, B, T, C, Q, into):
    R = C * Q
    nc = T // R
    blk0 = row0 // R
    in_spec = pl.BlockSpec((A_HEADS, R, A_HD), lambda b, c: (0, blk0 + b * nc + c, 0))
    hvec = pl.BlockSpec((A_HEADS, 1, A_HD), lambda b, c: (0, 0, 0))
    st = pl.BlockSpec((None, A_HEADS, A_HD, A_HD), lambda b, c: (b, 0, 0, 0))
    return pl.pallas_call(
        functools.partial(_rwkv_chunk_kernel, C=C, Q=Q, nc=nc), grid=(B, nc),
        in_specs=[in_spec] * 7 + [st, hvec, hvec, hvec, pl.BlockSpec(memory_space=pl.ANY)],
        out_specs=[pl.BlockSpec((R, BRANCH_W), lambda b, c: (blk0 + b * nc + c, 0)), st],
        out_shape=[jax.ShapeDtypeStruct(into.shape, F32),
                   jax.ShapeDtypeStruct((B, A_HEADS, A_HD, A_HD), F32)],
        scratch_shapes=[pltpu.VMEM((A_HEADS, A_HD, A_HD), F32)],
        input_output_aliases={11: 0},
        compiler_params=_cp(("arbitrary", "arbitrary")), name="rwkv_chunk_c%d" % C)(
            *pre, s0t, r_k, ln_g, ln_b, into)


def _foxcum_kernel(fl_ref, bf_ref, lf_ref, cum_ref, carry_ref):
    @pl.when(pl.program_id(0) == 0)
    def _():
        carry_ref[...] = jnp.zeros_like(carry_ref)

    lane = _iota((TM, LANES), 1)
    lf = jnp.where(lane < NH, _log_sigmoid(fl_ref[...] + bf_ref[...]), 0.0)
    tri = (_iota((TM, TM), 0) >= _iota((TM, TM), 1)).astype(BF16)
    cum = _dot_exact_l(tri, lf) + carry_ref[...]
    lf_ref[...] = lf
    cum_ref[...] = cum
    carry_ref[...] = cum[TM - 1:TM, :]


def _foxcum_call(z, bf):
    M = z.shape[0]
    cb = (Z_B + 3 * BRANCH_W) // LANES
    spec = pl.BlockSpec((TM, LANES), lambda i: (i, 0))
    return pl.pallas_call(
        _foxcum_kernel, grid=(M // TM,),
        in_specs=[pl.BlockSpec((TM, LANES), lambda i: (i, cb)), pl.BlockSpec((1, LANES), lambda i: (0, 0))],
        out_specs=[spec, spec], out_shape=[jax.ShapeDtypeStruct((M, LANES), F32)] * 2,
        scratch_shapes=[pltpu.VMEM((1, LANES), F32)],
        compiler_params=_cp(("arbitrary",)), name="fox_cum")(z, bf)


def _attn_prep_kernel(fq_ref, fk_ref, fv_ref, mq_ref, mk_ref, mv_ref, cum_ref, cprev_ref, cqprev_ref, km_ref,
                      fqT_o, fkA_o, fvT_o, mqT_o, mk_o, mvT_o, sel_o, fk_rows, fv_rows, mk_rows, mv_rows, *, scale):
    i = pl.program_id(0)
    lane = _iota((TM, LANES), 1)
    cum = cum_ref[...]
    qscale = scale * LOG2E
    crel_k = (cum - jnp.where(i == 0, 0.0, cprev_ref[7:8, :])) * LOG2E
    crel_q = (cum - jnp.where(i < FLASH_TQ // TM, 0.0, cqprev_ref[7:8, :])) * LOG2E
    fq = fq_ref[...]
    fk = fk_ref[...]
    fv = fv_ref[...]
    mq = mq_ref[...]
    mk = mk_ref[...]
    mv = mv_ref[...]
    km = km_ref[...]
    blk = _iota((km.shape[0], TM), 0)
    blk_f = blk.astype(F32)
    ones_rows = jnp.where(_iota((VROWS - HD, TM), 0) == 0, 1.0, 0.0).astype(BF16)
    for h in range(NH):
        sl = slice(h * HD, (h + 1) * HD)
        q3 = [p.astype(F32) for p in _split3(jnp.broadcast_to(crel_q[:, h:h + 1], (TM, LANES)))]
        k3 = [p.astype(F32) for p in _split3(jnp.broadcast_to(crel_k[:, h:h + 1], (TM, LANES)))]
        qb = jnp.where(lane == 0, q3[0], jnp.where(lane == 1, q3[1], jnp.where(lane == 2, q3[2],
                                                                             jnp.where(lane < 6, 1.0, 0.0))))
        kb = jnp.where(lane < 3, 1.0, jnp.where(lane == 3, -k3[0], jnp.where(lane == 4, -k3[1],
                                                                              jnp.where(lane == 5, -k3[2], 0.0))))
        fqT_o[h, 0:HD, :] = (fq[:, sl] * qscale).T.astype(BF16)
        fqT_o[h, HD:2 * HD, :] = qb.T.astype(BF16)
        for rows_o, src in ((fk_rows, fk), (fv_rows, fv), (mk_rows, mk), (mv_rows, mv)):
            rows_o[pl.ds(h, TM, stride=NH), :] = src[:, sl]
        fkA_o[h, :, 0:HD] = fk[:, sl].astype(BF16)
        fkA_o[h, :, HD:2 * HD] = kb.astype(BF16)
        fvT_o[h, 0:HD, :] = fv[:, sl].T.astype(BF16)
        fvT_o[h, HD:VROWS, :] = ones_rows
        mqT = mq[:, sl].T
        mqT_o[h] = (mqT * qscale).astype(BF16)
        mk_o[h] = mk[:, sl].astype(BF16)
        mvT_o[h, 0:HD, :] = mv[:, sl].T.astype(BF16)
        mvT_o[h, HD:VROWS, :] = ones_rows
        sc = _dot3(km[:, sl], mqT)
        sel_o[h] = _top3_select(sc, blk < i, blk_f, axis=0)


def _attn_prep_call(z, cum, kmean, Tp):
    W = BRANCH_W
    nbp = kmean.shape[0]
    cbB = Z_B // W
    cbC = Z_C // W
    rq = FLASH_TQ // TM
    zs = lambda cb: pl.BlockSpec((TM, W), lambda i: (i, cb))
    colT = lambda rows: pl.BlockSpec((NH, rows, TM), lambda i: (0, 0, i))
    rowm = lambda cols: pl.BlockSpec((NH, TM, cols), lambda i: (0, i, 0))
    sd = jax.ShapeDtypeStruct
    return pl.pallas_call(
        functools.partial(_attn_prep_kernel, scale=HD ** -0.5), grid=(Tp // TM,),
        in_specs=[zs(cbB), zs(cbB + 1), zs(cbB + 2), zs(cbC), zs(cbC + 1), zs(cbC + 2),
                  pl.BlockSpec((TM, LANES), lambda i: (i, 0)),
                  pl.BlockSpec((8, LANES), lambda i: (jnp.maximum(i * (TM // 8) - 1, 0), 0)),
                  pl.BlockSpec((8, LANES), lambda i: (jnp.maximum((i // rq) * (FLASH_TQ // 8) - 1, 0), 0)),
                  pl.BlockSpec((nbp, W), lambda i: (0, 0))],
        out_specs=[colT(2 * HD), rowm(2 * HD), colT(VROWS), colT(HD), rowm(HD), colT(VROWS), colT(nbp)]
        + [pl.BlockSpec((TM * NH, HD), lambda i: (i, 0))] * 4,
        out_shape=[sd((NH, 2 * HD, Tp), BF16), sd((NH, Tp, 2 * HD), BF16), sd((NH, VROWS, Tp), BF16),
                   sd((NH, HD, Tp), BF16), sd((NH, Tp, HD), BF16), sd((NH, VROWS, Tp), BF16),
                   sd((NH, nbp, Tp), F32)] + [sd((Tp * NH, HD), F32)] * 4,
        compiler_params=_cp(("arbitrary",)), name="attn_prep")(z, z, z, z, z, z, cum, cum, cum, kmean)


def _flash_kernel(ti_ref, tj_ref, bq_ref, bk_ref, qT_ref, kA_ref, vT_ref, *rest, moba):
    sel_ref = rest[0] if moba else None
    o_ref, m_ref, acc_ref = rest[-3:]
    step = pl.program_id(0)
    i = ti_ref[step]
    j = tj_ref[step]
    T = kA_ref.shape[1]
    Tq = qT_ref.shape[2]
    ratio = Tq // T
    nqt = bq_ref.shape[0] // NH
    nkt = bk_ref.shape[0] // NH

    @pl.when(j == 0)
    def _():
        m_ref[...] = jnp.full(m_ref.shape, NEG, F32)
        acc_ref[...] = jnp.zeros(acc_ref.shape, F32)

    def step_body(on_diagonal):
        H = range(NH)
        sT = {h: _dot(kA_ref[h], qT_ref[h]) for h in H}
        if on_diagonal:
            causal = (_iota((T, Tq), 0) - _iota((T, Tq), 1)) <= (i * Tq - j * T)
            sT = {h: jnp.where(causal, sT[h], NEG) for h in H}
        if moba:
            own = (_iota((1, Tq), 1) // T + i * ratio) == j
            sT = {h: sT[h] + (jnp.where(own, 1.0, sel_ref[h, pl.ds(j, 1), :]) - 1.0) * (-NEG) for h in H}
            off = {h: 0.0 for h in H}
        else:
            off = {h: (bq_ref[h * nqt + i] - bk_ref[h * nkt + j]) * LOG2E for h in H}
        m_prev = {h: m_ref[h] for h in H}
        m_new = {h: jnp.maximum(m_prev[h], jnp.max(sT[h], axis=0, keepdims=True) + off[h]) for h in H}
        p = {h: jnp.exp2((sT[h] - (m_new[h] - off[h])).astype(BF16)) for h in H}
        alpha = {h: jnp.exp2(m_prev[h] - m_new[h]) for h in H}
        pv = {h: _dot(vT_ref[h], p[h]) for h in H}
        for h in H:
            acc_ref[h] = alpha[h] * acc_ref[h] + pv[h]
            m_ref[h] = m_new[h]

    crosses = (j + 1) * T - 1 > i * Tq
    pl.when(crosses)(functools.partial(step_body, True))
    pl.when(jnp.logical_not(crosses))(functools.partial(step_body, False))

    @pl.when(j == (i + 1) * ratio - 1)
    def _():
        for h in range(NH):
            acc = acc_ref[h]
            o_ref[:, h * HD:(h + 1) * HD] = (acc[0:HD] / acc[HD:HD + 1]).T


def _flash_call(qT, kA, vT, base_q, base_k, sel, Tp, into, name):
    T = MOBA_BLOCK
    Tq = FLASH_TQ
    ratio = Tq // T
    Kc = kA.shape[2]
    moba = sel is not None
    ti = np.concatenate([np.full((i + 1) * ratio, i, np.int32) for i in range(Tp // Tq)])
    tj = np.concatenate([np.arange((i + 1) * ratio, dtype=np.int32) for i in range(Tp // Tq)])
    in_specs = [pl.BlockSpec((NH, Kc, Tq), lambda s, ti, tj, bq, bk: (0, 0, ti[s])),
                pl.BlockSpec((NH, T, Kc), lambda s, ti, tj, bq, bk: (0, tj[s], 0)),
                pl.BlockSpec((NH, VROWS, T), lambda s, ti, tj, bq, bk: (0, 0, tj[s]))]
    args = [qT, kA, vT]
    if moba:
        in_specs.append(pl.BlockSpec((NH, sel.shape[1], Tq), lambda s, ti, tj, bq, bk: (0, 0, ti[s])))
        args.append(sel)
    in_specs.append(pl.BlockSpec(memory_space=pl.ANY))
    return pl.pallas_call(
        functools.partial(_flash_kernel, moba=moba),
        grid_spec=pltpu.PrefetchScalarGridSpec(
            num_scalar_prefetch=4, grid=(len(ti),), in_specs=in_specs,
            out_specs=pl.BlockSpec((Tq, NH * HD), lambda s, ti, tj, bq, bk: (ti[s], 0)),
            scratch_shapes=[pltpu.VMEM((NH, 1, Tq), F32), pltpu.VMEM((NH, VROWS, Tq), F32)]),
        out_shape=jax.ShapeDtypeStruct(into.shape, F32),
        input_output_aliases={4 + len(args): 0},
        compiler_params=_cp(("arbitrary",)), name=name)(
            jnp.asarray(ti), jnp.asarray(tj), base_q, base_k, *args, into)


def _top3_select(sc, valid, idx_f, axis=1):
    sc = jnp.where(valid, sc, -jnp.inf)
    sel = jnp.zeros(sc.shape, F32)
    for _ in range(MOBA_TOPK):
        mx = jnp.max(sc, axis=axis, keepdims=True)
        idx = jnp.min(jnp.where(sc == mx, idx_f, 1e9), axis=axis, keepdims=True)
        hit = idx_f == idx
        sel = jnp.where(hit & valid, 1.0, sel)
        sc = jnp.where(hit, -jnp.inf, sc)
    return sel


def _blockmean_kernel(k_ref, o_ref):
    o_ref[...] = jnp.sum(k_ref[...], axis=0, keepdims=True) * (1.0 / MOBA_BLOCK)


def _blockmean_call(z, Tp):
    nb = Tp // MOBA_BLOCK
    W = BRANCH_W
    cb = Z_C // W + 1
    return pl.pallas_call(
        _blockmean_kernel, grid=(nb,),
        in_specs=[pl.BlockSpec((MOBA_BLOCK, W), lambda i: (i, cb))],
        out_specs=pl.BlockSpec((None, 1, W), lambda i: (i, 0, 0)),
        out_shape=jax.ShapeDtypeStruct((nb, 1, W), F32),
        compiler_params=_cp(("arbitrary",)), name="moba_kmean")(z)


def _ret_kernel(q_ref, k_ref, v_ref, gd_ref, cos_ref, sin_ref, din_ref, qd_ref, kd_ref, cd_ref, s0_ref, lng_ref,
                *rest, nc):
    y_ref, sfin_ref, S_ref = rest[-3:]
    c = pl.program_id(1)

    @pl.when(c == 0)
    def _():
        S_ref[...] = s0_ref[...]

    q = q_ref[...]
    k = k_ref[...]
    v = v_ref[...]
    gd = gd_ref[...]
    cos = cos_ref[...]
    sin = sin_ref[...]
    lng = lng_ref[...]
    for h in range(NH):
        sl = slice(h * HD, (h + 1) * HD)
        qh = q[:, sl]
        kh = k[:, sl]
        qr = qh * cos + pltpu.roll(qh, HD // 2, axis=1) * sin
        kr = (kh * cos + pltpu.roll(kh, HD // 2, axis=1) * sin) * (HD ** -0.5)
        vb = v[:, sl].astype(BF16)
        qb = qr.astype(BF16)
        att = _dot(qb, kr.astype(BF16), NT) * din_ref[h]
        S = S_ref[h]
        o = _dot(att.astype(BF16), vb) + _dot(qb, S.astype(BF16)) * qd_ref[h]
        S_ref[h] = S * cd_ref[h] + _dot((kr * kd_ref[h]).astype(BF16), vb, TN)
        y_ref[:, sl] = _ln(o, RET_GN_EPS) * lng[:, sl] * (gd[:, sl] * _sigmoid(gd[:, sl]))

    @pl.when(c == nc - 1)
    def _():
        sfin_ref[...] = S_ref[...]


def _ret_tables(C):
    lg = np.log(1.0 - 2.0 ** (-5.0 - np.arange(NH, dtype=np.float32))).astype(np.float32)
    i = np.arange(C, dtype=np.float32)
    diff = i[:, None] - i[None, :]
    din = np.where(diff[None] >= 0, np.exp(np.maximum(diff, 0.0)[None] * lg[:, None, None]), 0.0)
    qd = np.exp((i[None, :] + 1.0) * lg[:, None])
    kd = np.exp((C - 1.0 - i)[None, :] * lg[:, None])
    cd = np.exp(C * lg)
    bc = lambda t: np.ascontiguousarray(np.broadcast_to(t[:, :, None], (NH, C, HD))).astype(np.float32)
    cdb = np.ascontiguousarray(np.broadcast_to(cd[:, None, None], (NH, 1, HD))).astype(np.float32)
    return din.astype(np.float32), bc(qd), bc(kd), cdb


def _rope_tables(pos0, T):
    half = HD // 2
    inv = 1.0 / (ROPE_BASE ** (jnp.arange(half, dtype=F32) / half))
    ang = (pos0 + jnp.arange(T)).astype(F32)[:, None] * inv[None, :]
    cos = jnp.cos(ang)
    sin = jnp.sin(ang)
    return jnp.concatenate([cos, cos], axis=1), jnp.concatenate([-sin, sin], axis=1)


def _ret_call(z, s0, ln_g, l, row0, B, T, C, pos0, into):
    nc = T // C
    W = BRANCH_W
    cb = Z_D // W
    blk0 = row0 // C
    cos, sin = _rope_tables(pos0, T)
    din, qd, kd, cd = _ret_tables(C)
    zs = lambda off: pl.BlockSpec((C, W), lambda b, c: (blk0 + b * nc + c, cb + off))
    tab = pl.BlockSpec((C, HD), lambda b, c: (c, 0))
    full = lambda shape: pl.BlockSpec(shape, lambda b, c: (0,) * len(shape))
    st = pl.BlockSpec((None, NH, HD, HD), lambda b, c: (b, 0, 0, 0))
    return pl.pallas_call(
        functools.partial(_ret_kernel, nc=nc), grid=(B, nc),
        in_specs=[zs(0), zs(1), zs(2), zs(3), tab, tab, full((NH, C, C)), full((NH, C, HD)), full((NH, C, HD)),
                  full((NH, 1, HD)), st, pl.BlockSpec((None, 1, W), lambda b, c: (l, 0, 0)),
                  pl.BlockSpec(memory_space=pl.ANY)],
        out_specs=[pl.BlockSpec((C, W), lambda b, c: (blk0 + b * nc + c, 0)), st],
        out_shape=[jax.ShapeDtypeStruct(into.shape, F32), jax.ShapeDtypeStruct((B, NH, HD, HD), F32)],
        scratch_shapes=[pltpu.VMEM((NH, HD, HD), F32)],
        input_output_aliases={12: 0},
        compiler_params=_cp(("arbitrary", "arbitrary")), name="ret_c%d" % C)(
            z, z, z, z, cos, sin, din, qd, kd, cd, s0, ln_g, into)


def _lfsuf_kernel(pt_ref, lf_hbm, ts_ref, tot_ref, o_ref, buf, sem, *, npg, base):
    b = pl.program_id(0)

    def page_copy(p):
        return pltpu.make_async_copy(lf_hbm.at[pl.ds(NH * (base + pt_ref[b * npg + p]), NH), :],
                                     buf.at[pl.ds(NH * p, NH), :], sem.at[p])

    for p in range(npg):
        page_copy(p).start()
    for p in range(npg):
        page_copy(p).wait()
    loc = None
    tot = None
    for h in range(NH):
        x = buf[pl.ds(h, npg, stride=NH), :]
        a = _dot_exact_r(x, ts_ref[h])
        t = _dot_exact_r(x, tot_ref[h])
        loc = a if loc is None else loc + a
        tot = t if tot is None else tot + t
    later = (_iota((npg, npg), 1) > _iota((npg, npg), 0)).astype(BF16)
    o_ref[...] = loc + _dot_exact_l(later, tot)


def _lfsuf_call(pt_flat, lf_rows, ts, tot, B, npg, base):
    PW = PAGE_SIZE * NH
    sel = pl.BlockSpec((NH, PAGE_SIZE, PW), lambda b, pt: (0, 0, 0))
    return pl.pallas_call(
        functools.partial(_lfsuf_kernel, npg=npg, base=base),
        grid_spec=pltpu.PrefetchScalarGridSpec(
            num_scalar_prefetch=1, grid=(B,),
            in_specs=[pl.BlockSpec(memory_space=pl.ANY), sel, sel],
            out_specs=pl.BlockSpec((None, npg, PW), lambda b, pt: (b, 0, 0)),
            scratch_shapes=[pltpu.VMEM((NH * npg, PAGE_SIZE), F32), pltpu.SemaphoreType.DMA((npg,))]),
        out_shape=jax.ShapeDtypeStruct((B, npg, PW), F32),
        compiler_params=_cp(("arbitrary",)), name="fox_logf_suffix")(pt_flat, lf_rows, ts, tot)


def _suffix_matrices():
    r = np.arange(PAGE_SIZE)
    dst_r = np.repeat(r, NH)
    dst_h = np.tile(np.arange(NH), PAGE_SIZE)
    same = np.arange(NH)[:, None, None] == dst_h[None, None, :]
    ts = same & (r[None, :, None] > dst_r[None, None, :])
    return jnp.asarray(ts, BF16), jnp.asarray(np.broadcast_to(same, ts.shape), BF16)


def _sattn_kernel(pt_ref, fq_ref, fkn_ref, fvn_ref, lfn_ref, mq_ref, mkn_ref, mvn_ref, R_ref, *rest, nb, S, scale,
                  npg, base):
    P = SAMPLE_PAGES
    hbm = rest[0:4]
    (yb_ref, yc_ref, fqb, mqb, mqf, cnb, fm, fl, facc, sc_all, m_all, l_all, o_all,
     kf_buf, vf_buf, km_buf, vm_buf, sem) = rest[-18:]
    bufs = (kf_buf, vf_buf, km_buf, vm_buf)
    n = pl.program_id(1)
    n_steps = pl.num_programs(1)
    total = pl.num_programs(0) * n_steps
    step = pl.program_id(0) * n_steps + n
    PWc = PAGE_SIZE * NH

    def page_copies(st):
        slot = st % RING
        bb = st // n_steps
        nn = st % n_steps
        out = []
        for t in range(4):
            for g in range(P):
                row0 = pl.multiple_of((base + pt_ref[bb * npg + P * nn + g]) * PWc, PWc)
                out.append(pltpu.make_async_copy(hbm[t].at[pl.ds(row0, PWc), :], bufs[t].at[slot, g], sem.at[slot, t]))
        return out

    @pl.when(step == 0)
    def _():
        for ahead in range(RING - 1):
            @pl.when(ahead < total)
            def _():
                for cp in page_copies(step + ahead):
                    cp.start()

    @pl.when(step + (RING - 1) < total)
    def _():
        for cp in page_copies(step + (RING - 1)):
            cp.start()

    for cp in page_copies(step):
        cp.wait()
    slot = step % RING
    fk, fv, mk, mv = ([buf.at[slot, g] for g in range(P)] for buf in bufs)
    R4 = NH * S
    PW = PAGE_SIZE * NH
    own_head = (_iota((R4, PW), 1) % NH) == (_iota((R4, PW), 0) // S)
    lane = _iota((R4, LANES), 1)

    def stack_heads(x):
        return jnp.concatenate([x[:, h * HD:(h + 1) * HD] for h in range(NH)], axis=0)

    def head_sums(kpage):
        return jnp.sum(kpage.reshape(PW // 8, 8, HD), axis=0)

    @pl.when(n == 0)
    def _():
        fqb[...] = stack_heads(fq_ref[...]).astype(BF16)
        mq = stack_heads(mq_ref[...])
        mqb[...] = mq.astype(BF16)
        mqf[...] = mq
        tri = (_iota((S, S), 0) >= _iota((S, S), 1)).astype(BF16)
        cn = _dot_exact_l(tri, lfn_ref[...])
        cnb[...] = jnp.concatenate([jnp.broadcast_to(cn[:, h:h + 1], (S, LANES)) for h in range(NH)], axis=0)
        fm[...] = jnp.full(fm.shape, NEG, F32)
        fl[...] = jnp.zeros(fl.shape, F32)
        facc[...] = jnp.zeros(facc.shape, F32)
        sc_all[...] = jnp.full(sc_all.shape, -jnp.inf, F32)
        m_all[...] = jnp.full(m_all.shape, NEG, F32)
        l_all[...] = jnp.zeros(l_all.shape, F32)

    def fox_update(s_list, v_list):
        m_prev = fm[...]
        m_new = m_prev
        for s in s_list:
            m_new = jnp.maximum(m_new, jnp.max(s, axis=1, keepdims=True))
        alpha = jnp.exp(m_prev - m_new)
        p_list = [jnp.exp(s - m_new) for s in s_list]
        l_new = alpha * fl[...]
        acc = alpha * facc[...]
        for p, v_bf in zip(p_list, v_list):
            l_new = l_new + jnp.sum(p, axis=1, keepdims=True)
            acc = acc + _dot(p.astype(BF16), v_bf)
        fl[...] = l_new
        facc[...] = acc
        fm[...] = m_new

    cn_b = cnb[...]
    cn_w = jnp.concatenate([cn_b] * NH, axis=1)
    fq_b = fqb[...]
    s_fox = [_dot(fq_b, fk[g][...].astype(BF16), NT) * scale for g in range(P)]
    s_fox = [jnp.where(own_head, s_fox[g] + cn_w + R_ref[pl.ds(P * n + g, 1), :], NEG) for g in range(P)]
    fox_update(s_fox, [fv[g][...].astype(BF16) for g in range(P)])

    mq_b = mqb[...]
    mq_f = mqf[...]
    kpg = [mk[g][...] for g in range(P)]
    s_mo = [jnp.where(own_head, _dot(mq_b, kpg[g].astype(BF16), NT) * scale, NEG) for g in range(P)]
    for blk in range(P // 2):
        g0, g1 = 2 * blk, 2 * blk + 1
        ks = head_sums(kpg[g0]) + head_sums(kpg[g1])
        kmean = (ks[0:NH] + ks[NH:2 * NH]) * (1.0 / MOBA_BLOCK)
        kmean = jnp.concatenate([jnp.broadcast_to(kmean[h:h + 1], (S, HD)) for h in range(NH)], axis=0)
        sc_col = jnp.sum(mq_f * kmean, axis=1, keepdims=True)
        m_b = jnp.maximum(jnp.max(s_mo[g0], axis=1, keepdims=True), jnp.max(s_mo[g1], axis=1, keepdims=True))
        p0 = jnp.exp(s_mo[g0] - m_b)
        p1 = jnp.exp(s_mo[g1] - m_b)
        here = lane == (P // 2) * n + blk
        sc_all[...] = jnp.where(here, sc_col, sc_all[...])
        m_all[...] = jnp.where(here, m_b, m_all[...])
        l_all[...] = jnp.where(here, jnp.sum(p0, axis=1, keepdims=True) + jnp.sum(p1, axis=1, keepdims=True),
                               l_all[...])
        o_all[(P // 2) * n + blk] = (_dot(p0.astype(BF16), mv[g0][...].astype(BF16))
                                     + _dot(p1.astype(BF16), mv[g1][...].astype(BF16)))

    @pl.when(n == pl.num_programs(1) - 1)
    def _():
        ri = _iota((R4, R4), 0)
        ci = _iota((R4, R4), 1)
        causal = ((ri // S) == (ci // S)) & ((ci % S) <= (ri % S))
        cn_col = cn_b[:, 0:1]
        cn_row = jnp.sum(jnp.where(ri == ci, jnp.broadcast_to(cn_col, (R4, R4)), 0.0), axis=0, keepdims=True)
        s = _dot(fqb[...], stack_heads(fkn_ref[...]).astype(BF16), NT) * scale
        s = s + (cn_col - cn_row)
        fox_update([jnp.where(causal, s, NEG)], [stack_heads(fvn_ref[...]).astype(BF16)])
        out = facc[...] / fl[...]
        for h in range(NH):
            yb_ref[:, h * HD:(h + 1) * HD] = out[h * S:(h + 1) * S, :]

        s = _dot(mqb[...], stack_heads(mkn_ref[...]).astype(BF16), NT) * scale
        s = jnp.where(causal, s, NEG)
        m_o = jnp.max(s, axis=1, keepdims=True)
        p = jnp.exp(s - m_o)
        l_o = jnp.sum(p, axis=1, keepdims=True)
        o_o = _dot(p.astype(BF16), stack_heads(mvn_ref[...]).astype(BF16))
        sel = _top3_select(sc_all[...], lane < nb, lane.astype(F32)) > 0.5
        m_past = m_all[...]
        m_tot = jnp.maximum(jnp.max(jnp.where(sel, m_past, NEG), axis=1, keepdims=True), m_o)
        w = jnp.where(sel, jnp.exp(m_past - m_tot), 0.0)
        w_o = jnp.exp(m_o - m_tot)
        den = jnp.sum(w * l_all[...], axis=1, keepdims=True) + w_o * l_o
        num = w_o * o_o
        for blk in range(nb):
            num = num + w[:, blk:blk + 1] * o_all[blk]
        out = num / den
        for h in range(NH):
            yc_ref[:, h * HD:(h + 1) * HD] = out[h * S:(h + 1) * S, :]


def _sattn_call(pt_flat, z, lf, R, caches, l, Mp, B, S, npg, n_pool, yb_into, yc_into):
    fk, fv, mk, mv = caches
    W = BRANCH_W
    PW = PAGE_SIZE * NH
    nb = npg // 2
    R4 = NH * S
    rb = Mp // S
    base = l * n_pool
    zrow = lambda cb: pl.BlockSpec((S, W), lambda b, n, pt: (rb + b, cb))
    P = SAMPLE_PAGES
    pages = [pl.BlockSpec(memory_space=pl.ANY)]
    cbB = Z_B // W
    cbC = Z_C // W
    in_specs = [zrow(cbB), zrow(cbB + 1), zrow(cbB + 2),
                pl.BlockSpec((S, LANES), lambda b, n, pt: (rb + b, 0)),
                zrow(cbC), zrow(cbC + 1), zrow(cbC + 2),
                pl.BlockSpec((None, npg, PW), lambda b, n, pt: (b, 0, 0))] + pages * 4
    in_specs += [pl.BlockSpec(memory_space=pl.ANY)] * 2
    n_in = 1 + len(in_specs)
    out_spec = pl.BlockSpec((S, W), lambda b, n, pt: (rb + b, 0))
    scratch = [pltpu.VMEM((R4, HD), BF16), pltpu.VMEM((R4, HD), BF16), pltpu.VMEM((R4, HD), F32),
               pltpu.VMEM((R4, LANES), F32),
               pltpu.VMEM((R4, 1), F32), pltpu.VMEM((R4, 1), F32), pltpu.VMEM((R4, HD), F32),
               pltpu.VMEM((R4, LANES), F32), pltpu.VMEM((R4, LANES), F32), pltpu.VMEM((R4, LANES), F32),
               pltpu.VMEM((nb, R4, HD), F32)]
    scratch += [pltpu.VMEM((RING, P, PW, HD), F32)] * 4 + [pltpu.SemaphoreType.DMA((RING, 4))]
    return pl.pallas_call(
        functools.partial(_sattn_kernel, nb=nb, S=S, scale=HD ** -0.5, npg=npg, base=base),
        grid_spec=pltpu.PrefetchScalarGridSpec(
            num_scalar_prefetch=1, grid=(B, npg // P), in_specs=in_specs,
            out_specs=[out_spec, out_spec], scratch_shapes=scratch),
        out_shape=[jax.ShapeDtypeStruct(yb_into.shape, F32), jax.ShapeDtypeStruct(yc_into.shape, F32)],
        input_output_aliases={n_in - 2: 0, n_in - 1: 1},
        compiler_params=_cp(("arbitrary", "arbitrary")), name="sample_attn")(
            pt_flat, z, z, z, lf, z, z, z, R, fk, fv, mk, mv, yb_into, yc_into)


def _pick_tile(m, cands):
    for t in cands:
        if m % t == 0:
            return t
    raise ValueError("no row tile divides %d" % m)


def _pad_cols(x, n):
    return jnp.pad(x, ((0, 0), (0, n - x.shape[1])))


def kernel(x_prompt, x_sample, c_prompt, c_sample, page_table, cache_fox_k, cache_fox_v, cache_fox_logf, cache_moba_k, cache_moba_v, state_rwkv, state_rwkv_shift, state_ret, w_ada, b_ada, w_in, rwkv_mu, rwkv_w0, rwkv_w2, rwkv_a0, rwkv_a2, rwkv_g2, rwkv_k_k, rwkv_k_a, rwkv_r_k, rwkv_ln_g, rwkv_ln_b, fox_bf, ret_ln_g, w_branch, w_o, ln1_g, ln1_b, ln2_g, ln2_b, w_ffn_gate, w_ffn_in, w_ffn_out):
    Bp, Tp, D = x_prompt.shape
    Bs, S, _ = x_sample.shape
    depth = w_in.shape[0]
    n_pool = cache_fox_k.shape[1]
    npg = page_table.shape[1]
    past = npg * PAGE_SIZE
    Mp, Ms = Bp * Tp, Bs * S
    M = Mp + Ms
    assert Bp == 1 and D == D_MODEL and depth == DEPTH
    assert Mp % TM == 0 and Ms % TM == 0 and Tp % MOBA_BLOCK == 0 and Tp // MOBA_BLOCK <= LANES
    assert Tp % (RWKV_CHUNK * RWKV_CHUNKS_PER_STEP) == 0 and Tp % RET_CHUNK == 0 and S == 8 and past % MOBA_BLOCK == 0
    assert TM == MOBA_BLOCK and 2 * NH == 8 and MOBA_BLOCK == 2 * PAGE_SIZE
    assert Tp % FLASH_TQ == 0 and FLASH_TQ % MOBA_BLOCK == 0
    assert npg % SAMPLE_PAGES == 0 and SAMPLE_PAGES % 2 == 0 and npg // 2 <= LANES
    n_p = Mp // TM
    W = BRANCH_W
    tm_big = _pick_tile(M, (768, 512, 256))
    tm_small = _pick_tile(M, (384, 256))

    x = jnp.concatenate([x_prompt.reshape(Mp, D), x_sample.reshape(Ms, D)], axis=0)
    c_rows = 8 * (-(-(Bp + Bs) // 8))
    c_all = jnp.pad(jnp.concatenate([c_prompt, c_sample], axis=0), ((0, c_rows - Bp - Bs), (0, 0)))
    pt_flat = page_table.reshape(-1).astype(jnp.int32)
    cache2 = lambda t: t.reshape(depth * n_pool * PAGE_SIZE * NH, HD)
    caches = (cache2(cache_fox_k), cache2(cache_fox_v), cache2(cache_moba_k), cache2(cache_moba_v))
    lf_rows = jnp.swapaxes(cache_fox_logf, 2, 3).reshape(depth * n_pool * NH, PAGE_SIZE)
    ts_m, tot_m = _suffix_matrices()
    e64 = jnp.asarray(np.kron(np.eye(A_HEADS), np.ones((A_HD, A_HD))), BF16)
    b_ada3 = b_ada.reshape(depth, 1, -1)
    vec3 = lambda t: t.reshape(depth, 1, -1)
    ln1_g3, ln1_b3, ln2_g3, ln2_b3, ret_g3 = map(vec3, (ln1_g, ln1_b, ln2_g, ln2_b, ret_ln_g))

    assert w_in.shape[2] == _O_G + 4 * D_MODEL
    tm_in = _pick_tile(M, (768, 512, 256))

    per_layer = []
    for l in range(depth):
        ada = _ada_call(c_all, w_ada, b_ada3, l)
        ada_p = ada[0:8]
        mod_s = jnp.repeat(ada[Bp:Bp + Bs], S, axis=0)
        h = _lnmod_call(x, ada_p, mod_s, n_p, 1, 0)
        z = _win_call(h, w_in, l, tm_in)

        fill_s = jnp.repeat(_pad_cols(state_rwkv_shift[l], 2048), S, axis=0)
        pad_rows = lambda w2, r0: jnp.zeros((W, W), F32).at[r0:r0 + w2.shape[0]].set(w2)
        pre = _rwkv_pre_call(
            z, fill_s, _pad_cols(rwkv_mu[l][None], 2048), rwkv_w0[l][None], rwkv_a0[l][None],
            rwkv_k_k[l][None], rwkv_k_a[l][None], pad_rows(rwkv_w2[l], 0), pad_rows(rwkv_a2[l], A_LORA_W),
            pad_rows(rwkv_g2[l], A_LORA_W + A_LORA_A), e64, n_p, S)
        hv = lambda t: t.reshape(A_HEADS, 1, A_HD)
        rk, lg, lb = hv(rwkv_r_k[l]), hv(rwkv_ln_g[l]), hv(rwkv_ln_b[l])
        blank = jnp.zeros((M, W), F32)
        ya, sT_p = _rwkv_chunk_call(pre, jnp.zeros((Bp, A_HEADS, A_HD, A_HD), F32), rk, lg, lb,
                                    0, Bp, Tp, RWKV_CHUNK, RWKV_CHUNKS_PER_STEP, blank)
        ya, sT_s = _rwkv_chunk_call(pre, jnp.swapaxes(state_rwkv[l], -1, -2), rk, lg, lb, Mp, Bs, S, S, 1, ya)

        lf, cum = _foxcum_call(z, _pad_cols(fox_bf[l][None], LANES))
        nbk = Tp // MOBA_BLOCK
        kmean = _blockmean_call(z, Tp).reshape(nbk, W)
        kmean = jnp.pad(kmean, ((0, 8 * (-(-nbk // 8)) - nbk), (0, 0)))
        fqT, fkA, fvT, mqT, mkb, mvT, sel, fk_rows, fv_rows, mk_rows, mv_rows = _attn_prep_call(z, cum, kmean, Tp)
        tile_base = lambda t: jnp.transpose(jnp.concatenate(
            [jnp.zeros((1, NH), F32), cum[t - 1:Mp - 1:t, :NH]], axis=0)).reshape(-1)
        base_q, base_k = tile_base(FLASH_TQ), tile_base(MOBA_BLOCK)
        yb = _flash_call(fqT, fkA, fvT, base_q, base_k, None, Tp, blank, "fox_prompt")
        yc = _flash_call(mqT, mkb, mvT, base_q, base_k, sel, Tp, blank, "moba_prompt")
        R = _lfsuf_call(pt_flat, lf_rows, ts_m, tot_m, Bs, npg, l * n_pool)
        yb, yc = _sattn_call(pt_flat, z, lf, R, caches, l, Mp, Bs, S, npg, n_pool, yb, yc)

        yd, rS_p = _ret_call(z, jnp.zeros((Bp, NH, HD, HD), F32), ret_g3, l, 0, Bp, Tp, RET_CHUNK, 0, blank)
        yd, rS_s = _ret_call(z, state_ret[l], ret_g3, l, Mp, Bs, S, S, past, yd)

        mixin = _merge_call((ya, yb, yc, yd), w_branch, z, l, tm_big, 512)
        mix = _mm_call(mixin, w_o, l, tm_big, 512, F32, "w_o")
        x1, h2 = _resid_call(x, mix, ada_p, mod_s, ln1_g3, ln1_b3, l, n_p, 2, 4, 3)
        act = _ffn_call(h2, w_ffn_gate, w_ffn_in, l, tm_big, 512)
        f = _mm_call(act, w_ffn_out, l, tm_small, 512, F32, "ffn_out")
        x = _resid_call(x1, f, ada_p, mod_s, ln2_g3, ln2_b3, l, n_p, 5)

        zb = Z_B
        zc = Z_C
        hp = lambda t: t.reshape(Bp, Tp, NH, HD)
        hs = lambda t: t.reshape(Bs, S, NH, HD)
        per_layer.append(dict(
            fox_k_p=hp(fk_rows), fox_v_p=hp(fv_rows),
            fox_lf_p=lf[:Mp, :NH].reshape(Bp, Tp, NH),
            moba_k_p=hp(mk_rows), moba_v_p=hp(mv_rows),
            rwkv_S_p=jnp.swapaxes(sT_p, -1, -2), shift_p=z[Mp - 1:Mp, Z_A:Z_A + A_COLS], ret_S_p=rS_p,
            fox_k_s=hs(z[Mp:, zb + W:zb + 2 * W]), fox_v_s=hs(z[Mp:, zb + 2 * W:zb + 3 * W]),
            fox_lf_s=lf[Mp:, :NH].reshape(Bs, S, NH),
            moba_k_s=hs(z[Mp:, zc + W:zc + 2 * W]), moba_v_s=hs(z[Mp:, zc + 2 * W:zc + 3 * W]),
            rwkv_S_s=jnp.swapaxes(sT_s, -1, -2),
            shift_s=z[Mp:, Z_A:Z_A + A_COLS].reshape(Bs, S, A_COLS)[:, S - 1], ret_S_s=rS_s))

    st = lambda name: jnp.stack([p[name] for p in per_layer])
    return (x[:Mp].reshape(Bp, Tp, D), x[Mp:].reshape(Bs, S, D),
            st("fox_k_p"), st("fox_v_p"), st("fox_lf_p"), st("moba_k_p"), st("moba_v_p"),
            st("rwkv_S_p"), st("shift_p"), st("ret_S_p"),
            st("fox_k_s"), st("fox_v_s"), st("fox_lf_s"), st("moba_k_s"), st("moba_v_s"),
            st("rwkv_S_s"), st("shift_s"), st("ret_S_s"))
```
